```python
import jax, jax.numpy as jnp
from jax import lax
import numpy as np

D_MODEL = 1024
BATCH = 8
SEQ = 2048
DEPTH = 1
DEC_BATCH = 128
DEC_SEQ = 4
PAST_LEN = 16384
PAGE_SIZE = 128

MIX_W = D_MODEL
H_GLA = 4
DV_GLA = MIX_W // 2 // H_GLA
DK_GLA = DV_GLA // 2
GLA_RANK = 16
GLA_TAU = 16.0
H_GDN = 4
DK_GDN = MIX_W // 2 // H_GDN
DV_GDN = DK_GDN
GDN_CONV_W = 4
D_FF = ((8 * D_MODEL // 3) + 127) // 128 * 128
FFN_CONV_W = 3
CHUNK = 32
EPS = 1e-6

GLA_QK = H_GLA * DK_GLA
GLA_V = H_GLA * DV_GLA
GDN_QK = H_GDN * DK_GDN
GDN_V = H_GDN * DV_GDN
GDN_CONV_C = 2 * GDN_QK + GDN_V
IN_SIZES = [GLA_QK, GLA_QK, GLA_V, GLA_V, GLA_RANK, GDN_CONV_C, GDN_V, H_GDN, H_GDN]
N_IN = sum(IN_SIZES)
IN_IDX = list(np.cumsum(IN_SIZES)[:-1].tolist())

kernel_name = "hymba_gla_gdn_convffn_step"


def rmsnorm(x, g):
    xf = x.astype(jnp.float32)
    y = xf * lax.rsqrt(jnp.mean(xf * xf, axis=-1, keepdims=True) + EPS) * g.astype(jnp.float32)
    return y.astype(x.dtype)


def l2norm(x):
    xf = x.astype(jnp.float32)
    return xf * lax.rsqrt(jnp.sum(xf * xf, axis=-1, keepdims=True) + EPS)


def causal_dwconv(x, buf, w):
    W = w.shape[0]
    C = x.shape[-1]
    xc = jnp.concatenate([buf.astype(x.dtype), x], axis=1)
    y = lax.conv_general_dilated(xc, w[:, None, :].astype(x.dtype), (1,), 'VALID',
                                 dimension_numbers=('NWC', 'WIO', 'NWC'),
                                 feature_group_count=C)
    return y, xc[:, xc.shape[1] - (W - 1):]


def to_chunks(a):
    B, T = a.shape[0], a.shape[1]
    pad = (-T) % CHUNK
    a = jnp.pad(a, [(0, 0), (0, pad)] + [(0, 0)] * (a.ndim - 2))
    nc = (T + pad) // CHUNK
    a = a.reshape((B, nc, CHUNK) + a.shape[2:])
    perm = (1, 0, 3, 2) + tuple(range(4, a.ndim))
    return a.transpose(perm)


def from_chunks(o, T):
    nc, B, H, C, dv = o.shape
    return o.transpose(1, 0, 3, 2, 4).reshape(B, nc * C, H, dv)[:, :T]


def gla_chunked(q, k, v, log_a, h0):
    T = q.shape[1]
    f32 = jnp.float32
    qc, kc, vc, ac = (to_chunks(t.astype(f32)) for t in (q, k, v, log_a))
    incl = jnp.tril(jnp.ones((CHUNK, CHUNK), bool))

    def step(h, inp):
        qi, ki, vi, ai = inp
        b = jnp.cumsum(ai, axis=-2)
        diff = b[..., :, None, :] - b[..., None, :, :]
        decay = jnp.exp(jnp.where(incl[:, :, None], diff, -jnp.inf))
        attn = jnp.einsum('bhid,bhjd,bhijd->bhij', qi, ki, decay)
        o = (jnp.einsum('bhij,bhjv->bhiv', attn, vi)
             + jnp.einsum('bhid,bhdv->bhiv', qi * jnp.exp(b), h))
        b_last = b[..., -1:, :]
        h_new = (jnp.exp(b_last)[..., 0, :, None] * h
                 + jnp.einsum('bhjd,bhjv->bhdv', ki * jnp.exp(b_last - b), vi))
        return h_new, o

    h_fin, o = lax.scan(step, h0.astype(f32), (qc, kc, vc, ac))
    return from_chunks(o, T), h_fin


def gdn_chunked(q, k, v, g, beta, h0):
    T = q.shape[1]
    f32 = jnp.float32
    qc, kc, vc, gc, bc = (to_chunks(t.astype(f32)) for t in (q, k, v, g, beta))
    incl = jnp.tril(jnp.ones((CHUNK, CHUNK), bool))
    strict = jnp.tril(jnp.ones((CHUNK, CHUNK), bool), -1)
    eye = jnp.eye(CHUNK, dtype=f32)

    def step(h, inp):
        qi, ki, vi, gi, bi = inp
        G = jnp.cumsum(gi, axis=-1)
        diff = G[..., :, None] - G[..., None, :]
        decay = jnp.exp(jnp.where(incl, diff, -jnp.inf))
        kk = jnp.einsum('bhid,bhjd->bhij', ki, ki)
        L = jnp.where(strict, bi[..., :, None] * kk * decay, 0.0)
        eG = jnp.exp(G)[..., None]
        rhs = bi[..., None] * (vi - eG * jnp.einsum('bhid,bhdv->bhiv', ki, h))
        U = lax.linalg.triangular_solve(L + eye, rhs, left_side=True, lower=True,
                                        unit_diagonal=True)
        qk = jnp.einsum('bhid,bhjd->bhij', qi, ki) * decay
        o = eG * jnp.einsum('bhid,bhdv->bhiv', qi, h) + jnp.einsum('bhij,bhjv->bhiv', qk, U)
        G_last = G[..., -1:]
        h_new = (jnp.exp(G_last)[..., None] * h
                 + jnp.einsum('bhjd,bhjv->bhdv', ki * jnp.exp(G_last - G)[..., None], U))
        return h_new, o

    h_fin, o = lax.scan(step, h0.astype(f32), (qc, kc, vc, gc, bc))
    return from_chunks(o, T), h_fin


def trunk(x, st_gla, st_gdn, st_conv, st_ffn, params):
    (norm1_g, w_in, gla_w_a2, gla_b_a, gla_norm_g, gdn_conv_w, gdn_a_log, gdn_dt_bias,
     gdn_norm_g, w_out, norm2_g, w_up, ffn_conv_w, w_down, norm_f_g) = params
    B, T = x.shape[0], x.shape[1]
    dt = x.dtype
    n_gla, n_gdn, n_conv, n_ffn = [], [], [], []
    for l in range(DEPTH):
        hn = rmsnorm(x, norm1_g[l])
        proj = hn @ w_in[l]
        q_a, k_a, v_a, gate_a, low_a, qkv_b, z_b, beta_b, dec_b = jnp.split(proj, IN_IDX, axis=-1)
        q = q_a.reshape(B, T, H_GLA, DK_GLA) * (DK_GLA ** -0.5)
        k = k_a.reshape(B, T, H_GLA, DK_GLA)
        v = v_a.reshape(B, T, H_GLA, DV_GLA)
        log_a = jax.nn.log_sigmoid((low_a @ gla_w_a2[l] + gla_b_a[l]).astype(jnp.float32)) / GLA_TAU
        o_a, h_a = gla_chunked(q, k, v, log_a.reshape(B, T, H_GLA, DK_GLA), st_gla[l])
        o_a = rmsnorm(o_a.astype(dt), gla_norm_g[l]) * jax.nn.silu(gate_a.reshape(B, T, H_GLA, DV_GLA))
        qkv, conv_new = causal_dwconv(qkv_b, st_conv[l], gdn_conv_w[l])
        qkv = jax.nn.silu(qkv)
        qd, kd, vd = jnp.split(qkv, [GDN_QK, 2 * GDN_QK], axis=-1)
        qd = l2norm(qd.reshape(B, T, H_GDN, DK_GDN)) * (DK_GDN ** -0.5)
        kd = l2norm(kd.reshape(B, T, H_GDN, DK_GDN))
        vd = vd.reshape(B, T, H_GDN, DV_GDN)
        g = -jnp.exp(gdn_a_log[l].astype(jnp.float32)) * jax.nn.softplus(
            dec_b.astype(jnp.float32) + gdn_dt_bias[l].astype(jnp.float32))
        beta = jax.nn.sigmoid(beta_b.astype(jnp.float32))
        o_b, h_b = gdn_chunked(qd, kd, vd, g, beta, st_gdn[l])
        o_b = rmsnorm(o_b.astype(dt), gdn_norm_g[l]) * jax.nn.silu(z_b.reshape(B, T, H_GDN, DV_GDN))
        mix = jnp.concatenate([o_a.reshape(B, T, GLA_V), o_b.reshape(B, T, GDN_V)], axis=-1)
        x = x + mix @ w_out[l]
        hn2 = rmsnorm(x, norm2_g[l])
        up, ffn_new = causal_dwconv(hn2 @ w_up[l], st_ffn[l], ffn_conv_w[l])
        a_f, b_f = jnp.split(up, [D_FF], axis=-1)
        x = x + (jax.nn.silu(a_f) * b_f) @ w_down[l]
        n_gla.append(h_a.astype(st_gla.dtype))
        n_gdn.append(h_b.astype(st_gdn.dtype))
        n_conv.append(conv_new.astype(st_conv.dtype))
        n_ffn.append(ffn_new.astype(st_ffn.dtype))
    y = rmsnorm(x, norm_f_g)
    return y, jnp.stack(n_gla), jnp.stack(n_gdn), jnp.stack(n_conv), jnp.stack(n_ffn)


def setup_inputs(seed: int = 0) -> dict:
    key = jax.random.key(seed)
    ks = jax.random.split(key, 24)
    nrm = jax.random.normal
    f32 = jnp.float32
    dt_min, dt_max = 0.001, 0.1
    dt0 = jnp.exp(jax.random.uniform(ks[13], (DEPTH, H_GDN), f32) * (np.log(dt_max) - np.log(dt_min)) + np.log(dt_min))
    return {
        "x_prompt": nrm(ks[0], (BATCH, SEQ, D_MODEL), f32),
        "x_sample": nrm(ks[1], (DEC_BATCH, DEC_SEQ, D_MODEL), f32),
        "state_gla": 0.3 * nrm(ks[2], (DEPTH, DEC_BATCH, H_GLA, DK_GLA, DV_GLA), f32),
        "state_gdn": 0.1 * nrm(ks[3], (DEPTH, DEC_BATCH, H_GDN, DK_GDN, DV_GDN), f32),
        "state_gdn_conv": nrm(ks[4], (DEPTH, DEC_BATCH, GDN_CONV_W - 1, GDN_CONV_C), f32),
        "state_ffn_conv": nrm(ks[5], (DEPTH, DEC_BATCH, FFN_CONV_W - 1, 2 * D_FF), f32),
        "norm1_g": 1.0 + 0.02 * nrm(ks[6], (DEPTH, D_MODEL), f32),
        "w_in": nrm(ks[7], (DEPTH, D_MODEL, N_IN), f32) * D_MODEL ** -0.5,
        "gla_w_a2": nrm(ks[8], (DEPTH, GLA_RANK, GLA_QK), f32) * GLA_RANK ** -0.5,
        "gla_b_a": 0.1 * nrm(ks[9], (DEPTH, GLA_QK), f32),
        "gla_norm_g": 1.0 + 0.02 * nrm(ks[10], (DEPTH, DV_GLA), f32),
        "gdn_conv_w": nrm(ks[11], (DEPTH, GDN_CONV_W, GDN_CONV_C), f32) * GDN_CONV_W ** -0.5,
        "gdn_a_log": jnp.log(jax.random.uniform(ks[12], (DEPTH, H_GDN), f32, 1.0, 16.0)),
        "gdn_dt_bias": dt0 + jnp.log(-jnp.expm1(-dt0)),
        "gdn_norm_g": 1.0 + 0.02 * nrm(ks[14], (DEPTH, DV_GDN), f32),
        "w_out": nrm(ks[15], (DEPTH, MIX_W, D_MODEL), f32) * MIX_W ** -0.5,
        "norm2_g": 1.0 + 0.02 * nrm(ks[16], (DEPTH, D_MODEL), f32),
        "w_up": nrm(ks[17], (DEPTH, D_MODEL, 2 * D_FF), f32) * D_MODEL ** -0.5,
        "ffn_conv_w": nrm(ks[18], (DEPTH, FFN_CONV_W, 2 * D_FF), f32) * FFN_CONV_W ** -0.5,
        "w_down": nrm(ks[19], (DEPTH, D_FF, D_MODEL), f32) * D_FF ** -0.5,
        "norm_f_g": 1.0 + 0.02 * nrm(ks[20], (D_MODEL,), f32),
    }


def reference(x_prompt, x_sample, state_gla, state_gdn, state_gdn_conv, state_ffn_conv,
              norm1_g, w_in, gla_w_a2, gla_b_a, gla_norm_g, gdn_conv_w, gdn_a_log, gdn_dt_bias,
              gdn_norm_g, w_out, norm2_g, w_up, ffn_conv_w, w_down, norm_f_g):
    params = (norm1_g, w_in, gla_w_a2, gla_b_a, gla_norm_g, gdn_conv_w, gdn_a_log, gdn_dt_bias,
              gdn_norm_g, w_out, norm2_g, w_up, ffn_conv_w, w_down, norm_f_g)
    Bp = x_prompt.shape[0]
    dt = x_prompt.dtype
    z_gla = jnp.zeros((DEPTH, Bp) + state_gla.shape[2:], state_gla.dtype)
    z_gdn = jnp.zeros((DEPTH, Bp) + state_gdn.shape[2:], state_gdn.dtype)
    z_conv = jnp.zeros((DEPTH, Bp) + state_gdn_conv.shape[2:], dt)
    z_ffn = jnp.zeros((DEPTH, Bp) + state_ffn_conv.shape[2:], dt)
    y_prompt, p_gla, p_gdn, p_conv, p_ffn = trunk(x_prompt, z_gla, z_gdn, z_conv, z_ffn, params)
    y_sample, s_gla, s_gdn, s_conv, s_ffn = trunk(x_sample, state_gla, state_gdn, state_gdn_conv,
                                                  state_ffn_conv, params)
    return (y_prompt, y_sample, p_gla, p_gdn, p_conv, p_ffn, s_gla, s_gdn, s_conv, s_ffn)
```

```python
import functools

import jax
import jax.numpy as jnp
from jax import lax
from jax.experimental import pallas as pl
from jax.experimental.pallas import tpu as pltpu

F32 = jnp.float32
BF16 = jnp.bfloat16

D_MODEL = 1024
H = 4
GLA_DK = 64
GLA_DV = 128
GLA_RANK = 16
GLA_TAU = 16.0
GDN_D = 128
GDN_CONV_W = 4
D_FF = 2816
FFN_CONV_W = 3
EPS = 1e-6

GLA_QK = H * GLA_DK
GLA_V = H * GLA_DV
GDN_QK = H * GDN_D
GDN_V = H * GDN_D
GDN_CONV_C = 2 * GDN_QK + GDN_V
IN_SIZES = (GLA_QK, GLA_QK, GLA_V, GLA_V, GLA_RANK, GDN_CONV_C, GDN_V, H, H)

LANE = 128
SUBLANE = 8
VMEM_LIMIT = 56 * 1024 * 1024

W_GLA = 2 * GLA_QK + 2 * GLA_V
W_GDN = GDN_CONV_C + GDN_V
W_LOW_OFF = W_GLA + W_GDN
W_BD_OFF = W_LOW_OFF + LANE
W_IN_COLS = W_BD_OFF + LANE
GLA_COLS = W_GLA + GLA_QK
GDN_COLS = W_GDN + LANE

CHUNK = 128
FF_CK = 256


def _dot(a, b):
    return jnp.dot(a, b, preferred_element_type=F32)


def _dot_nt(a, b):
    return lax.dot_general(a, b, (((1,), (1,)), ((), ())), preferred_element_type=F32)


def _split2(x):
    hi = x.astype(BF16)
    lo = (x - hi.astype(F32)).astype(BF16)
    return hi, lo


def _dot3(a, b):
    ah, al = _split2(a)
    bh, bl = _split2(b)
    return _dot(ah, bh) + (_dot(ah, bl) + _dot(al, bh))


def _cumsum_rows(tri_bf16, x):
    h1 = x.astype(BF16)
    r1 = x - h1.astype(F32)
    h2 = r1.astype(BF16)
    h3 = (r1 - h2.astype(F32)).astype(BF16)
    return _dot(tri_bf16, h1) + (_dot(tri_bf16, h2) + _dot(tri_bf16, h3))


def _sigmoid(x):
    return 1.0 / (1.0 + jnp.exp(-x))


def _silu(x):
    return x * _sigmoid(x)


def _softplus(x):
    return jnp.maximum(x, 0.0) + jnp.log(1.0 + jnp.exp(-jnp.abs(x)))


def _rms(x, g):
    ms = jnp.mean(x * x, axis=-1, keepdims=True)
    return x * lax.rsqrt(ms + EPS) * g


def _shift_rows(x, k, prev8):
    rolled = pltpu.roll(x, k, 0)
    prev_rolled = pltpu.roll(prev8, k, 0)
    row = lax.broadcasted_iota(jnp.int32, (SUBLANE, x.shape[1]), 0)
    first = jnp.where(row < k, prev_rolled, rolled[0:SUBLANE])
    return jnp.concatenate([first, rolled[SUBLANE:]], axis=0)


def _inproj_kernel(x_ref, g1_ref, w_ref, wa2_ref, ba_ref, alog_ref, dtb_ref, gla_ref, gdn_ref):
    hn = _rms(x_ref[...], g1_ref[...]).astype(BF16)
    for c in range(0, W_GLA, 512):
        gla_ref[:, c:c + 512] = _dot(hn, w_ref[:, c:c + 512])
    for c in range(0, W_GDN, 512):
        gdn_ref[:, c:c + 512] = _dot(hn, w_ref[:, W_GLA + c:W_GLA + c + 512])
    low = _dot(hn, w_ref[:, W_LOW_OFF:W_LOW_OFF + LANE])
    xa = _dot(low.astype(BF16), wa2_ref[...]) + ba_ref[...]
    gla_ref[:, W_GLA:GLA_COLS] = -_softplus(-xa) * (1.0 / GLA_TAU)
    bd = _dot(hn, w_ref[:, W_BD_OFF:W_BD_OFF + LANE])
    lane = lax.broadcasted_iota(jnp.int32, bd.shape, 1)
    beta = _sigmoid(bd)
    g = -jnp.exp(alog_ref[...]) * _softplus(bd + dtb_ref[...])
    gdn_ref[:, W_GDN:GDN_COLS] = jnp.where(lane < H, beta, jnp.where(lane < 2 * H, g, 0.0))


def _inproj(x2d, g1, w1, wa2, ba, alog, dtb, tm):
    n = x2d.shape[0]
    const = lambda i: (0, 0)
    return pl.pallas_call(
        _inproj_kernel,
        grid=(n // tm,),
        in_specs=[
            pl.BlockSpec((tm, D_MODEL), lambda i: (i, 0)),
            pl.BlockSpec((1, D_MODEL), const),
            pl.BlockSpec((D_MODEL, W_IN_COLS), const),
            pl.BlockSpec((LANE, GLA_QK), const),
            pl.BlockSpec((1, GLA_QK), const),
            pl.BlockSpec((1, LANE), const),
            pl.BlockSpec((1, LANE), const),
        ],
        out_specs=[
            pl.BlockSpec((tm, GLA_COLS), lambda i: (i, 0)),
            pl.BlockSpec((tm, GDN_COLS), lambda i: (i, 0)),
        ],
        out_shape=[
            jax.ShapeDtypeStruct((n, GLA_COLS), F32),
            jax.ShapeDtypeStruct((n, GDN_COLS), F32),
        ],
        compiler_params=pltpu.CompilerParams(
            dimension_semantics=("parallel",), vmem_limit_bytes=VMEM_LIMIT),
        name="inproj",
    )(x2d, g1, w1, wa2, ba, alog, dtb)


def _gla_chunk_kernel(gla_ref, s0_ref, gn_ref, o_ref, st_ref, h_scr, *, nc):
    c = pl.program_id(1)
    tc = CHUNK

    @pl.when(c == 0)
    def _():
        h_scr[...] = jnp.zeros_like(h_scr)
        for h in range(H):
            h_scr[h * GLA_DK:(h + 1) * GLA_DK, h * GLA_DV:(h + 1) * GLA_DV] = s0_ref[0, h]

    q = gla_ref[:, 0:GLA_QK] * (GLA_DK ** -0.5)
    k = gla_ref[:, GLA_QK:2 * GLA_QK]
    v = gla_ref[:, 2 * GLA_QK:2 * GLA_QK + GLA_V]
    la = gla_ref[:, W_GLA:GLA_COLS]

    row = lax.broadcasted_iota(jnp.int32, (tc, tc), 0)
    col = lax.broadcasted_iota(jnp.int32, (tc, tc), 1)
    causal = row >= col
    tri = causal.astype(BF16)

    b = _cumsum_rows(tri, la)
    b_mid = b[tc // 2 - 1:tc // 2, :]
    qm = (q * jnp.exp(b - b_mid)).astype(BF16)
    km = (k * jnp.exp(b_mid - b)).astype(BF16)
    qi = (q * jnp.exp(b)).astype(BF16)

    hbd = h_scr[...]
    o_inter = _dot(qi, hbd.astype(BF16))

    lane_head = lax.broadcasted_iota(jnp.int32, (1, GLA_QK), 1) // GLA_DK
    vb = v.astype(BF16)
    gn = gn_ref[...]
    for h in range(H):
        qh = jnp.where(lane_head == h, qm, jnp.zeros_like(qm))
        s = jnp.where(causal, _dot_nt(qh, km), 0.0)
        sl = slice(h * GLA_DV, (h + 1) * GLA_DV)
        oh = _dot(s.astype(BF16), vb[:, sl]) + o_inter[:, sl]
        gate = gla_ref[:, 2 * GLA_QK + GLA_V + h * GLA_DV:2 * GLA_QK + GLA_V + (h + 1) * GLA_DV]
        o_ref[:, sl] = (_rms(oh, gn) * _silu(gate)).astype(o_ref.dtype)

    bt = b.T
    b_last = bt[:, tc - 1:tc]
    klt = (k.T * jnp.exp(b_last - bt)).astype(BF16)
    upd = _dot(klt, vb)
    rblk = lax.broadcasted_iota(jnp.int32, upd.shape, 0) // GLA_DK
    cblk = lax.broadcasted_iota(jnp.int32, upd.shape, 1) // GLA_DV
    h_new = hbd * jnp.exp(b_last) + jnp.where(rblk == cblk, upd, 0.0)
    h_scr[...] = h_new

    @pl.when(c == nc - 1)
    def _():
        for h in range(H):
            st_ref[0, h] = h_new[h * GLA_DK:(h + 1) * GLA_DK, h * GLA_DV:(h + 1) * GLA_DV]


def _gla_prompt(gla2d, s0, gn, bsz, t):
    nc = t // CHUNK
    return pl.pallas_call(
        functools.partial(_gla_chunk_kernel, nc=nc),
        grid=(bsz, nc),
        in_specs=[
            pl.BlockSpec((CHUNK, GLA_COLS), lambda b, c: (b * nc + c, 0)),
            pl.BlockSpec((1, H, GLA_DK, GLA_DV), lambda b, c: (b, 0, 0, 0)),
            pl.BlockSpec((1, GLA_DV), lambda b, c: (0, 0)),
        ],
        out_specs=[
            pl.BlockSpec((CHUNK, GLA_V), lambda b, c: (b * nc + c, 0)),
            pl.BlockSpec((1, H, GLA_DK, GLA_DV), lambda b, c: (b, 0, 0, 0)),
        ],
        out_shape=[
            jax.ShapeDtypeStruct((bsz * t, GLA_V), BF16),
            jax.ShapeDtypeStruct((bsz, H, GLA_DK, GLA_DV), F32),
        ],
        scratch_shapes=[pltpu.VMEM((GLA_QK, GLA_V), F32)],
        compiler_params=pltpu.CompilerParams(
            dimension_semantics=("parallel", "arbitrary"), vmem_limit_bytes=VMEM_LIMIT),
        name="gla_prompt",
    )(gla2d, s0, gn)


def _tri_inverse(l_strict, eye):
    m = -l_strict
    p = eye + m
    n = l_strict.shape[0]
    step = 2
    while step < n:
        m = _dot3(m, m)
        p = p + _dot3(p, m)
        step *= 2
    return p


def _gdn_chunk_kernel(gdn_ref, s0_ref, cw_ref, gn_ref, o_ref, st_ref, h_scr, xprev_scr, *, nc):
    c = pl.program_id(1)
    tc = CHUNK

    @pl.when(c == 0)
    def _():
        xprev_scr[...] = jnp.zeros_like(xprev_scr)
        for h in range(H):
            h_scr[h * GDN_D:(h + 1) * GDN_D, :] = s0_ref[0, h]

    x = gdn_ref[:, 0:GDN_CONV_C]
    prev8 = xprev_scr[...]
    y = x * cw_ref[3:4, :]
    for kk in range(1, GDN_CONV_W):
        y = y + _shift_rows(x, kk, prev8) * cw_ref[3 - kk:4 - kk, :]
    xprev_scr[...] = x[tc - SUBLANE:tc, :]
    qkv = _silu(y)

    bg = gdn_ref[:, W_GDN:GDN_COLS]
    row = lax.broadcasted_iota(jnp.int32, (tc, tc), 0)
    col = lax.broadcasted_iota(jnp.int32, (tc, tc), 1)
    causal = row >= col
    strict = row > col
    eye = (row == col).astype(F32)
    gc = _cumsum_rows(causal.astype(BF16), bg)
    gt = gc.T
    gn = gn_ref[...]

    for h in range(H):
        sl = slice(h * GDN_D, (h + 1) * GDN_D)
        qh = qkv[:, sl]
        kh = qkv[:, GDN_QK + h * GDN_D:GDN_QK + (h + 1) * GDN_D]
        vh = qkv[:, 2 * GDN_QK + h * GDN_D:2 * GDN_QK + (h + 1) * GDN_D]
        qh = qh * lax.rsqrt(jnp.sum(qh * qh, axis=-1, keepdims=True) + EPS) * (GDN_D ** -0.5)
        kh = kh * lax.rsqrt(jnp.sum(kh * kh, axis=-1, keepdims=True) + EPS)
        beta = bg[:, h:h + 1]
        gcol = gc[:, H + h:H + h + 1]
        grow = gt[H + h:H + h + 1, :]
        decay = jnp.where(causal, jnp.exp(jnp.where(causal, gcol - grow, 0.0)), 0.0)
        kb = kh.astype(BF16)
        qb = qh.astype(BF16)
        lmat = jnp.where(strict, beta * _dot_nt(kb, kb) * decay, 0.0)
        tinv = _tri_inverse(lmat, eye)
        eg = jnp.exp(gcol)
        hh = h_scr[sl, :]
        hb = hh.astype(BF16)
        rhs = beta * (vh - eg * _dot(kb, hb))
        u = _dot3(tinv, rhs)
        ub = u.astype(BF16)
        qk = (_dot_nt(qb, kb) * decay).astype(BF16)
        o = eg * _dot(qb, hb) + _dot(qk, ub)
        g_last = gcol[tc - 1:tc, :]
        kd = (kh * jnp.exp(g_last - gcol)).T.astype(BF16)
        h_new = jnp.exp(g_last) * hh + _dot(kd, ub)
        h_scr[sl, :] = h_new
        z = gdn_ref[:, GDN_CONV_C + h * GDN_D:GDN_CONV_C + (h + 1) * GDN_D]
        o_ref[:, sl] = (_rms(o, gn) * _silu(z)).astype(o_ref.dtype)

        @pl.when(c == nc - 1)
        def _():
            st_ref[0, h] = h_new


def _gdn_prompt(gdn2d, s0, cw, gn, bsz, t):
    nc = t // CHUNK
    return pl.pallas_call(
        functools.partial(_gdn_chunk_kernel, nc=nc),
        grid=(bsz, nc),
        in_specs=[
            pl.BlockSpec((CHUNK, GDN_COLS), lambda b, c: (b * nc + c, 0)),
            pl.BlockSpec((1, H, GDN_D, GDN_D), lambda b, c: (b, 0, 0, 0)),
            pl.BlockSpec((GDN_CONV_W, GDN_CONV_C), lambda b, c: (0, 0)),
            pl.BlockSpec((1, GDN_D), lambda b, c: (0, 0)),
        ],
        out_specs=[
            pl.BlockSpec((CHUNK, GDN_V), lambda b, c: (b * nc + c, 0)),
            pl.BlockSpec((1, H, GDN_D, GDN_D), lambda b, c: (b, 0, 0, 0)),
        ],
        out_shape=[
            jax.ShapeDtypeStruct((bsz * t, GDN_V), BF16),
            jax.ShapeDtypeStruct((bsz, H, GDN_D, GDN_D), F32),
        ],
        scratch_shapes=[pltpu.VMEM((H * GDN_D, GDN_D), F32),
                        pltpu.VMEM((SUBLANE, GDN_CONV_C), F32)],
        compiler_params=pltpu.CompilerParams(
            dimension_semantics=("parallel", "arbitrary"), vmem_limit_bytes=VMEM_LIMIT),
        name="gdn_prompt",
    )(gdn2d, s0, cw, gn)


def _mix_sample_kernel(gla_ref, gdn_ref, xc_ref, sa_ref, sb_ref, cw_ref, gna_ref, gnb_ref,
                       oa_ref, ob_ref, sta_ref, stb_ref, pad_scr, *, t, group):
    @pl.when(pl.program_id(0) == 0)
    def _():
        pad_scr[...] = jnp.zeros_like(pad_scr)

    gna = gna_ref[...]
    gnb = gnb_ref[...]
    for s in range(group):
        pad_scr[0:t, 0:2 * GLA_QK] = gla_ref[s, :, 0:2 * GLA_QK]
        pad_scr[0:t, 2 * GLA_QK:3 * GLA_QK] = gla_ref[s, :, W_GLA:GLA_COLS]
        cols = pad_scr[:, 0:3 * GLA_QK].T
        q_t = cols[0:GLA_QK] * (GLA_DK ** -0.5)
        k_t = cols[GLA_QK:2 * GLA_QK]
        a_t = jnp.exp(cols[2 * GLA_QK:3 * GLA_QK])
        hst = jnp.concatenate([sa_ref[s, h] for h in range(H)], axis=0)
        for tt in range(t):
            vrep = jnp.concatenate(
                [jnp.broadcast_to(gla_ref[s, tt:tt + 1, 2 * GLA_QK + h * GLA_DV:2 * GLA_QK + (h + 1) * GLA_DV],
                                  (GLA_DK, GLA_DV)) for h in range(H)], axis=0)
            hst = a_t[:, tt:tt + 1] * hst + k_t[:, tt:tt + 1] * vrep
            qh = q_t[:, tt:tt + 1] * hst
            for h in range(H):
                o = jnp.sum(qh[h * GLA_DK:(h + 1) * GLA_DK], axis=0, keepdims=True)
                gate = gla_ref[s, tt:tt + 1, 2 * GLA_QK + GLA_V + h * GLA_DV:2 * GLA_QK + GLA_V + (h + 1) * GLA_DV]
                oa_ref[s, tt:tt + 1, h * GLA_DV:(h + 1) * GLA_DV] = (
                    _rms(o, gna) * _silu(gate)).astype(oa_ref.dtype)
        for h in range(H):
            sta_ref[s, h] = hst[h * GLA_DK:(h + 1) * GLA_DK]

        y = xc_ref[s, 0:t, :] * cw_ref[0:1, :]
        for w in range(1, GDN_CONV_W):
            y = y + xc_ref[s, w:w + t, :] * cw_ref[w:w + 1, :]
        qkv = _silu(y)
        for h in range(H):
            sl = slice(h * GDN_D, (h + 1) * GDN_D)
            qh = qkv[:, sl]
            kh = qkv[:, GDN_QK + h * GDN_D:GDN_QK + (h + 1) * GDN_D]
            qh = qh * lax.rsqrt(jnp.sum(qh * qh, axis=-1, keepdims=True) + EPS) * (GDN_D ** -0.5)
            kh = kh * lax.rsqrt(jnp.sum(kh * kh, axis=-1, keepdims=True) + EPS)
            pad_scr[0:t, sl] = qh
            pad_scr[0:t, GDN_QK + h * GDN_D:GDN_QK + (h + 1) * GDN_D] = kh
        cols = pad_scr[:, 0:2 * GDN_QK].T
        bg = gdn_ref[s, :, W_GDN:GDN_COLS]
        ebg = jnp.exp(bg)
        for h in range(H):
            sl = slice(h * GDN_D, (h + 1) * GDN_D)
            hh = sb_ref[s, h]
            for tt in range(t):
                kcol = cols[GDN_QK + h * GDN_D:GDN_QK + (h + 1) * GDN_D, tt:tt + 1]
                qcol = cols[h * GDN_D:(h + 1) * GDN_D, tt:tt + 1]
                hh = ebg[tt:tt + 1, H + h:H + h + 1] * hh
                hk = jnp.sum(kcol * hh, axis=0, keepdims=True)
                vrow = qkv[tt:tt + 1, 2 * GDN_QK + h * GDN_D:2 * GDN_QK + (h + 1) * GDN_D]
                u = bg[tt:tt + 1, h:h + 1] * (vrow - hk)
                hh = hh + kcol * u
                o = jnp.sum(qcol * hh, axis=0, keepdims=True)
                z = gdn_ref[s, tt:tt + 1, GDN_CONV_C + h * GDN_D:GDN_CONV_C + (h + 1) * GDN_D]
                ob_ref[s, tt:tt + 1, sl] = (_rms(o, gnb) * _silu(z)).astype(ob_ref.dtype)
            stb_ref[s, h] = hh


def _mix_sample(gla3d, gdn3d, xc, sa, sb, cw, gna, gnb, group):
    bsz, t = gla3d.shape[0], gla3d.shape[1]
    i3 = lambda i: (i, 0, 0)
    i4 = lambda i: (i, 0, 0, 0)
    c2 = lambda i: (0, 0)
    return pl.pallas_call(
        functools.partial(_mix_sample_kernel, t=t, group=group),
        grid=(bsz // group,),
        in_specs=[
            pl.BlockSpec((group, t, GLA_COLS), i3),
            pl.BlockSpec((group, t, GDN_COLS), i3),
            pl.BlockSpec((group, SUBLANE, GDN_CONV_C), i3),
            pl.BlockSpec((group, H, GLA_DK, GLA_DV), i4),
            pl.BlockSpec((group, H, GDN_D, GDN_D), i4),
            pl.BlockSpec((GDN_CONV_W, GDN_CONV_C), c2),
            pl.BlockSpec((1, GLA_DV), c2),
            pl.BlockSpec((1, GDN_D), c2),
        ],
        out_specs=[
            pl.BlockSpec((group, t, GLA_V), i3),
            pl.BlockSpec((group, t, GDN_V), i3),
            pl.BlockSpec((group, H, GLA_DK, GLA_DV), i4),
            pl.BlockSpec((group, H, GDN_D, GDN_D), i4),
        ],
        out_shape=[
            jax.ShapeDtypeStruct((bsz, t, GLA_V), F32),
            jax.ShapeDtypeStruct((bsz, t, GDN_V), F32),
            jax.ShapeDtypeStruct((bsz, H, GLA_DK, GLA_DV), F32),
            jax.ShapeDtypeStruct((bsz, H, GDN_D, GDN_D), F32),
        ],
        scratch_shapes=[pltpu.VMEM((LANE, GLA_COLS), F32)],
        compiler_params=pltpu.CompilerParams(
            dimension_semantics=("arbitrary",), vmem_limit_bytes=VMEM_LIMIT),
        name="mix_sample",
    )(gla3d, gdn3d, xc, sa, sb, cw, gna, gnb)


def _ffn_kernel(*refs, seq_tiles, has_hist, seq_len):
    if has_hist:
        (x_ref, oa_ref, ob_ref, wo_ref, g2_ref, wu_ref, cw_ref, wd_ref, gf_ref, h1_ref, h2_ref,
         y_ref, u_ref) = refs
        carry = None
    else:
        (x_ref, oa_ref, ob_ref, wo_ref, g2_ref, wu_ref, cw_ref, wd_ref, gf_ref,
         y_ref, tail_ref, carry) = refs
    tm = x_ref.shape[0]

    if not has_hist:
        @pl.when(pl.program_id(0) % seq_tiles == 0)
        def _():
            carry[...] = jnp.zeros_like(carry)

    x1 = (x_ref[...] + _dot(oa_ref[...].astype(BF16), wo_ref[0:GLA_V, :])
          + _dot(ob_ref[...].astype(BF16), wo_ref[GLA_V:GLA_V + GDN_V, :]))
    hn = _rms(x1, g2_ref[...]).astype(BF16)

    if has_hist:
        pos = lax.broadcasted_iota(jnp.int32, (tm, FF_CK), 0) % seq_len

    def conv(u, off):
        cols = slice(off, off + FF_CK)
        if has_hist:
            m1 = jnp.where(pos >= 1, pltpu.roll(u, 1, 0), h1_ref[:, cols])
            m2 = jnp.where(pos >= 2, pltpu.roll(u, 2, 0), h2_ref[:, cols])
            u_ref[:, cols] = u
        else:
            prev8 = carry[:, cols]
            m1 = _shift_rows(u, 1, prev8)
            m2 = _shift_rows(u, 2, prev8)
            tail = u[tm - SUBLANE:tm, :]
            carry[:, cols] = tail
            tail_ref[0, :, cols] = tail
        return (m2 * cw_ref[0:1, cols] + m1 * cw_ref[1:2, cols] + u * cw_ref[2:3, cols])

    acc = jnp.zeros((tm, D_MODEL), F32)
    for c in range(0, D_FF, FF_CK):
        a = conv(_dot(hn, wu_ref[:, c:c + FF_CK]), c)
        b = conv(_dot(hn, wu_ref[:, D_FF + c:D_FF + c + FF_CK]), D_FF + c)
        act = (_silu(a) * b).astype(BF16)
        acc = acc + _dot(act, wd_ref[c:c + FF_CK, :])
    y_ref[...] = _rms(x1 + acc, gf_ref[...])


def _ffn(x2d, oa, ob, wo, g2, wu, cw, wd, gf, tm, seq_len, hist=None):
    n = x2d.shape[0]
    has_hist = hist is not None
    row = lambda i: (i, 0)
    const = lambda i: (0, 0)
    in_specs = [
        pl.BlockSpec((tm, D_MODEL), row),
        pl.BlockSpec((tm, GLA_V), row),
        pl.BlockSpec((tm, GDN_V), row),
        pl.BlockSpec((GLA_V + GDN_V, D_MODEL), const),
        pl.BlockSpec((1, D_MODEL), const),
        pl.BlockSpec((D_MODEL, 2 * D_FF), const),
        pl.BlockSpec((FFN_CONV_W, 2 * D_FF), const),
        pl.BlockSpec((D_FF, D_MODEL), const),
        pl.BlockSpec((1, D_MODEL), const),
    ]
    args = [x2d, oa, ob, wo, g2, wu, cw, wd, gf]
    out_specs = [pl.BlockSpec((tm, D_MODEL), row)]
    out_shape = [jax.ShapeDtypeStruct((n, D_MODEL), F32)]
    scratch = []
    if has_hist:
        assert tm % seq_len == 0
        seq_tiles = 1
        in_specs += [pl.BlockSpec((tm, 2 * D_FF), row)] * 2
        args += list(hist)
        out_specs.append(pl.BlockSpec((tm, 2 * D_FF), row))
        out_shape.append(jax.ShapeDtypeStruct((n, 2 * D_FF), F32))
    else:
        assert seq_len % tm == 0
        seq_tiles = seq_len // tm
        out_specs.append(pl.BlockSpec((1, SUBLANE, 2 * D_FF), lambda i: (i, 0, 0)))
        out_shape.append(jax.ShapeDtypeStruct((n // tm, SUBLANE, 2 * D_FF), F32))
        scratch.append(pltpu.VMEM((SUBLANE, 2 * D_FF), F32))
    return pl.pallas_call(
        functools.partial(_ffn_kernel, seq_tiles=seq_tiles, has_hist=has_hist, seq_len=seq_len),
        grid=(n // tm,),
        in_specs=in_specs,
        out_specs=out_specs,
        out_shape=out_shape,
        scratch_shapes=scratch,
        compiler_params=pltpu.CompilerParams(
            dimension_semantics=("arbitrary",), vmem_limit_bytes=VMEM_LIMIT),
        name="ffn_sample" if has_hist else "ffn_prompt",
    )(*args)


def _pad_cols(a, width):
    return jnp.pad(a, ((0, 0), (0, width - a.shape[1])))


def kernel(x_prompt, x_sample, state_gla, state_gdn, state_gdn_conv, state_ffn_conv, norm1_g, w_in, gla_w_a2, gla_b_a, gla_norm_g, gdn_conv_w, gdn_a_log, gdn_dt_bias, gdn_norm_g, w_out, norm2_g, w_up, ffn_conv_w, w_down, norm_f_g):
    assert w_in.shape[0] == 1, "single layer"
    bp, tp, _ = x_prompt.shape
    bs, ts, _ = x_sample.shape

    offs = [0]
    for s in IN_SIZES:
        offs.append(offs[-1] + s)
    seg = [w_in[0][:, offs[i]:offs[i + 1]] for i in range(len(IN_SIZES))]
    w1 = jnp.concatenate(
        seg[0:4] + [seg[5], seg[6], _pad_cols(seg[4], LANE),
                    _pad_cols(jnp.concatenate([seg[7], seg[8]], axis=1), LANE)], axis=1).astype(BF16)
    wa2 = jnp.pad(gla_w_a2[0], ((0, LANE - GLA_RANK), (0, 0))).astype(BF16)
    ba = gla_b_a[0][None, :]
    alog = jnp.pad(gdn_a_log[0], (H, LANE - 2 * H))[None, :]
    dtb = jnp.pad(gdn_dt_bias[0], (H, LANE - 2 * H))[None, :]
    g1 = norm1_g[0][None, :]
    g2 = norm2_g[0][None, :]
    gf = norm_f_g[None, :]
    gna = gla_norm_g[0][None, :]
    gnb = gdn_norm_g[0][None, :]
    cwb = gdn_conv_w[0]
    cwf = ffn_conv_w[0]
    wo = w_out[0].astype(BF16)
    wu = w_up[0].astype(BF16)
    wd = w_down[0].astype(BF16)

    xp = x_prompt.reshape(bp * tp, D_MODEL)
    gla_p, gdn_p = _inproj(xp, g1, w1, wa2, ba, alog, dtb, tm=256)
    oa_p, p_gla = _gla_prompt(gla_p, jnp.zeros((bp,) + state_gla.shape[2:], F32), gna, bp, tp)
    ob_p, p_gdn = _gdn_prompt(gdn_p, jnp.zeros((bp,) + state_gdn.shape[2:], F32), cwb, gnb, bp, tp)
    tm_p = 512
    y_p, tail_p = _ffn(xp, oa_p, ob_p, wo, g2, wu, cwf, wd, gf, tm=tm_p, seq_len=tp)
    y_prompt = y_p.reshape(bp, tp, D_MODEL)
    p_conv = gdn_p.reshape(bp, tp, GDN_COLS)[:, tp - (GDN_CONV_W - 1):, :GDN_CONV_C]
    p_ffn = tail_p.reshape(bp, tp // tm_p, SUBLANE, 2 * D_FF)[:, -1, SUBLANE - (FFN_CONV_W - 1):, :]

    xs = x_sample.reshape(bs * ts, D_MODEL)
    gla_s, gdn_s = _inproj(xs, g1, w1, wa2, ba, alog, dtb, tm=256)
    gdn_s3 = gdn_s.reshape(bs, ts, GDN_COLS)
    xc = jnp.concatenate(
        [state_gdn_conv[0], gdn_s3[:, :, :GDN_CONV_C],
         jnp.zeros((bs, SUBLANE - ts - (GDN_CONV_W - 1), GDN_CONV_C), F32)], axis=1)
    oa_s, ob_s, s_gla, s_gdn = _mix_sample(
        gla_s.reshape(bs, ts, GLA_COLS), gdn_s3, xc, state_gla[0], state_gdn[0], cwb, gna, gnb, group=4)
    hist = state_ffn_conv[0]
    zrow = jnp.zeros((bs, 1, 2 * D_FF), F32)
    hist_m1 = jnp.concatenate([hist[:, 1:2]] + [zrow] * (ts - 1), axis=1).reshape(bs * ts, 2 * D_FF)
    hist_m2 = jnp.concatenate([hist] + [zrow] * (ts - 2), axis=1).reshape(bs * ts, 2 * D_FF)
    y_s, u_s = _ffn(xs, oa_s.reshape(bs * ts, GLA_V), ob_s.reshape(bs * ts, GDN_V), wo, g2, wu, cwf, wd,
                    gf, tm=128, seq_len=ts, hist=(hist_m1, hist_m2))
    y_sample = y_s.reshape(bs, ts, D_MODEL)
    s_conv = xc[:, ts:ts + GDN_CONV_W - 1]
    s_ffn = u_s.reshape(bs, ts, 2 * D_FF)[:, ts - (FFN_CONV_W - 1):]

    return (y_prompt, y_sample, p_gla[None], p_gdn[None], p_conv[None], p_ffn[None],
            s_gla[None], s_gdn[None], s_conv[None], s_ffn[None])
```

```python
import functools

import jax
import jax.numpy as jnp
from jax import lax
from jax.experimental import pallas as pl
from jax.experimental.pallas import tpu as pltpu

F32 = jnp.float32
BF16 = jnp.bfloat16

D_MODEL = 1024
H = 4
GLA_DK = 64
GLA_DV = 128
GLA_RANK = 16
GLA_TAU = 16.0
GDN_D = 128
GDN_CONV_W = 4
D_FF = 2816
FFN_CONV_W = 3
EPS = 1e-6

GLA_QK = H * GLA_DK
GLA_V = H * GLA_DV
GDN_QK = H * GDN_D
GDN_V = H * GDN_D
GDN_CONV_C = 2 * GDN_QK + GDN_V
IN_SIZES = (GLA_QK, GLA_QK, GLA_V, GLA_V, GLA_RANK, GDN_CONV_C, GDN_V, H, H)

LANE = 128
SUBLANE = 8
VMEM_LIMIT = 56 * 1024 * 1024

W_GLA = 2 * GLA_QK + 2 * GLA_V
W_GDN = GDN_CONV_C + GDN_V
W_LOW_OFF = W_GLA + W_GDN
W_BD_OFF = W_LOW_OFF + LANE
W_IN_COLS = W_BD_OFF + LANE
GLA_COLS = W_GLA + GLA_QK
GDN_COLS = W_GDN + LANE

CHUNK = 128
FF_CK = 256
FF_LOOKAHEAD = 2


def _dot(a, b):
    return jnp.dot(a, b, preferred_element_type=F32)


def _dot_nt(a, b):
    return lax.dot_general(a, b, (((1,), (1,)), ((), ())), preferred_element_type=F32)


def _split2(x):
    hi = x.astype(BF16)
    lo = (x - hi.astype(F32)).astype(BF16)
    return hi, lo


def _dot3(a, b):
    ah, al = _split2(a)
    bh, bl = _split2(b)
    return _dot(jnp.concatenate([ah, ah, al], axis=1), jnp.concatenate([bh, bl, bh], axis=0))


def _dot3_many(a_list, b_list):
    lhs = [jnp.concatenate([ah, ah, al], axis=1) for ah, al in map(_split2, a_list)]
    rhs = [jnp.concatenate([bh, bl, bh], axis=0) for bh, bl in map(_split2, b_list)]
    return [_dot(x, y) for x, y in zip(lhs, rhs)]


def _cumsum_rows(tri_bf16, x):
    h1 = x.astype(BF16)
    r1 = x - h1.astype(F32)
    h2 = r1.astype(BF16)
    h3 = (r1 - h2.astype(F32)).astype(BF16)
    return _dot(jnp.concatenate([tri_bf16] * 3, axis=1), jnp.concatenate([h1, h2, h3], axis=0))


def _sigmoid(x):
    return 1.0 / (1.0 + jnp.exp(-x))


def _silu(x):
    return x * _sigmoid(x)


def _softplus(x):
    return jnp.maximum(x, 0.0) + jnp.log(1.0 + jnp.exp(-jnp.abs(x)))


def _rms(x, g):
    ms = jnp.mean(x * x, axis=-1, keepdims=True)
    return x * lax.rsqrt(ms + EPS) * g


def _shift_rows(x, k, prev8):
    rolled = pltpu.roll(x, k, 0)
    prev_rolled = pltpu.roll(prev8, k, 0)
    row = lax.broadcasted_iota(jnp.int32, (SUBLANE, x.shape[1]), 0)
    first = jnp.where(row < k, prev_rolled, rolled[0:SUBLANE])
    return jnp.concatenate([first, rolled[SUBLANE:]], axis=0)


def _inproj_kernel(x_ref, g1_ref, w_ref, wa2_ref, ba_ref, alog_ref, dtb_ref, gla_ref, gdn_ref):
    hn = _rms(x_ref[...], g1_ref[...]).astype(BF16)
    for c in range(0, W_GLA, 512):
        gla_ref[:, c:c + 512] = _dot(hn, w_ref[:, c:c + 512])
    for c in range(0, W_GDN, 512):
        gdn_ref[:, c:c + 512] = _dot(hn, w_ref[:, W_GLA + c:W_GLA + c + 512])
    low = _dot(hn, w_ref[:, W_LOW_OFF:W_LOW_OFF + LANE])
    xa = _dot(low.astype(BF16), wa2_ref[...]) + ba_ref[...]
    gla_ref[:, W_GLA:GLA_COLS] = -_softplus(-xa) * (1.0 / GLA_TAU)
    bd = _dot(hn, w_ref[:, W_BD_OFF:W_BD_OFF + LANE])
    lane = lax.broadcasted_iota(jnp.int32, bd.shape, 1)
    beta = _sigmoid(bd)
    g = -jnp.exp(alog_ref[...]) * _softplus(bd + dtb_ref[...])
    gdn_ref[:, W_GDN:GDN_COLS] = jnp.where(lane < H, beta, jnp.where(lane < 2 * H, g, 0.0))


def _inproj(x2d, g1, w1, wa2, ba, alog, dtb, tm):
    n = x2d.shape[0]
    const = lambda i: (0, 0)
    return pl.pallas_call(
        _inproj_kernel,
        grid=(n // tm,),
        in_specs=[
            pl.BlockSpec((tm, D_MODEL), lambda i: (i, 0)),
            pl.BlockSpec((1, D_MODEL), const),
            pl.BlockSpec((D_MODEL, W_IN_COLS), const),
            pl.BlockSpec((LANE, GLA_QK), const),
            pl.BlockSpec((1, GLA_QK), const),
            pl.BlockSpec((1, LANE), const),
            pl.BlockSpec((1, LANE), const),
        ],
        out_specs=[
            pl.BlockSpec((tm, GLA_COLS), lambda i: (i, 0)),
            pl.BlockSpec((tm, GDN_COLS), lambda i: (i, 0)),
        ],
        out_shape=[
            jax.ShapeDtypeStruct((n, GLA_COLS), F32),
            jax.ShapeDtypeStruct((n, GDN_COLS), F32),
        ],
        compiler_params=pltpu.CompilerParams(
            dimension_semantics=("parallel",), vmem_limit_bytes=VMEM_LIMIT),
        name="inproj",
    )(x2d, g1, w1, wa2, ba, alog, dtb)


def _gla_chunk_kernel(gla_ref, s0_ref, gn_ref, o_ref, st_ref, h_scr, *, nc):
    c = pl.program_id(1)
    tc = CHUNK

    @pl.when(c == 0)
    def _():
        h_scr[...] = jnp.zeros_like(h_scr)
        for h in range(H):
            h_scr[h * GLA_DK:(h + 1) * GLA_DK, h * GLA_DV:(h + 1) * GLA_DV] = s0_ref[0, h]

    q = gla_ref[:, 0:GLA_QK] * (GLA_DK ** -0.5)
    k = gla_ref[:, GLA_QK:2 * GLA_QK]
    v = gla_ref[:, 2 * GLA_QK:2 * GLA_QK + GLA_V]
    la = gla_ref[:, W_GLA:GLA_COLS]

    row = lax.broadcasted_iota(jnp.int32, (tc, tc), 0)
    col = lax.broadcasted_iota(jnp.int32, (tc, tc), 1)
    causal = row >= col
    tri = causal.astype(BF16)

    b = _cumsum_rows(tri, la)
    b_mid = b[tc // 2 - 1:tc // 2, :]
    qm = (q * jnp.exp(b - b_mid)).astype(BF16)
    km = (k * jnp.exp(b_mid - b)).astype(BF16)
    qi = (q * jnp.exp(b)).astype(BF16)

    hbd = h_scr[...]
    o_inter = _dot(qi, hbd.astype(BF16))

    lane_head = lax.broadcasted_iota(jnp.int32, (1, GLA_QK), 1) // GLA_DK
    vb = v.astype(BF16)
    gn = gn_ref[...]
    scores = [_dot_nt(jnp.where(lane_head == h, qm, jnp.zeros_like(qm)), km) for h in range(H)]
    scores = [jnp.where(causal, s, 0.0).astype(BF16) for s in scores]
    o_intra = [_dot(scores[h], vb[:, h * GLA_DV:(h + 1) * GLA_DV]) for h in range(H)]
    for h in range(H):
        sl = slice(h * GLA_DV, (h + 1) * GLA_DV)
        oh = o_intra[h] + o_inter[:, sl]
        gate = gla_ref[:, 2 * GLA_QK + GLA_V + h * GLA_DV:2 * GLA_QK + GLA_V + (h + 1) * GLA_DV]
        o_ref[:, sl] = (_rms(oh, gn) * _silu(gate)).astype(o_ref.dtype)

    bt = b.T
    b_last = bt[:, tc - 1:tc]
    klt = (k.T * jnp.exp(b_last - bt)).astype(BF16)
    upd = _dot(klt, vb)
    rblk = lax.broadcasted_iota(jnp.int32, upd.shape, 0) // GLA_DK
    cblk = lax.broadcasted_iota(jnp.int32, upd.shape, 1) // GLA_DV
    h_new = hbd * jnp.exp(b_last) + jnp.where(rblk == cblk, upd, 0.0)
    h_scr[...] = h_new

    @pl.when(c == nc - 1)
    def _():
        for h in range(H):
            st_ref[0, h] = h_new[h * GLA_DK:(h + 1) * GLA_DK, h * GLA_DV:(h + 1) * GLA_DV]


def _gla_prompt(gla2d, s0, gn, bsz, t):
    nc = t // CHUNK
    return pl.pallas_call(
        functools.partial(_gla_chunk_kernel, nc=nc),
        grid=(bsz, nc),
        in_specs=[
            pl.BlockSpec((CHUNK, GLA_COLS), lambda b, c: (b * nc + c, 0)),
            pl.BlockSpec((1, H, GLA_DK, GLA_DV), lambda b, c: (b, 0, 0, 0)),
            pl.BlockSpec((1, GLA_DV), lambda b, c: (0, 0)),
        ],
        out_specs=[
            pl.BlockSpec((CHUNK, GLA_V), lambda b, c: (b * nc + c, 0)),
            pl.BlockSpec((1, H, GLA_DK, GLA_DV), lambda b, c: (b, 0, 0, 0)),
        ],
        out_shape=[
            jax.ShapeDtypeStruct((bsz * t, GLA_V), BF16),
            jax.ShapeDtypeStruct((bsz, H, GLA_DK, GLA_DV), F32),
        ],
        scratch_shapes=[pltpu.VMEM((GLA_QK, GLA_V), F32)],
        compiler_params=pltpu.CompilerParams(
            dimension_semantics=("parallel", "arbitrary"), vmem_limit_bytes=VMEM_LIMIT),
        name="gla_prompt",
    )(gla2d, s0, gn)


def _tri_inverse(l_strict, eye):
    n = eye.shape[0]
    m = [-l for l in l_strict]
    p = [eye + x for x in m]
    m = _dot3_many(m, m)
    power = 2
    while 2 * power < n:
        pm = _dot3_many([jnp.concatenate([a, b], axis=0) for a, b in zip(p, m)], m)
        p = [a + x[:n] for a, x in zip(p, pm)]
        m = [x[n:] for x in pm]
        power *= 2
    return [a + x for a, x in zip(p, _dot3_many(p, m))]


def _gdn_chunk_kernel(gdn_ref, s0_ref, cw_ref, gn_ref, o_ref, st_ref, h_scr, xprev_scr, *, nc):
    c = pl.program_id(1)
    tc = CHUNK

    @pl.when(c == 0)
    def _():
        xprev_scr[...] = jnp.zeros_like(xprev_scr)
        for h in range(H):
            h_scr[h * GDN_D:(h + 1) * GDN_D, :] = s0_ref[0, h]

    x = gdn_ref[:, 0:GDN_CONV_C]
    prev8 = xprev_scr[...]
    y = x * cw_ref[3:4, :]
    for kk in range(1, GDN_CONV_W):
        y = y + _shift_rows(x, kk, prev8) * cw_ref[3 - kk:4 - kk, :]
    xprev_scr[...] = x[tc - SUBLANE:tc, :]
    qkv = _silu(y)

    bg = gdn_ref[:, W_GDN:GDN_COLS]
    row = lax.broadcasted_iota(jnp.int32, (tc, tc), 0)
    col = lax.broadcasted_iota(jnp.int32, (tc, tc), 1)
    causal = row >= col
    strict = row > col
    eye = (row == col).astype(F32)
    gc = _cumsum_rows(causal.astype(BF16), bg)
    gt = gc.T
    gn = gn_ref[...]

    heads = range(H)
    sls = [slice(h * GDN_D, (h + 1) * GDN_D) for h in heads]
    kf, kb, qb, vf, beta, gcol, decay = [], [], [], [], [], [], []
    for h in heads:
        qh = qkv[:, sls[h]]
        kh = qkv[:, GDN_QK + h * GDN_D:GDN_QK + (h + 1) * GDN_D]
        vf.append(qkv[:, 2 * GDN_QK + h * GDN_D:2 * GDN_QK + (h + 1) * GDN_D])
        qh = qh * lax.rsqrt(jnp.sum(qh * qh, axis=-1, keepdims=True) + EPS) * (GDN_D ** -0.5)
        kh = kh * lax.rsqrt(jnp.sum(kh * kh, axis=-1, keepdims=True) + EPS)
        kf.append(kh)
        kb.append(kh.astype(BF16))
        qb.append(qh.astype(BF16))
        beta.append(bg[:, h:h + 1])
        gcol.append(gc[:, H + h:H + h + 1])
        grow = gt[H + h:H + h + 1, :]
        decay.append(jnp.where(causal, jnp.exp(jnp.where(causal, gcol[h] - grow, 0.0)), 0.0))

    qk_kk = [_dot_nt(jnp.concatenate([qb[h], kb[h]], axis=0), kb[h]) for h in heads]
    hh = [h_scr[sls[h], :] for h in heads]
    kq_h = [_dot(jnp.concatenate([kb[h], qb[h]], axis=0), hh[h].astype(BF16)) for h in heads]
    tinv = _tri_inverse([jnp.where(strict, beta[h] * qk_kk[h][tc:] * decay[h], 0.0) for h in heads], eye)
    eg = [jnp.exp(gcol[h]) for h in heads]
    rhs = [beta[h] * (vf[h] - eg[h] * kq_h[h][:tc]) for h in heads]
    ub = [u.astype(BF16) for u in _dot3_many(tinv, rhs)]
    ou = []
    for h in heads:
        qk = (qk_kk[h][:tc] * decay[h]).astype(BF16)
        g_last = gcol[h][tc - 1:tc, :]
        kd = (kf[h] * jnp.exp(g_last - gcol[h])).T.astype(BF16)
        ou.append(_dot(jnp.concatenate([qk, kd], axis=0), ub[h]))
    for h in heads:
        o = eg[h] * kq_h[h][tc:] + ou[h][:tc]
        h_new = jnp.exp(gcol[h][tc - 1:tc, :]) * hh[h] + ou[h][tc:]
        h_scr[sls[h], :] = h_new
        z = gdn_ref[:, GDN_CONV_C + h * GDN_D:GDN_CONV_C + (h + 1) * GDN_D]
        o_ref[:, sls[h]] = (_rms(o, gn) * _silu(z)).astype(o_ref.dtype)
        st_ref[0, h] = h_new


def _gdn_prompt(gdn2d, s0, cw, gn, bsz, t):
    nc = t // CHUNK
    return pl.pallas_call(
        functools.partial(_gdn_chunk_kernel, nc=nc),
        grid=(bsz, nc),
        in_specs=[
            pl.BlockSpec((CHUNK, GDN_COLS), lambda b, c: (b * nc + c, 0)),
            pl.BlockSpec((1, H, GDN_D, GDN_D), lambda b, c: (b, 0, 0, 0)),
            pl.BlockSpec((GDN_CONV_W, GDN_CONV_C), lambda b, c: (0, 0)),
            pl.BlockSpec((1, GDN_D), lambda b, c: (0, 0)),
        ],
        out_specs=[
            pl.BlockSpec((CHUNK, GDN_V), lambda b, c: (b * nc + c, 0)),
            pl.BlockSpec((1, H, GDN_D, GDN_D), lambda b, c: (b, 0, 0, 0)),
        ],
        out_shape=[
            jax.ShapeDtypeStruct((bsz * t, GDN_V), BF16),
            jax.ShapeDtypeStruct((bsz, H, GDN_D, GDN_D), F32),
        ],
        scratch_shapes=[pltpu.VMEM((H * GDN_D, GDN_D), F32),
                        pltpu.VMEM((SUBLANE, GDN_CONV_C), F32)],
        compiler_params=pltpu.CompilerParams(
            dimension_semantics=("parallel", "arbitrary"), vmem_limit_bytes=VMEM_LIMIT),
        name="gdn_prompt",
    )(gdn2d, s0, cw, gn)


def _mix_sample_kernel(gla_ref, gdn_ref, xc_ref, sa_ref, sb_ref, cw_ref, gna_ref, gnb_ref,
                       oa_ref, ob_ref, sta_ref, stb_ref, pad_scr, *, t, group):
    @pl.when(pl.program_id(0) == 0)
    def _():
        pad_scr[...] = jnp.zeros_like(pad_scr)

    gna = gna_ref[...]
    gnb = gnb_ref[...]
    for s in range(group):
        pad_scr[0:t, 0:2 * GLA_QK] = gla_ref[s, :, 0:2 * GLA_QK]
        pad_scr[0:t, 2 * GLA_QK:3 * GLA_QK] = gla_ref[s, :, W_GLA:GLA_COLS]
        cols = pad_scr[:, 0:3 * GLA_QK].T
        q_t = cols[0:GLA_QK] * (GLA_DK ** -0.5)
        k_t = cols[GLA_QK:2 * GLA_QK]
        a_t = jnp.exp(cols[2 * GLA_QK:3 * GLA_QK])
        hst = jnp.concatenate([sa_ref[s, h] for h in range(H)], axis=0)
        for tt in range(t):
            vrep = jnp.concatenate(
                [jnp.broadcast_to(gla_ref[s, tt:tt + 1, 2 * GLA_QK + h * GLA_DV:2 * GLA_QK + (h + 1) * GLA_DV],
                                  (GLA_DK, GLA_DV)) for h in range(H)], axis=0)
            hst = a_t[:, tt:tt + 1] * hst + k_t[:, tt:tt + 1] * vrep
            qh = q_t[:, tt:tt + 1] * hst
            for h in range(H):
                o = jnp.sum(qh[h * GLA_DK:(h + 1) * GLA_DK], axis=0, keepdims=True)
                gate = gla_ref[s, tt:tt + 1, 2 * GLA_QK + GLA_V + h * GLA_DV:2 * GLA_QK + GLA_V + (h + 1) * GLA_DV]
                oa_ref[s, tt:tt + 1, h * GLA_DV:(h + 1) * GLA_DV] = (
                    _rms(o, gna) * _silu(gate)).astype(oa_ref.dtype)
        for h in range(H):
            sta_ref[s, h] = hst[h * GLA_DK:(h + 1) * GLA_DK]

        y = xc_ref[s, 0:t, :] * cw_ref[0:1, :]
        for w in range(1, GDN_CONV_W):
            y = y + xc_ref[s, w:w + t, :] * cw_ref[w:w + 1, :]
        qkv = _silu(y)
        for h in range(H):
            sl = slice(h * GDN_D, (h + 1) * GDN_D)
            qh = qkv[:, sl]
            kh = qkv[:, GDN_QK + h * GDN_D:GDN_QK + (h + 1) * GDN_D]
            qh = qh * lax.rsqrt(jnp.sum(qh * qh, axis=-1, keepdims=True) + EPS) * (GDN_D ** -0.5)
            kh = kh * lax.rsqrt(jnp.sum(kh * kh, axis=-1, keepdims=True) + EPS)
            pad_scr[0:t, sl] = qh
            pad_scr[0:t, GDN_QK + h * GDN_D:GDN_QK + (h + 1) * GDN_D] = kh
        cols = pad_scr[:, 0:2 * GDN_QK].T
        bg = gdn_ref[s, :, W_GDN:GDN_COLS]
        ebg = jnp.exp(bg)
        for h in range(H):
            sl = slice(h * GDN_D, (h + 1) * GDN_D)
            hh = sb_ref[s, h]
            for tt in range(t):
                kcol = cols[GDN_QK + h * GDN_D:GDN_QK + (h + 1) * GDN_D, tt:tt + 1]
                qcol = cols[h * GDN_D:(h + 1) * GDN_D, tt:tt + 1]
                hh = ebg[tt:tt + 1, H + h:H + h + 1] * hh
                hk = jnp.sum(kcol * hh, axis=0, keepdims=True)
                vrow = qkv[tt:tt + 1, 2 * GDN_QK + h * GDN_D:2 * GDN_QK + (h + 1) * GDN_D]
                u = bg[tt:tt + 1, h:h + 1] * (vrow - hk)
                hh = hh + kcol * u
                o = jnp.sum(qcol * hh, axis=0, keepdims=True)
                z = gdn_ref[s, tt:tt + 1, GDN_CONV_C + h * GDN_D:GDN_CONV_C + (h + 1) * GDN_D]
                ob_ref[s, tt:tt + 1, sl] = (_rms(o, gnb) * _silu(z)).astype(ob_ref.dtype)
            stb_ref[s, h] = hh


def _mix_sample(gla3d, gdn3d, xc, sa, sb, cw, gna, gnb, group):
    bsz, t = gla3d.shape[0], gla3d.shape[1]
    i3 = lambda i: (i, 0, 0)
    i4 = lambda i: (i, 0, 0, 0)
    c2 = lambda i: (0, 0)
    return pl.pallas_call(
        functools.partial(_mix_sample_kernel, t=t, group=group),
        grid=(bsz // group,),
        in_specs=[
            pl.BlockSpec((group, t, GLA_COLS), i3),
            pl.BlockSpec((group, t, GDN_COLS), i3),
            pl.BlockSpec((group, SUBLANE, GDN_CONV_C), i3),
            pl.BlockSpec((group, H, GLA_DK, GLA_DV), i4),
            pl.BlockSpec((group, H, GDN_D, GDN_D), i4),
            pl.BlockSpec((GDN_CONV_W, GDN_CONV_C), c2),
            pl.BlockSpec((1, GLA_DV), c2),
            pl.BlockSpec((1, GDN_D), c2),
        ],
        out_specs=[
            pl.BlockSpec((group, t, GLA_V), i3),
            pl.BlockSpec((group, t, GDN_V), i3),
            pl.BlockSpec((group, H, GLA_DK, GLA_DV), i4),
            pl.BlockSpec((group, H, GDN_D, GDN_D), i4),
        ],
        out_shape=[
            jax.ShapeDtypeStruct((bsz, t, GLA_V), F32),
            jax.ShapeDtypeStruct((bsz, t, GDN_V), F32),
            jax.ShapeDtypeStruct((bsz, H, GLA_DK, GLA_DV), F32),
            jax.ShapeDtypeStruct((bsz, H, GDN_D, GDN_D), F32),
        ],
        scratch_shapes=[pltpu.VMEM((LANE, GLA_COLS), F32)],
        compiler_params=pltpu.CompilerParams(
            dimension_semantics=("arbitrary",), vmem_limit_bytes=VMEM_LIMIT),
        name="mix_sample",
    )(gla3d, gdn3d, xc, sa, sb, cw, gna, gnb)


def _ffn_kernel(*refs, seq_tiles, has_hist, seq_len):
    if has_hist:
        (x_ref, oa_ref, ob_ref, wo_ref, g2_ref, wu_ref, cw_ref, wd_ref, gf_ref, h1_ref, h2_ref,
         y_ref, u_ref) = refs
    else:
        (x_ref, oa_ref, ob_ref, wo_ref, g2_ref, wu_ref, cw_ref, wd_ref, gf_ref,
         y_ref, tail_ref, u_scr) = refs
    tm = x_ref.shape[0]

    if not has_hist:
        @pl.when(pl.program_id(0) % seq_tiles == 0)
        def _():
            u_scr[0:SUBLANE, :] = jnp.zeros((SUBLANE, 2 * D_FF), F32)

    x1 = (x_ref[...] + _dot(oa_ref[...].astype(BF16), wo_ref[0:GLA_V, :])
          + _dot(ob_ref[...].astype(BF16), wo_ref[GLA_V:GLA_V + GDN_V, :]))
    hn = _rms(x1, g2_ref[...]).astype(BF16)

    if has_hist:
        pos = lax.broadcasted_iota(jnp.int32, (tm, FF_CK), 0) % seq_len

    def conv(u, off):
        cols = slice(off, off + FF_CK)
        if has_hist:
            m1 = jnp.where(pos >= 1, pltpu.roll(u, 1, 0), h1_ref[:, cols])
            m2 = jnp.where(pos >= 2, pltpu.roll(u, 2, 0), h2_ref[:, cols])
            u_ref[:, cols] = u
        else:
            u = u_scr[SUBLANE:SUBLANE + tm, cols]
            m1 = u_scr[SUBLANE - 1:SUBLANE - 1 + tm, cols]
            m2 = u_scr[SUBLANE - 2:SUBLANE - 2 + tm, cols]
            tail = u_scr[tm:tm + SUBLANE, cols]
            tail_ref[0, :, cols] = tail
            u_scr[0:SUBLANE, cols] = tail
        return (m2 * cw_ref[0:1, cols] + m1 * cw_ref[1:2, cols] + u * cw_ref[2:3, cols])

    def up(c):
        ua = _dot(hn, wu_ref[:, c:c + FF_CK])
        ub = _dot(hn, wu_ref[:, D_FF + c:D_FF + c + FF_CK])
        if has_hist:
            return ua, ub
        u_scr[SUBLANE:SUBLANE + tm, c:c + FF_CK] = ua
        u_scr[SUBLANE:SUBLANE + tm, D_FF + c:D_FF + c + FF_CK] = ub
        return None, None

    acc = jnp.zeros((tm, D_MODEL), F32)
    ahead = 1 if has_hist else FF_LOOKAHEAD
    steps = list(range(0, D_FF, FF_CK))
    pending = [up(c) for c in steps[:ahead]]
    for i, c in enumerate(steps):
        if i + ahead < len(steps):
            pending.append(up(steps[i + ahead]))
        ua, ub = pending.pop(0)
        a = conv(ua, c)
        b = conv(ub, D_FF + c)
        act = (_silu(a) * b).astype(BF16)
        acc = acc + _dot(act, wd_ref[c:c + FF_CK, :])
    y_ref[...] = _rms(x1 + acc, gf_ref[...])


def _ffn(x2d, oa, ob, wo, g2, wu, cw, wd, gf, tm, seq_len, hist=None):
    n = x2d.shape[0]
    has_hist = hist is not None
    row = lambda i: (i, 0)
    const = lambda i: (0, 0)
    in_specs = [
        pl.BlockSpec((tm, D_MODEL), row),
        pl.BlockSpec((tm, GLA_V), row),
        pl.BlockSpec((tm, GDN_V), row),
        pl.BlockSpec((GLA_V + GDN_V, D_MODEL), const),
        pl.BlockSpec((1, D_MODEL), const),
        pl.BlockSpec((D_MODEL, 2 * D_FF), const),
        pl.BlockSpec((FFN_CONV_W, 2 * D_FF), const),
        pl.BlockSpec((D_FF, D_MODEL), const),
        pl.BlockSpec((1, D_MODEL), const),
    ]
    args = [x2d, oa, ob, wo, g2, wu, cw, wd, gf]
    out_specs = [pl.BlockSpec((tm, D_MODEL), row)]
    out_shape = [jax.ShapeDtypeStruct((n, D_MODEL), F32)]
    scratch = []
    if has_hist:
        assert tm % seq_len == 0
        seq_tiles = 1
        in_specs += [pl.BlockSpec((tm, 2 * D_FF), row)] * 2
        args += list(hist)
        out_specs.append(pl.BlockSpec((tm, 2 * D_FF), row))
        out_shape.append(jax.ShapeDtypeStruct((n, 2 * D_FF), F32))
    else:
        assert seq_len % tm == 0
        seq_tiles = seq_len // tm
        out_specs.append(pl.BlockSpec((1, SUBLANE, 2 * D_FF), lambda i: (i, 0, 0)))
        out_shape.append(jax.ShapeDtypeStruct((n // tm, SUBLANE, 2 * D_FF), F32))
        scratch.append(pltpu.VMEM((SUBLANE + tm, 2 * D_FF), F32))
    return pl.pallas_call(
        functools.partial(_ffn_kernel, seq_tiles=seq_tiles, has_hist=has_hist, seq_len=seq_len),
        grid=(n // tm,),
        in_specs=in_specs,
        out_specs=out_specs,
        out_shape=out_shape,
        scratch_shapes=scratch,
        compiler_params=pltpu.CompilerParams(
            dimension_semantics=("arbitrary",), vmem_limit_bytes=VMEM_LIMIT),
        name="ffn_sample" if has_hist else "ffn_prompt",
    )(*args)


def _pad_cols(a, width):
    return jnp.pad(a, ((0, 0), (0, width - a.shape[1])))


def kernel(x_prompt, x_sample, state_gla, state_gdn, state_gdn_conv, state_ffn_conv, norm1_g, w_in, gla_w_a2, gla_b_a, gla_norm_g, gdn_conv_w, gdn_a_log, gdn_dt_bias, gdn_norm_g, w_out, norm2_g, w_up, ffn_conv_w, w_down, norm_f_g):
    assert w_in.shape[0] == 1, "single layer"
    bp, tp, _ = x_prompt.shape
    bs, ts, _ = x_sample.shape

    offs = [0]
    for s in IN_SIZES:
        offs.append(offs[-1] + s)
    seg = [w_in[0][:, offs[i]:offs[i + 1]] for i in range(len(IN_SIZES))]
    w1 = jnp.concatenate(
        seg[0:4] + [seg[5], seg[6], _pad_cols(seg[4], LANE),
                    _pad_cols(jnp.concatenate([seg[7], seg[8]], axis=1), LANE)], axis=1).astype(BF16)
    wa2 = jnp.pad(gla_w_a2[0], ((0, LANE - GLA_RANK), (0, 0))).astype(BF16)
    ba = gla_b_a[0][None, :]
    alog = jnp.pad(gdn_a_log[0], (H, LANE - 2 * H))[None, :]
    dtb = jnp.pad(gdn_dt_bias[0], (H, LANE - 2 * H))[None, :]
    g1 = norm1_g[0][None, :]
    g2 = norm2_g[0][None, :]
    gf = norm_f_g[None, :]
    gna = gla_norm_g[0][None, :]
    gnb = gdn_norm_g[0][None, :]
    cwb = gdn_conv_w[0]
    cwf = ffn_conv_w[0]
    wo = w_out[0].astype(BF16)
    wu = w_up[0].astype(BF16)
    wd = w_down[0].astype(BF16)

    xp = x_prompt.reshape(bp * tp, D_MODEL)
    gla_p, gdn_p = _inproj(xp, g1, w1, wa2, ba, alog, dtb, tm=256)
    oa_p, p_gla = _gla_prompt(gla_p, jnp.zeros((bp,) + state_gla.shape[2:], F32), gna, bp, tp)
    ob_p, p_gdn = _gdn_prompt(gdn_p, jnp.zeros((bp,) + state_gdn.shape[2:], F32), cwb, gnb, bp, tp)
    tm_p = 512
    y_p, tail_p = _ffn(xp, oa_p, ob_p, wo, g2, wu, cwf, wd, gf, tm=tm_p, seq_len=tp)
    y_prompt = y_p.reshape(bp, tp, D_MODEL)
    p_conv = gdn_p.reshape(bp, tp, GDN_COLS)[:, tp - (GDN_CONV_W - 1):, :GDN_CONV_C]
    p_ffn = tail_p.reshape(bp, tp // tm_p, SUBLANE, 2 * D_FF)[:, -1, SUBLANE - (FFN_CONV_W - 1):, :]

    xs = x_sample.reshape(bs * ts, D_MODEL)
    gla_s, gdn_s = _inproj(xs, g1, w1, wa2, ba, alog, dtb, tm=256)
    gdn_s3 = gdn_s.reshape(bs, ts, GDN_COLS)
    xc = jnp.concatenate(
        [state_gdn_conv[0], gdn_s3[:, :, :GDN_CONV_C],
         jnp.zeros((bs, SUBLANE - ts - (GDN_CONV_W - 1), GDN_CONV_C), F32)], axis=1)
    oa_s, ob_s, s_gla, s_gdn = _mix_sample(
        gla_s.reshape(bs, ts, GLA_COLS), gdn_s3, xc, state_gla[0], state_gdn[0], cwb, gna, gnb, group=4)
    hist = state_ffn_conv[0]
    zrow = jnp.zeros((bs, 1, 2 * D_FF), F32)
    hist_m1 = jnp.concatenate([hist[:, 1:2]] + [zrow] * (ts - 1), axis=1).reshape(bs * ts, 2 * D_FF)
    hist_m2 = jnp.concatenate([hist] + [zrow] * (ts - 2), axis=1).reshape(bs * ts, 2 * D_FF)
    y_s, u_s = _ffn(xs, oa_s.reshape(bs * ts, GLA_V), ob_s.reshape(bs * ts, GDN_V), wo, g2, wu, cwf, wd,
                    gf, tm=128, seq_len=ts, hist=(hist_m1, hist_m2))
    y_sample = y_s.reshape(bs, ts, D_MODEL)
    s_conv = xc[:, ts:ts + GDN_CONV_W - 1]
    s_ffn = u_s.reshape(bs, ts, 2 * D_FF)[:, ts - (FFN_CONV_W - 1):]

    return (y_prompt, y_sample, p_gla[None], p_gdn[None], p_conv[None], p_ffn[None],
            s_gla[None], s_gdn[None], s_conv[None], s_ffn[None])
```

```python
import functools

import jax
import jax.numpy as jnp
from jax import lax
from jax.experimental import pallas as pl
from jax.experimental.pallas import tpu as pltpu

F32 = jnp.float32
BF16 = jnp.bfloat16

D_MODEL = 1024
H = 4
GLA_DK = 64
GLA_DV = 128
GLA_RANK = 16
GLA_TAU = 16.0
GDN_D = 128
GDN_CONV_W = 4
D_FF = 2816
FFN_CONV_W = 3
EPS = 1e-6

GLA_QK = H * GLA_DK
GLA_V = H * GLA_DV
GDN_QK = H * GDN_D
GDN_V = H * GDN_D
GDN_CONV_C = 2 * GDN_QK + GDN_V
IN_SIZES = (GLA_QK, GLA_QK, GLA_V, GLA_V, GLA_RANK, GDN_CONV_C, GDN_V, H, H)

LANE = 128
SUBLANE = 8
VMEM_LIMIT = 56 * 1024 * 1024

W_GLA = 2 * GLA_QK + 2 * GLA_V
W_GDN = GDN_CONV_C + GDN_V
W_LOW_OFF = W_GLA + W_GDN
W_BD_OFF = W_LOW_OFF + LANE
W_IN_COLS = W_BD_OFF + LANE
GLA_COLS = W_GLA + GLA_QK
GDN_COLS = W_GDN + LANE

CHUNK = 128
FF_CK = 256
FF_LOOKAHEAD = 2


def _dot(a, b):
    return jnp.dot(a, b, preferred_element_type=F32)


def _dot_nt(a, b):
    return lax.dot_general(a, b, (((1,), (1,)), ((), ())), preferred_element_type=F32)


def _split2(x):
    hi = x.astype(BF16)
    lo = (x - hi.astype(F32)).astype(BF16)
    return hi, lo


def _dot3(a, b):
    ah, al = _split2(a)
    bh, bl = _split2(b)
    return _dot(jnp.concatenate([ah, ah, al], axis=1), jnp.concatenate([bh, bl, bh], axis=0))


def _dot3_many(a_list, b_list):
    lhs = [jnp.concatenate([ah, ah, al], axis=1) for ah, al in map(_split2, a_list)]
    rhs = [jnp.concatenate([bh, bl, bh], axis=0) for bh, bl in map(_split2, b_list)]
    return [_dot(x, y) for x, y in zip(lhs, rhs)]


def _exact_rows(sel_bf16, x):
    h1 = x.astype(BF16)
    r1 = x - h1.astype(F32)
    h2 = r1.astype(BF16)
    h3 = (r1 - h2.astype(F32)).astype(BF16)
    return _dot(jnp.concatenate([sel_bf16] * 3, axis=1), jnp.concatenate([h1, h2, h3], axis=0))


def _sigmoid(x):
    return 1.0 / (1.0 + jnp.exp(-x))


def _silu(x):
    return x * _sigmoid(x)


def _softplus(x):
    return jnp.maximum(x, 0.0) + jnp.log(1.0 + jnp.exp(-jnp.abs(x)))


def _rms(x, g):
    ms = jnp.mean(x * x, axis=-1, keepdims=True)
    return x * lax.rsqrt(ms + EPS) * g


def _shift_rows(x, k, prev8):
    rolled = pltpu.roll(x, k, 0)
    prev_rolled = pltpu.roll(prev8, k, 0)
    row = lax.broadcasted_iota(jnp.int32, (SUBLANE, x.shape[1]), 0)
    first = jnp.where(row < k, prev_rolled, rolled[0:SUBLANE])
    return jnp.concatenate([first, rolled[SUBLANE:]], axis=0)


def _inproj_kernel(x_ref, g1_ref, w_ref, wa2_ref, ba_ref, alog_ref, dtb_ref, gla_ref, gdn_ref):
    hn = _rms(x_ref[...], g1_ref[...]).astype(BF16)
    for c in range(0, W_GLA, 512):
        gla_ref[:, c:c + 512] = _dot(hn, w_ref[:, c:c + 512])
    for c in range(0, W_GDN, 512):
        gdn_ref[:, c:c + 512] = _dot(hn, w_ref[:, W_GLA + c:W_GLA + c + 512])
    low = _dot(hn, w_ref[:, W_LOW_OFF:W_LOW_OFF + LANE])
    xa = _dot(low.astype(BF16), wa2_ref[...]) + ba_ref[...]
    gla_ref[:, W_GLA:GLA_COLS] = -_softplus(-xa) * (1.0 / GLA_TAU)
    bd = _dot(hn, w_ref[:, W_BD_OFF:W_BD_OFF + LANE])
    lane = lax.broadcasted_iota(jnp.int32, bd.shape, 1)
    beta = _sigmoid(bd)
    g = -jnp.exp(alog_ref[...]) * _softplus(bd + dtb_ref[...])
    gdn_ref[:, W_GDN:GDN_COLS] = jnp.where(lane < H, beta, jnp.where(lane < 2 * H, g, 0.0))


def _inproj(x2d, g1, w1, wa2, ba, alog, dtb, tm):
    n = x2d.shape[0]
    const = lambda i: (0, 0)
    return pl.pallas_call(
        _inproj_kernel,
        grid=(n // tm,),
        in_specs=[
            pl.BlockSpec((tm, D_MODEL), lambda i: (i, 0)),
            pl.BlockSpec((1, D_MODEL), const),
            pl.BlockSpec((D_MODEL, W_IN_COLS), const),
            pl.BlockSpec((LANE, GLA_QK), const),
            pl.BlockSpec((1, GLA_QK), const),
            pl.BlockSpec((1, LANE), const),
            pl.BlockSpec((1, LANE), const),
        ],
        out_specs=[
            pl.BlockSpec((tm, GLA_COLS), lambda i: (i, 0)),
            pl.BlockSpec((tm, GDN_COLS), lambda i: (i, 0)),
        ],
        out_shape=[
            jax.ShapeDtypeStruct((n, GLA_COLS), F32),
            jax.ShapeDtypeStruct((n, GDN_COLS), F32),
        ],
        compiler_params=pltpu.CompilerParams(
            dimension_semantics=("parallel",), vmem_limit_bytes=VMEM_LIMIT),
        name="inproj",
    )(x2d, g1, w1, wa2, ba, alog, dtb)


def _gla_chunk_kernel(gla_ref, s0_ref, gn_ref, o_ref, st_ref, h_scr, *, nc):
    c = pl.program_id(1)
    tc = CHUNK

    @pl.when(c == 0)
    def _():
        h_scr[...] = jnp.zeros_like(h_scr)
        for h in range(H):
            h_scr[h * GLA_DK:(h + 1) * GLA_DK, h * GLA_DV:(h + 1) * GLA_DV] = s0_ref[0, h]

    q = gla_ref[:, 0:GLA_QK] * (GLA_DK ** -0.5)
    k = gla_ref[:, GLA_QK:2 * GLA_QK]
    v = gla_ref[:, 2 * GLA_QK:2 * GLA_QK + GLA_V]
    la = gla_ref[:, W_GLA:GLA_COLS]

    row = lax.broadcasted_iota(jnp.int32, (tc, tc), 0)
    col = lax.broadcasted_iota(jnp.int32, (tc, tc), 1)
    causal = row >= col
    tri = causal.astype(BF16)

    b = _exact_rows(tri, la)
    b_mid = b[tc // 2 - 1:tc // 2, :]
    qm = (q * jnp.exp(b - b_mid)).astype(BF16)
    km = (k * jnp.exp(b_mid - b)).astype(BF16)
    qi = (q * jnp.exp(b)).astype(BF16)

    hbd = h_scr[...]
    o_inter = _dot(qi, hbd.astype(BF16))

    lane_head = lax.broadcasted_iota(jnp.int32, (1, GLA_QK), 1) // GLA_DK
    vb = v.astype(BF16)
    gn = gn_ref[...]
    scores = [_dot_nt(jnp.where(lane_head == h, qm, jnp.zeros_like(qm)), km) for h in range(H)]
    scores = [jnp.where(causal, s, 0.0).astype(BF16) for s in scores]
    o_intra = [_dot(scores[h], vb[:, h * GLA_DV:(h + 1) * GLA_DV]) for h in range(H)]
    for h in range(H):
        sl = slice(h * GLA_DV, (h + 1) * GLA_DV)
        oh = o_intra[h] + o_inter[:, sl]
        gate = gla_ref[:, 2 * GLA_QK + GLA_V + h * GLA_DV:2 * GLA_QK + GLA_V + (h + 1) * GLA_DV]
        o_ref[:, sl] = (_rms(oh, gn) * _silu(gate)).astype(o_ref.dtype)

    bt = b.T
    b_last = bt[:, tc - 1:tc]
    klt = (k.T * jnp.exp(b_last - bt)).astype(BF16)
    upd = _dot(klt, vb)
    rblk = lax.broadcasted_iota(jnp.int32, upd.shape, 0) // GLA_DK
    cblk = lax.broadcasted_iota(jnp.int32, upd.shape, 1) // GLA_DV
    h_new = hbd * jnp.exp(b_last) + jnp.where(rblk == cblk, upd, 0.0)
    h_scr[...] = h_new

    @pl.when(c == nc - 1)
    def _():
        for h in range(H):
            st_ref[0, h] = h_new[h * GLA_DK:(h + 1) * GLA_DK, h * GLA_DV:(h + 1) * GLA_DV]


def _gla_prompt(gla2d, s0, gn, bsz, t):
    nc = t // CHUNK
    return pl.pallas_call(
        functools.partial(_gla_chunk_kernel, nc=nc),
        grid=(bsz, nc),
        in_specs=[
            pl.BlockSpec((CHUNK, GLA_COLS), lambda b, c: (b * nc + c, 0)),
            pl.BlockSpec((1, H, GLA_DK, GLA_DV), lambda b, c: (b, 0, 0, 0)),
            pl.BlockSpec((1, GLA_DV), lambda b, c: (0, 0)),
        ],
        out_specs=[
            pl.BlockSpec((CHUNK, GLA_V), lambda b, c: (b * nc + c, 0)),
            pl.BlockSpec((1, H, GLA_DK, GLA_DV), lambda b, c: (b, 0, 0, 0)),
        ],
        out_shape=[
            jax.ShapeDtypeStruct((bsz * t, GLA_V), BF16),
            jax.ShapeDtypeStruct((bsz, H, GLA_DK, GLA_DV), F32),
        ],
        scratch_shapes=[pltpu.VMEM((GLA_QK, GLA_V), F32)],
        compiler_params=pltpu.CompilerParams(
            dimension_semantics=("parallel", "arbitrary"), vmem_limit_bytes=VMEM_LIMIT),
        name="gla_prompt",
    )(gla2d, s0, gn)


def _tri_inverse(l_strict, eye, order):
    n = eye.shape[0]
    m = [-l for l in l_strict]
    p = [eye + x for x in m]
    m = _dot3_many(m, m)
    power = 2
    while 2 * power < order:
        pm = _dot3_many([jnp.concatenate([a, b], axis=0) for a, b in zip(p, m)], m)
        p = [a + x[:n] for a, x in zip(p, pm)]
        m = [x[n:] for x in pm]
        power *= 2
    return [a + x for a, x in zip(p, _dot3_many(p, m))]


def _gdn_chunk_kernel(gdn_ref, s0_ref, cw_ref, gn_ref, o_ref, st_ref, h_scr, xprev_scr, *, nc):
    c = pl.program_id(1)
    tc = CHUNK

    @pl.when(c == 0)
    def _():
        xprev_scr[...] = jnp.zeros_like(xprev_scr)
        for h in range(H):
            h_scr[h * GDN_D:(h + 1) * GDN_D, :] = s0_ref[0, h]

    x = gdn_ref[:, 0:GDN_CONV_C]
    prev8 = xprev_scr[...]
    y = x * cw_ref[3:4, :]
    for kk in range(1, GDN_CONV_W):
        y = y + _shift_rows(x, kk, prev8) * cw_ref[3 - kk:4 - kk, :]
    xprev_scr[...] = x[tc - SUBLANE:tc, :]
    qkv = _silu(y)

    bg = gdn_ref[:, W_GDN:GDN_COLS]
    row = lax.broadcasted_iota(jnp.int32, (tc, tc), 0)
    col = lax.broadcasted_iota(jnp.int32, (tc, tc), 1)
    causal = row >= col
    strict = row > col
    eye = (row == col).astype(F32)
    gc = _exact_rows(causal.astype(BF16), bg)
    gt = gc.T
    gn = gn_ref[...]

    heads = range(H)
    sls = [slice(h * GDN_D, (h + 1) * GDN_D) for h in heads]
    kf, kb, qb, vf, beta, gcol, decay = [], [], [], [], [], [], []
    for h in heads:
        qh = qkv[:, sls[h]]
        kh = qkv[:, GDN_QK + h * GDN_D:GDN_QK + (h + 1) * GDN_D]
        vf.append(qkv[:, 2 * GDN_QK + h * GDN_D:2 * GDN_QK + (h + 1) * GDN_D])
        qh = qh * lax.rsqrt(jnp.sum(qh * qh, axis=-1, keepdims=True) + EPS) * (GDN_D ** -0.5)
        kh = kh * lax.rsqrt(jnp.sum(kh * kh, axis=-1, keepdims=True) + EPS)
        kf.append(kh)
        kb.append(kh.astype(BF16))
        qb.append(qh.astype(BF16))
        beta.append(bg[:, h:h + 1])
        gcol.append(gc[:, H + h:H + h + 1])
        grow = gt[H + h:H + h + 1, :]
        decay.append(jnp.where(causal, jnp.exp(jnp.where(causal, gcol[h] - grow, 0.0)), 0.0))

    qk_kk = [_dot_nt(jnp.concatenate([qb[h], kb[h]], axis=0), kb[h]) for h in heads]
    hh = [h_scr[sls[h], :] for h in heads]
    kq_h = [_dot(jnp.concatenate([kb[h], qb[h]], axis=0), hh[h].astype(BF16)) for h in heads]
    tinv = _tri_inverse([jnp.where(strict, beta[h] * qk_kk[h][tc:] * decay[h], 0.0) for h in heads], eye, tc)
    eg = [jnp.exp(gcol[h]) for h in heads]
    rhs = [beta[h] * (vf[h] - eg[h] * kq_h[h][:tc]) for h in heads]
    ub = [u.astype(BF16) for u in _dot3_many(tinv, rhs)]
    ou = []
    for h in heads:
        qk = (qk_kk[h][:tc] * decay[h]).astype(BF16)
        g_last = gcol[h][tc - 1:tc, :]
        kd = (kf[h] * jnp.exp(g_last - gcol[h])).T.astype(BF16)
        ou.append(_dot(jnp.concatenate([qk, kd], axis=0), ub[h]))
    for h in heads:
        o = eg[h] * kq_h[h][tc:] + ou[h][:tc]
        h_new = jnp.exp(gcol[h][tc - 1:tc, :]) * hh[h] + ou[h][tc:]
        h_scr[sls[h], :] = h_new
        z = gdn_ref[:, GDN_CONV_C + h * GDN_D:GDN_CONV_C + (h + 1) * GDN_D]
        o_ref[:, sls[h]] = (_rms(o, gn) * _silu(z)).astype(o_ref.dtype)
        st_ref[0, h] = h_new


def _gdn_prompt(gdn2d, s0, cw, gn, bsz, t):
    nc = t // CHUNK
    return pl.pallas_call(
        functools.partial(_gdn_chunk_kernel, nc=nc),
        grid=(bsz, nc),
        in_specs=[
            pl.BlockSpec((CHUNK, GDN_COLS), lambda b, c: (b * nc + c, 0)),
            pl.BlockSpec((1, H, GDN_D, GDN_D), lambda b, c: (b, 0, 0, 0)),
            pl.BlockSpec((GDN_CONV_W, GDN_CONV_C), lambda b, c: (0, 0)),
            pl.BlockSpec((1, GDN_D), lambda b, c: (0, 0)),
        ],
        out_specs=[
            pl.BlockSpec((CHUNK, GDN_V), lambda b, c: (b * nc + c, 0)),
            pl.BlockSpec((1, H, GDN_D, GDN_D), lambda b, c: (b, 0, 0, 0)),
        ],
        out_shape=[
            jax.ShapeDtypeStruct((bsz * t, GDN_V), BF16),
            jax.ShapeDtypeStruct((bsz, H, GDN_D, GDN_D), F32),
        ],
        scratch_shapes=[pltpu.VMEM((H * GDN_D, GDN_D), F32),
                        pltpu.VMEM((SUBLANE, GDN_CONV_C), F32)],
        compiler_params=pltpu.CompilerParams(
            dimension_semantics=("parallel", "arbitrary"), vmem_limit_bytes=VMEM_LIMIT),
        name="gdn_prompt",
    )(gdn2d, s0, cw, gn)


SEQ_PER_STEP = 16


def _mix_sample_kernel(gla_ref, gdn_ref, h1_ref, h2_ref, h3_ref, sa_ref, sb_ref, cw_ref, gna_ref, gnb_ref,
                       oa_ref, ob_ref, sta_ref, stb_ref, a_scr, b_scr, *, t):
    n = SEQ_PER_STEP
    rows = n * t
    tc = LANE
    spg = SUBLANE // t

    @pl.when(pl.program_id(0) == 0)
    def _():
        a_scr[...] = jnp.zeros_like(a_scr)
        b_scr[...] = jnp.zeros_like(b_scr)

    x = gdn_ref[:, 0:GDN_CONV_C]
    pos = lax.broadcasted_iota(jnp.int32, x.shape, 0) % t
    y = x * cw_ref[GDN_CONV_W - 1:GDN_CONV_W, :]
    for kk, h_ref in zip(range(1, GDN_CONV_W), (h1_ref, h2_ref, h3_ref)):
        m = jnp.where(pos >= kk, pltpu.roll(x, kk, 0), h_ref[...])
        y = y + m * cw_ref[GDN_CONV_W - 1 - kk:GDN_CONV_W - kk, :]
    b_scr[0:rows, 0:GDN_CONV_C] = _silu(y)
    b_scr[0:rows, GDN_CONV_C:GDN_CONV_C + LANE] = gdn_ref[:, W_GDN:GDN_COLS]
    a_scr[0:rows, :] = gla_ref[...]

    row = lax.broadcasted_iota(jnp.int32, (tc, tc), 0)
    col = lax.broadcasted_iota(jnp.int32, (tc, tc), 1)
    same = (row // t) == (col // t)
    causal = same & (row >= col)
    strict = same & (row > col)
    eye = (row == col).astype(F32)
    tri = causal.astype(BF16)
    last = (col == (row // t) * t + (t - 1)).astype(BF16)
    lane_seq = lax.broadcasted_iota(jnp.int32, (1, tc), 1) // t
    sub = lax.broadcasted_iota(jnp.int32, (SUBLANE, LANE), 0)

    def pick_rows(res, off):
        groups = []
        for g in range(rows // SUBLANE):
            sl = slice(off + g * SUBLANE, off + (g + 1) * SUBLANE)
            piece = res[g * spg + spg - 1][sl]
            for j in reversed(range(spg - 1)):
                piece = jnp.where(sub < (j + 1) * t, res[g * spg + j][sl], piece)
            groups.append(piece)
        return jnp.concatenate(groups, axis=0)

    q = a_scr[:, 0:GLA_QK] * (GLA_DK ** -0.5)
    k = a_scr[:, GLA_QK:2 * GLA_QK]
    vb = a_scr[:, 2 * GLA_QK:2 * GLA_QK + GLA_V].astype(BF16)
    b = _exact_rows(tri, a_scr[:, W_GLA:GLA_COLS])
    b_end = _exact_rows(last, b)
    qe = (q * jnp.exp(b)).astype(BF16)
    ke = (k * jnp.exp(-b)).astype(BF16)
    klt = (k * jnp.exp(b_end - b)).T.astype(BF16)
    bt = b.T
    lane_head = lax.broadcasted_iota(jnp.int32, (1, GLA_QK), 1) // GLA_DK
    qmask = [jnp.where(lane_head == h, qe, jnp.zeros_like(qe)) for h in range(H)]
    scores = [_dot_nt(qmask[h], ke) for h in range(H)]
    scores = [jnp.where(causal, s, 0.0).astype(BF16) for s in scores]
    o_intra = [_dot(scores[h], vb[:, h * GLA_DV:(h + 1) * GLA_DV]) for h in range(H)]
    lhs_q = jnp.concatenate(qmask, axis=0)
    hs = [jnp.concatenate([sa_ref[s, h] for h in range(H)], axis=0) for s in range(n)]
    res = [_dot(lhs_q, hs[s].astype(BF16)) for s in range(n)]
    upd = [_dot(jnp.where(lane_seq == s, klt, jnp.zeros_like(klt)), vb) for s in range(n)]
    gna = gna_ref[...]
    for h in range(H):
        sl = slice(h * GLA_DV, (h + 1) * GLA_DV)
        oh = o_intra[h][0:rows] + pick_rows(res, h * tc)
        gate = gla_ref[:, 2 * GLA_QK + GLA_V + h * GLA_DV:2 * GLA_QK + GLA_V + (h + 1) * GLA_DV]
        oa_ref[:, sl] = (_rms(oh, gna) * _silu(gate)).astype(oa_ref.dtype)
    for s in range(n):
        e_col = jnp.exp(bt[:, s * t + t - 1:s * t + t])
        diag = jnp.concatenate(
            [upd[s][h * GLA_DK:(h + 1) * GLA_DK, h * GLA_DV:(h + 1) * GLA_DV] for h in range(H)], axis=0)
        new = hs[s] * e_col + diag
        for h in range(H):
            sta_ref[s, h] = new[h * GLA_DK:(h + 1) * GLA_DK]

    bg = b_scr[:, GDN_CONV_C:GDN_CONV_C + LANE]
    gc = _exact_rows(tri, bg)
    gl = _exact_rows(last, gc)
    gt = gc.T
    heads = range(H)
    sls = [slice(h * GDN_D, (h + 1) * GDN_D) for h in heads]
    kf, kb, qb, vf, beta, gcol, decay = [], [], [], [], [], [], []
    for h in heads:
        qh = b_scr[:, sls[h]]
        kh = b_scr[:, GDN_QK + h * GDN_D:GDN_QK + (h + 1) * GDN_D]
        vf.append(b_scr[:, 2 * GDN_QK + h * GDN_D:2 * GDN_QK + (h + 1) * GDN_D])
        qh = qh * lax.rsqrt(jnp.sum(qh * qh, axis=-1, keepdims=True) + EPS) * (GDN_D ** -0.5)
        kh = kh * lax.rsqrt(jnp.sum(kh * kh, axis=-1, keepdims=True) + EPS)
        kf.append(kh)
        kb.append(kh.astype(BF16))
        qb.append(qh.astype(BF16))
        beta.append(bg[:, h:h + 1])
        gcol.append(gc[:, H + h:H + h + 1])
        grow = gt[H + h:H + h + 1, :]
        decay.append(jnp.where(causal, jnp.exp(jnp.where(causal, gcol[h] - grow, 0.0)), 0.0))
    qk_kk = [_dot_nt(jnp.concatenate([qb[h], kb[h]], axis=0), kb[h]) for h in heads]
    hsb = [[sb_ref[s, h] for h in heads] for s in range(n)]
    kq = [[_dot(jnp.concatenate([kb[h], qb[h]], axis=0), hsb[s][h].astype(BF16)) for s in range(n)]
          for h in heads]
    tinv = _tri_inverse([jnp.where(strict, beta[h] * qk_kk[h][tc:] * decay[h], 0.0) for h in heads], eye, t)
    pad_rows = jnp.zeros((tc - rows, GDN_D), F32)
    eg = [jnp.exp(gcol[h]) for h in heads]
    k_h = [jnp.concatenate([pick_rows(kq[h], 0), pad_rows], axis=0) for h in heads]
    rhs = [beta[h] * (vf[h] - eg[h] * k_h[h]) for h in heads]
    ub = [u.astype(BF16) for u in _dot3_many(tinv, rhs)]
    qku = [_dot((qk_kk[h][:tc] * decay[h]).astype(BF16), ub[h]) for h in heads]
    kdt = [(kf[h] * jnp.exp(gl[:, H + h:H + h + 1] - gcol[h])).T.astype(BF16) for h in heads]
    updb = [[_dot(jnp.where(lane_seq == s, kdt[h], jnp.zeros_like(kdt[h])), ub[h]) for s in range(n)]
            for h in heads]
    gnb = gnb_ref[...]
    for h in heads:
        o = eg[h][0:rows] * pick_rows(kq[h], tc) + qku[h][0:rows]
        z = gdn_ref[:, GDN_CONV_C + h * GDN_D:GDN_CONV_C + (h + 1) * GDN_D]
        ob_ref[:, sls[h]] = (_rms(o, gnb) * _silu(z)).astype(ob_ref.dtype)
        for s in range(n):
            e_last = jnp.exp(gl[s * t:s * t + 1, H + h:H + h + 1])
            stb_ref[s, h] = e_last * hsb[s][h] + updb[h][s]


def _mix_sample(gla2d, gdn2d, hists, sa, sb, cw, gna, gnb, t):
    n_rows = gla2d.shape[0]
    rows = SEQ_PER_STEP * t
    assert SUBLANE % t == 0 and rows <= LANE and n_rows % rows == 0
    bsz = n_rows // t
    r2 = lambda i: (i, 0)
    i4 = lambda i: (i, 0, 0, 0)
    c2 = lambda i: (0, 0)
    return pl.pallas_call(
        functools.partial(_mix_sample_kernel, t=t),
        grid=(n_rows // rows,),
        in_specs=[
            pl.BlockSpec((rows, GLA_COLS), r2),
            pl.BlockSpec((rows, GDN_COLS), r2),
            pl.BlockSpec((rows, GDN_CONV_C), r2),
            pl.BlockSpec((rows, GDN_CONV_C), r2),
            pl.BlockSpec((rows, GDN_CONV_C), r2),
            pl.BlockSpec((SEQ_PER_STEP, H, GLA_DK, GLA_DV), i4),
            pl.BlockSpec((SEQ_PER_STEP, H, GDN_D, GDN_D), i4),
            pl.BlockSpec((GDN_CONV_W, GDN_CONV_C), c2),
            pl.BlockSpec((1, GLA_DV), c2),
            pl.BlockSpec((1, GDN_D), c2),
        ],
        out_specs=[
            pl.BlockSpec((rows, GLA_V), r2),
            pl.BlockSpec((rows, GDN_V), r2),
            pl.BlockSpec((SEQ_PER_STEP, H, GLA_DK, GLA_DV), i4),
            pl.BlockSpec((SEQ_PER_STEP, H, GDN_D, GDN_D), i4),
        ],
        out_shape=[
            jax.ShapeDtypeStruct((n_rows, GLA_V), BF16),
            jax.ShapeDtypeStruct((n_rows, GDN_V), BF16),
            jax.ShapeDtypeStruct((bsz, H, GLA_DK, GLA_DV), F32),
            jax.ShapeDtypeStruct((bsz, H, GDN_D, GDN_D), F32),
        ],
        scratch_shapes=[pltpu.VMEM((LANE, GLA_COLS), F32),
                        pltpu.VMEM((LANE, GDN_CONV_C + LANE), F32)],
        compiler_params=pltpu.CompilerParams(
            dimension_semantics=("arbitrary",), vmem_limit_bytes=VMEM_LIMIT),
        name="mix_sample",
    )(gla2d, gdn2d, *hists, sa, sb, cw, gna, gnb)


def _ffn_kernel(*refs, seq_tiles, has_hist, seq_len):
    if has_hist:
        (x_ref, oa_ref, ob_ref, wo_ref, g2_ref, wu_ref, cw_ref, wd_ref, gf_ref, h1_ref, h2_ref,
         y_ref, u_ref) = refs
    else:
        (x_ref, oa_ref, ob_ref, wo_ref, g2_ref, wu_ref, cw_ref, wd_ref, gf_ref,
         y_ref, tail_ref, u_scr) = refs
    tm = x_ref.shape[0]

    if not has_hist:
        @pl.when(pl.program_id(0) % seq_tiles == 0)
        def _():
            u_scr[0:SUBLANE, :] = jnp.zeros((SUBLANE, 2 * D_FF), F32)

    x1 = (x_ref[...] + _dot(oa_ref[...].astype(BF16), wo_ref[0:GLA_V, :])
          + _dot(ob_ref[...].astype(BF16), wo_ref[GLA_V:GLA_V + GDN_V, :]))
    hn = _rms(x1, g2_ref[...]).astype(BF16)

    if has_hist:
        pos = lax.broadcasted_iota(jnp.int32, (tm, FF_CK), 0) % seq_len

    def conv(u, off):
        cols = slice(off, off + FF_CK)
        if has_hist:
            m1 = jnp.where(pos >= 1, pltpu.roll(u, 1, 0), h1_ref[:, cols])
            m2 = jnp.where(pos >= 2, pltpu.roll(u, 2, 0), h2_ref[:, cols])
            u_ref[:, cols] = u
        else:
            u = u_scr[SUBLANE:SUBLANE + tm, cols]
            m1 = u_scr[SUBLANE - 1:SUBLANE - 1 + tm, cols]
            m2 = u_scr[SUBLANE - 2:SUBLANE - 2 + tm, cols]
            tail = u_scr[tm:tm + SUBLANE, cols]
            tail_ref[0, :, cols] = tail
            u_scr[0:SUBLANE, cols] = tail
        return (m2 * cw_ref[0:1, cols] + m1 * cw_ref[1:2, cols] + u * cw_ref[2:3, cols])

    def up(c):
        ua = _dot(hn, wu_ref[:, c:c + FF_CK])
        ub = _dot(hn, wu_ref[:, D_FF + c:D_FF + c + FF_CK])
        if has_hist:
            return ua, ub
        u_scr[SUBLANE:SUBLANE + tm, c:c + FF_CK] = ua
        u_scr[SUBLANE:SUBLANE + tm, D_FF + c:D_FF + c + FF_CK] = ub
        return None, None

    acc = jnp.zeros((tm, D_MODEL), F32)
    ahead = 1 if has_hist else FF_LOOKAHEAD
    steps = list(range(0, D_FF, FF_CK))
    pending = [up(c) for c in steps[:ahead]]
    for i, c in enumerate(steps):
        if i + ahead < len(steps):
            pending.append(up(steps[i + ahead]))
        ua, ub = pending.pop(0)
        a = conv(ua, c)
        b = conv(ub, D_FF + c)
        act = (_silu(a) * b).astype(BF16)
        acc = acc + _dot(act, wd_ref[c:c + FF_CK, :])
    y_ref[...] = _rms(x1 + acc, gf_ref[...])


def _ffn(x2d, oa, ob, wo, g2, wu, cw, wd, gf, tm, seq_len, hist=None):
    n = x2d.shape[0]
    has_hist = hist is not None
    row = lambda i: (i, 0)
    const = lambda i: (0, 0)
    in_specs = [
        pl.BlockSpec((tm, D_MODEL), row),
        pl.BlockSpec((tm, GLA_V), row),
        pl.BlockSpec((tm, GDN_V), row),
        pl.BlockSpec((GLA_V + GDN_V, D_MODEL), const),
        pl.BlockSpec((1, D_MODEL), const),
        pl.BlockSpec((D_MODEL, 2 * D_FF), const),
        pl.BlockSpec((FFN_CONV_W, 2 * D_FF), const),
        pl.BlockSpec((D_FF, D_MODEL), const),
        pl.BlockSpec((1, D_MODEL), const),
    ]
    args = [x2d, oa, ob, wo, g2, wu, cw, wd, gf]
    out_specs = [pl.BlockSpec((tm, D_MODEL), row)]
    out_shape = [jax.ShapeDtypeStruct((n, D_MODEL), F32)]
    scratch = []
    if has_hist:
        assert tm % seq_len == 0
        seq_tiles = 1
        in_specs += [pl.BlockSpec((tm, 2 * D_FF), row)] * 2
        args += list(hist)
        out_specs.append(pl.BlockSpec((tm, 2 * D_FF), row))
        out_shape.append(jax.ShapeDtypeStruct((n, 2 * D_FF), F32))
    else:
        assert seq_len % tm == 0
        seq_tiles = seq_len // tm
        out_specs.append(pl.BlockSpec((1, SUBLANE, 2 * D_FF), lambda i: (i, 0, 0)))
        out_shape.append(jax.ShapeDtypeStruct((n // tm, SUBLANE, 2 * D_FF), F32))
        scratch.append(pltpu.VMEM((SUBLANE + tm, 2 * D_FF), F32))
    return pl.pallas_call(
        functools.partial(_ffn_kernel, seq_tiles=seq_tiles, has_hist=has_hist, seq_len=seq_len),
        grid=(n // tm,),
        in_specs=in_specs,
        out_specs=out_specs,
        out_shape=out_shape,
        scratch_shapes=scratch,
        compiler_params=pltpu.CompilerParams(
            dimension_semantics=("arbitrary",), vmem_limit_bytes=VMEM_LIMIT),
        name="ffn_sample" if has_hist else "ffn_prompt",
    )(*args)


def _pad_cols(a, width):
    return jnp.pad(a, ((0, 0), (0, width - a.shape[1])))


def _delayed_history(state, k, t):
    bsz, w1, c = state.shape
    rows = [state[:, w1 + p - k] if p < k else jnp.zeros((bsz, c), state.dtype) for p in range(t)]
    return jnp.stack(rows, axis=1).reshape(bsz * t, c)


def _last_rows(flat, k, t):
    return jnp.stack([flat[p::t] for p in range(t - k, t)], axis=1)


def kernel(x_prompt, x_sample, state_gla, state_gdn, state_gdn_conv, state_ffn_conv, norm1_g, w_in, gla_w_a2, gla_b_a, gla_norm_g, gdn_conv_w, gdn_a_log, gdn_dt_bias, gdn_norm_g, w_out, norm2_g, w_up, ffn_conv_w, w_down, norm_f_g):
    assert w_in.shape[0] == 1, "single layer"
    bp, tp, _ = x_prompt.shape
    bs, ts, _ = x_sample.shape

    offs = [0]
    for s in IN_SIZES:
        offs.append(offs[-1] + s)
    seg = [w_in[0][:, offs[i]:offs[i + 1]] for i in range(len(IN_SIZES))]
    w1 = jnp.concatenate(
        seg[0:4] + [seg[5], seg[6], _pad_cols(seg[4], LANE),
                    _pad_cols(jnp.concatenate([seg[7], seg[8]], axis=1), LANE)], axis=1).astype(BF16)
    wa2 = jnp.pad(gla_w_a2[0], ((0, LANE - GLA_RANK), (0, 0))).astype(BF16)
    ba = gla_b_a[0][None, :]
    alog = jnp.pad(gdn_a_log[0], (H, LANE - 2 * H))[None, :]
    dtb = jnp.pad(gdn_dt_bias[0], (H, LANE - 2 * H))[None, :]
    g1 = norm1_g[0][None, :]
    g2 = norm2_g[0][None, :]
    gf = norm_f_g[None, :]
    gna = gla_norm_g[0][None, :]
    gnb = gdn_norm_g[0][None, :]
    cwb = gdn_conv_w[0]
    cwf = ffn_conv_w[0]
    wo = w_out[0].astype(BF16)
    wu = w_up[0].astype(BF16)
    wd = w_down[0].astype(BF16)

    xp = x_prompt.reshape(bp * tp, D_MODEL)
    gla_p, gdn_p = _inproj(xp, g1, w1, wa2, ba, alog, dtb, tm=256)
    oa_p, p_gla = _gla_prompt(gla_p, jnp.zeros((bp,) + state_gla.shape[2:], F32), gna, bp, tp)
    ob_p, p_gdn = _gdn_prompt(gdn_p, jnp.zeros((bp,) + state_gdn.shape[2:], F32), cwb, gnb, bp, tp)
    tm_p = 512
    y_p, tail_p = _ffn(xp, oa_p, ob_p, wo, g2, wu, cwf, wd, gf, tm=tm_p, seq_len=tp)
    y_prompt = y_p.reshape(bp, tp, D_MODEL)
    p_conv = gdn_p.reshape(bp, tp, GDN_COLS)[:, tp - (GDN_CONV_W - 1):, :GDN_CONV_C]
    p_ffn = tail_p.reshape(bp, tp // tm_p, SUBLANE, 2 * D_FF)[:, -1, SUBLANE - (FFN_CONV_W - 1):, :]

    xs = x_sample.reshape(bs * ts, D_MODEL)
    gla_s, gdn_s = _inproj(xs, g1, w1, wa2, ba, alog, dtb, tm=256)
    oa_s, ob_s, s_gla, s_gdn = _mix_sample(
        gla_s, gdn_s, [_delayed_history(state_gdn_conv[0], kk, ts) for kk in range(1, GDN_CONV_W)],
        state_gla[0], state_gdn[0], cwb, gna, gnb, ts)
    hist_ffn = [_delayed_history(state_ffn_conv[0], kk, ts) for kk in range(1, FFN_CONV_W)]
    y_s, u_s = _ffn(xs, oa_s, ob_s, wo, g2, wu, cwf, wd, gf, tm=128, seq_len=ts, hist=hist_ffn)
    y_sample = y_s.reshape(bs, ts, D_MODEL)
    s_conv = _last_rows(gdn_s[:, :GDN_CONV_C], GDN_CONV_W - 1, ts)
    s_ffn = _last_rows(u_s, FFN_CONV_W - 1, ts)

    return (y_prompt, y_sample, p_gla[None], p_gdn[None], p_conv[None], p_ffn[None],
            s_gla[None], s_gdn[None], s_conv[None], s_ffn[None])
```

```python
import functools

import jax
import jax.numpy as jnp
from jax import lax
from jax.experimental import pallas as pl
from jax.experimental.pallas import tpu as pltpu

F32 = jnp.float32
BF16 = jnp.bfloat16

D_MODEL = 1024
H = 4
GLA_DK = 64
GLA_DV = 128
GLA_RANK = 16
GLA_TAU = 16.0
GDN_D = 128
GDN_CONV_W = 4
D_FF = 2816
FFN_CONV_W = 3
EPS = 1e-6

GLA_QK = H * GLA_DK
GLA_V = H * GLA_DV
GDN_QK = H * GDN_D
GDN_V = H * GDN_D
GDN_CONV_C = 2 * GDN_QK + GDN_V
IN_SIZES = (GLA_QK, GLA_QK, GLA_V, GLA_V, GLA_RANK, GDN_CONV_C, GDN_V, H, H)

LANE = 128
SUBLANE = 8
VMEM_LIMIT = 56 * 1024 * 1024

W_GLA = 2 * GLA_QK + 2 * GLA_V
W_GDN = GDN_CONV_C + GDN_V
GLA_COLS = W_GLA + GLA_QK
GDN_COLS = W_GDN + LANE

CHUNK = 128
FF_CK = 256
FF_LOOKAHEAD = 11


def _dot(a, b):
    return jnp.dot(a, b, preferred_element_type=F32)


def _dot_nt(a, b):
    return lax.dot_general(a, b, (((1,), (1,)), ((), ())), preferred_element_type=F32)


def _split2(x):
    hi = x.astype(BF16)
    lo = (x - hi.astype(F32)).astype(BF16)
    return hi, lo


def _dot3(a, b):
    ah, al = _split2(a)
    bh, bl = _split2(b)
    return _dot(jnp.concatenate([ah, ah, al], axis=1), jnp.concatenate([bh, bl, bh], axis=0))


def _dot3_many(a_list, b_list):
    lhs = [jnp.concatenate([ah, ah, al], axis=1) for ah, al in map(_split2, a_list)]
    rhs = [jnp.concatenate([bh, bl, bh], axis=0) for bh, bl in map(_split2, b_list)]
    return [_dot(x, y) for x, y in zip(lhs, rhs)]


def _exact_rows(sel_bf16, x):
    h1 = x.astype(BF16)
    r1 = x - h1.astype(F32)
    h2 = r1.astype(BF16)
    h3 = (r1 - h2.astype(F32)).astype(BF16)
    return _dot(jnp.concatenate([sel_bf16] * 3, axis=1), jnp.concatenate([h1, h2, h3], axis=0))


def _sigmoid(x):
    return 1.0 / (1.0 + jnp.exp(-x))


def _silu(x):
    return x * _sigmoid(x)


def _softplus(x):
    return jnp.maximum(x, 0.0) + jnp.log(1.0 + jnp.exp(-jnp.abs(x)))


def _rms(x, g):
    ms = jnp.mean(x * x, axis=-1, keepdims=True)
    return x * lax.rsqrt(ms + EPS) * g


def _shift_rows(x, k, prev8):
    rolled = pltpu.roll(x, k, 0)
    prev_rolled = pltpu.roll(prev8, k, 0)
    row = lax.broadcasted_iota(jnp.int32, (SUBLANE, x.shape[1]), 0)
    first = jnp.where(row < k, prev_rolled, rolled[0:SUBLANE])
    return jnp.concatenate([first, rolled[SUBLANE:]], axis=0)


def _inproj_kernel(x_ref, g1_ref, wa_ref, wb_ref, ws_ref, wa2_ref, ba_ref, alog_ref, dtb_ref,
                   gla_ref, gdn_ref):
    hn = _rms(x_ref[...], g1_ref[...]).astype(BF16)
    small = _dot(hn, ws_ref[...])
    for c in range(0, W_GLA, 512):
        gla_ref[:, c:c + 512] = _dot(hn, wa_ref[:, c:c + 512])
    for c in range(0, W_GDN, 512):
        gdn_ref[:, c:c + 512] = _dot(hn, wb_ref[:, c:c + 512])
    xa = _dot(small[:, 0:LANE].astype(BF16), wa2_ref[...]) + ba_ref[...]
    gla_ref[:, W_GLA:GLA_COLS] = -_softplus(-xa) * (1.0 / GLA_TAU)
    bd = small[:, LANE:2 * LANE]
    lane = lax.broadcasted_iota(jnp.int32, bd.shape, 1)
    beta = _sigmoid(bd)
    g = -jnp.exp(alog_ref[...]) * _softplus(bd + dtb_ref[...])
    gdn_ref[:, W_GDN:GDN_COLS] = jnp.where(lane < H, beta, jnp.where(lane < 2 * H, g, 0.0))


def _inproj(x2d, g1, w_in_parts, wa2, ba, alog, dtb, tm):
    n = x2d.shape[0]
    const = lambda i: (0, 0)
    return pl.pallas_call(
        _inproj_kernel,
        grid=(n // tm,),
        in_specs=[
            pl.BlockSpec((tm, D_MODEL), lambda i: (i, 0)),
            pl.BlockSpec((1, D_MODEL), const),
            pl.BlockSpec((D_MODEL, W_GLA), const),
            pl.BlockSpec((D_MODEL, W_GDN), const),
            pl.BlockSpec((D_MODEL, 2 * LANE), const),
            pl.BlockSpec((LANE, GLA_QK), const),
            pl.BlockSpec((1, GLA_QK), const),
            pl.BlockSpec((1, LANE), const),
            pl.BlockSpec((1, LANE), const),
        ],
        out_specs=[
            pl.BlockSpec((tm, GLA_COLS), lambda i: (i, 0)),
            pl.BlockSpec((tm, GDN_COLS), lambda i: (i, 0)),
        ],
        out_shape=[
            jax.ShapeDtypeStruct((n, GLA_COLS), F32),
            jax.ShapeDtypeStruct((n, GDN_COLS), F32),
        ],
        compiler_params=pltpu.CompilerParams(
            dimension_semantics=("parallel",), vmem_limit_bytes=VMEM_LIMIT),
        name="inproj",
    )(x2d, g1, *w_in_parts, wa2, ba, alog, dtb)


def _gla_chunk_kernel(gla_ref, s0_ref, gn_ref, o_ref, st_ref, h_scr, *, nc):
    c = pl.program_id(1)
    tc = CHUNK

    @pl.when(c == 0)
    def _():
        h_scr[...] = jnp.zeros_like(h_scr)
        for h in range(H):
            h_scr[h * GLA_DK:(h + 1) * GLA_DK, h * GLA_DV:(h + 1) * GLA_DV] = s0_ref[0, h]

    q = gla_ref[:, 0:GLA_QK] * (GLA_DK ** -0.5)
    k = gla_ref[:, GLA_QK:2 * GLA_QK]
    v = gla_ref[:, 2 * GLA_QK:2 * GLA_QK + GLA_V]
    la = gla_ref[:, W_GLA:GLA_COLS]

    row = lax.broadcasted_iota(jnp.int32, (tc, tc), 0)
    col = lax.broadcasted_iota(jnp.int32, (tc, tc), 1)
    causal = row >= col
    tri = causal.astype(BF16)

    b = _exact_rows(tri, la)
    b_mid = b[tc // 2 - 1:tc // 2, :]
    qm = (q * jnp.exp(b - b_mid)).astype(BF16)
    km = (k * jnp.exp(b_mid - b)).astype(BF16)
    qi = (q * jnp.exp(b)).astype(BF16)

    hbd = h_scr[...]
    o_inter = _dot(qi, hbd.astype(BF16))

    lane_head = lax.broadcasted_iota(jnp.int32, (1, GLA_QK), 1) // GLA_DK
    vb = v.astype(BF16)
    gn = gn_ref[...]
    scores = [_dot_nt(jnp.where(lane_head == h, qm, jnp.zeros_like(qm)), km) for h in range(H)]
    scores = [jnp.where(causal, s, 0.0).astype(BF16) for s in scores]
    o_intra = [_dot(scores[h], vb[:, h * GLA_DV:(h + 1) * GLA_DV]) for h in range(H)]
    for h in range(H):
        sl = slice(h * GLA_DV, (h + 1) * GLA_DV)
        oh = o_intra[h] + o_inter[:, sl]
        gate = gla_ref[:, 2 * GLA_QK + GLA_V + h * GLA_DV:2 * GLA_QK + GLA_V + (h + 1) * GLA_DV]
        o_ref[:, sl] = (_rms(oh, gn) * _silu(gate)).astype(o_ref.dtype)

    bt = b.T
    b_last = bt[:, tc - 1:tc]
    klt = (k.T * jnp.exp(b_last - bt)).astype(BF16)
    upd = _dot(klt, vb)
    rblk = lax.broadcasted_iota(jnp.int32, upd.shape, 0) // GLA_DK
    cblk = lax.broadcasted_iota(jnp.int32, upd.shape, 1) // GLA_DV
    h_new = hbd * jnp.exp(b_last) + jnp.where(rblk == cblk, upd, 0.0)
    h_scr[...] = h_new

    @pl.when(c == nc - 1)
    def _():
        for h in range(H):
            st_ref[0, h] = h_new[h * GLA_DK:(h + 1) * GLA_DK, h * GLA_DV:(h + 1) * GLA_DV]


def _gla_prompt(gla2d, s0, gn, bsz, t):
    nc = t // CHUNK
    return pl.pallas_call(
        functools.partial(_gla_chunk_kernel, nc=nc),
        grid=(bsz, nc),
        in_specs=[
            pl.BlockSpec((CHUNK, GLA_COLS), lambda b, c: (b * nc + c, 0)),
            pl.BlockSpec((1, H, GLA_DK, GLA_DV), lambda b, c: (b, 0, 0, 0)),
            pl.BlockSpec((1, GLA_DV), lambda b, c: (0, 0)),
        ],
        out_specs=[
            pl.BlockSpec((CHUNK, GLA_V), lambda b, c: (b * nc + c, 0)),
            pl.BlockSpec((1, H, GLA_DK, GLA_DV), lambda b, c: (b, 0, 0, 0)),
        ],
        out_shape=[
            jax.ShapeDtypeStruct((bsz * t, GLA_V), BF16),
            jax.ShapeDtypeStruct((bsz, H, GLA_DK, GLA_DV), F32),
        ],
        scratch_shapes=[pltpu.VMEM((GLA_QK, GLA_V), F32)],
        compiler_params=pltpu.CompilerParams(
            dimension_semantics=("parallel", "arbitrary"), vmem_limit_bytes=VMEM_LIMIT),
        name="gla_prompt",
    )(gla2d, s0, gn)


def _tri_inverse(l_strict, eye, order):
    n = eye.shape[0]
    m = [-l for l in l_strict]
    p = [eye + x for x in m]
    m = _dot3_many(m, m)
    power = 2
    while 2 * power < order:
        pm = _dot3_many([jnp.concatenate([a, b], axis=0) for a, b in zip(p, m)], m)
        p = [a + x[:n] for a, x in zip(p, pm)]
        m = [x[n:] for x in pm]
        power *= 2
    return [a + x for a, x in zip(p, _dot3_many(p, m))]


def _gdn_chunk_kernel(gdn_ref, s0_ref, cw_ref, gn_ref, o_ref, st_ref, h_scr, xprev_scr, *, nc):
    c = pl.program_id(1)
    tc = CHUNK

    @pl.when(c == 0)
    def _():
        xprev_scr[...] = jnp.zeros_like(xprev_scr)
        for h in range(H):
            h_scr[h * GDN_D:(h + 1) * GDN_D, :] = s0_ref[0, h]

    x = gdn_ref[:, 0:GDN_CONV_C]
    prev8 = xprev_scr[...]
    y = x * cw_ref[3:4, :]
    for kk in range(1, GDN_CONV_W):
        y = y + _shift_rows(x, kk, prev8) * cw_ref[3 - kk:4 - kk, :]
    xprev_scr[...] = x[tc - SUBLANE:tc, :]
    qkv = _silu(y)

    bg = gdn_ref[:, W_GDN:GDN_COLS]
    row = lax.broadcasted_iota(jnp.int32, (tc, tc), 0)
    col = lax.broadcasted_iota(jnp.int32, (tc, tc), 1)
    causal = row >= col
    strict = row > col
    eye = (row == col).astype(F32)
    gc = _exact_rows(causal.astype(BF16), bg)
    gt = gc.T
    gn = gn_ref[...]

    heads = range(H)
    sls = [slice(h * GDN_D, (h + 1) * GDN_D) for h in heads]
    kf, kb, qb, vf, beta, gcol, decay = [], [], [], [], [], [], []
    for h in heads:
        qh = qkv[:, sls[h]]
        kh = qkv[:, GDN_QK + h * GDN_D:GDN_QK + (h + 1) * GDN_D]
        vf.append(qkv[:, 2 * GDN_QK + h * GDN_D:2 * GDN_QK + (h + 1) * GDN_D])
        qh = qh * lax.rsqrt(jnp.sum(qh * qh, axis=-1, keepdims=True) + EPS) * (GDN_D ** -0.5)
        kh = kh * lax.rsqrt(jnp.sum(kh * kh, axis=-1, keepdims=True) + EPS)
        kf.append(kh)
        kb.append(kh.astype(BF16))
        qb.append(qh.astype(BF16))
        beta.append(bg[:, h:h + 1])
        gcol.append(gc[:, H + h:H + h + 1])
        grow = gt[H + h:H + h + 1, :]
        decay.append(jnp.where(causal, jnp.exp(jnp.where(causal, gcol[h] - grow, 0.0)), 0.0))

    qk_kk = [_dot_nt(jnp.concatenate([qb[h], kb[h]], axis=0), kb[h]) for h in heads]
    hh = [h_scr[sls[h], :] for h in heads]
    kq_h = [_dot(jnp.concatenate([kb[h], qb[h]], axis=0), hh[h].astype(BF16)) for h in heads]
    tinv = _tri_inverse([jnp.where(strict, beta[h] * qk_kk[h][tc:] * decay[h], 0.0) for h in heads], eye, tc)
    eg = [jnp.exp(gcol[h]) for h in heads]
    rhs = [beta[h] * (vf[h] - eg[h] * kq_h[h][:tc]) for h in heads]
    ub = [u.astype(BF16) for u in _dot3_many(tinv, rhs)]
    ou = []
    for h in heads:
        qk = (qk_kk[h][:tc] * decay[h]).astype(BF16)
        g_last = gcol[h][tc - 1:tc, :]
        kd = (kf[h] * jnp.exp(g_last - gcol[h])).T.astype(BF16)
        ou.append(_dot(jnp.concatenate([qk, kd], axis=0), ub[h]))
    for h in heads:
        o = eg[h] * kq_h[h][tc:] + ou[h][:tc]
        h_new = jnp.exp(gcol[h][tc - 1:tc, :]) * hh[h] + ou[h][tc:]
        h_scr[sls[h], :] = h_new
        z = gdn_ref[:, GDN_CONV_C + h * GDN_D:GDN_CONV_C + (h + 1) * GDN_D]
        o_ref[:, sls[h]] = (_rms(o, gn) * _silu(z)).astype(o_ref.dtype)
        st_ref[0, h] = h_new


def _gdn_prompt(gdn2d, s0, cw, gn, bsz, t):
    nc = t // CHUNK
    return pl.pallas_call(
        functools.partial(_gdn_chunk_kernel, nc=nc),
        grid=(bsz, nc),
        in_specs=[
            pl.BlockSpec((CHUNK, GDN_COLS), lambda b, c: (b * nc + c, 0)),
            pl.BlockSpec((1, H, GDN_D, GDN_D), lambda b, c: (b, 0, 0, 0)),
            pl.BlockSpec((GDN_CONV_W, GDN_CONV_C), lambda b, c: (0, 0)),
            pl.BlockSpec((1, GDN_D), lambda b, c: (0, 0)),
        ],
        out_specs=[
            pl.BlockSpec((CHUNK, GDN_V), lambda b, c: (b * nc + c, 0)),
            pl.BlockSpec((1, H, GDN_D, GDN_D), lambda b, c: (b, 0, 0, 0)),
        ],
        out_shape=[
            jax.ShapeDtypeStruct((bsz * t, GDN_V), BF16),
            jax.ShapeDtypeStruct((bsz, H, GDN_D, GDN_D), F32),
        ],
        scratch_shapes=[pltpu.VMEM((H * GDN_D, GDN_D), F32),
                        pltpu.VMEM((SUBLANE, GDN_CONV_C), F32)],
        compiler_params=pltpu.CompilerParams(
            dimension_semantics=("parallel", "arbitrary"), vmem_limit_bytes=VMEM_LIMIT),
        name="gdn_prompt",
    )(gdn2d, s0, cw, gn)


SEQ_PER_STEP = 16


def _mix_sample_kernel(gla_ref, gdn_ref, h1_ref, h2_ref, h3_ref, sa_ref, sb_ref, cw_ref, gna_ref, gnb_ref,
                       oa_ref, ob_ref, sta_ref, stb_ref, cv1_ref, cv2_ref, cv3_ref, a_scr, b_scr, *, t):
    n = SEQ_PER_STEP
    rows = n * t
    tc = LANE
    spg = SUBLANE // t

    @pl.when(pl.program_id(0) == 0)
    def _():
        a_scr[...] = jnp.zeros_like(a_scr)
        b_scr[...] = jnp.zeros_like(b_scr)

    nh = GDN_CONV_W - 1
    st_refs = (h1_ref, h2_ref, h3_ref)
    cv_refs = (cv1_ref, cv2_ref, cv3_ref)
    x = gdn_ref[:, 0:GDN_CONV_C]
    pos = lax.broadcasted_iota(jnp.int32, x.shape, 0) % t
    r_ts = lax.broadcasted_iota(jnp.int32, (rows, n), 0)
    s_ts = lax.broadcasted_iota(jnp.int32, (rows, n), 1)
    spread = [(r_ts == s_ts * t + p).astype(BF16) for p in range(nh)]
    r_st = lax.broadcasted_iota(jnp.int32, (n, rows), 1)
    s_st = lax.broadcasted_iota(jnp.int32, (n, rows), 0)
    y = x * cw_ref[GDN_CONV_W - 1:GDN_CONV_W, :]
    for kk in range(1, GDN_CONV_W):
        hist = _exact_rows(jnp.concatenate(spread[:kk], axis=1),
                           jnp.concatenate([st_refs[nh + p - kk][...] for p in range(kk)], axis=0))
        m = jnp.where(pos >= kk, pltpu.roll(x, kk, 0), hist)
        y = y + m * cw_ref[GDN_CONV_W - 1 - kk:GDN_CONV_W - kk, :]
    for j in range(nh):
        cv_refs[j][...] = _exact_rows((r_st == s_st * t + t - nh + j).astype(BF16), x)
    b_scr[0:rows, 0:GDN_CONV_C] = _silu(y)
    b_scr[0:rows, GDN_CONV_C:GDN_CONV_C + LANE] = gdn_ref[:, W_GDN:GDN_COLS]
    a_scr[0:rows, :] = gla_ref[...]

    row = lax.broadcasted_iota(jnp.int32, (tc, tc), 0)
    col = lax.broadcasted_iota(jnp.int32, (tc, tc), 1)
    same = (row // t) == (col // t)
    causal = same & (row >= col)
    strict = same & (row > col)
    eye = (row == col).astype(F32)
    tri = causal.astype(BF16)
    last = (col == (row // t) * t + (t - 1)).astype(BF16)
    lane_seq = lax.broadcasted_iota(jnp.int32, (1, tc), 1) // t
    sub = lax.broadcasted_iota(jnp.int32, (SUBLANE, LANE), 0)

    def pick_rows(res, off):
        groups = []
        for g in range(rows // SUBLANE):
            sl = slice(off + g * SUBLANE, off + (g + 1) * SUBLANE)
            piece = res[g * spg + spg - 1][sl]
            for j in reversed(range(spg - 1)):
                piece = jnp.where(sub < (j + 1) * t, res[g * spg + j][sl], piece)
            groups.append(piece)
        return jnp.concatenate(groups, axis=0)

    q = a_scr[:, 0:GLA_QK] * (GLA_DK ** -0.5)
    k = a_scr[:, GLA_QK:2 * GLA_QK]
    vb = a_scr[:, 2 * GLA_QK:2 * GLA_QK + GLA_V].astype(BF16)
    b = _exact_rows(tri, a_scr[:, W_GLA:GLA_COLS])
    b_end = _exact_rows(last, b)
    qe = (q * jnp.exp(b)).astype(BF16)
    ke = (k * jnp.exp(-b)).astype(BF16)
    klt = (k * jnp.exp(b_end - b)).T.astype(BF16)
    bt = b.T
    lane_head = lax.broadcasted_iota(jnp.int32, (1, GLA_QK), 1) // GLA_DK
    qmask = [jnp.where(lane_head == h, qe, jnp.zeros_like(qe)) for h in range(H)]
    scores = [_dot_nt(qmask[h], ke) for h in range(H)]
    scores = [jnp.where(causal, s, 0.0).astype(BF16) for s in scores]
    o_intra = [_dot(scores[h], vb[:, h * GLA_DV:(h + 1) * GLA_DV]) for h in range(H)]
    lhs_q = jnp.concatenate(qmask, axis=0)
    hs = [jnp.concatenate([sa_ref[s, h] for h in range(H)], axis=0) for s in range(n)]
    res = [_dot(lhs_q, hs[s].astype(BF16)) for s in range(n)]
    upd = [_dot(jnp.where(lane_seq == s, klt, jnp.zeros_like(klt)), vb) for s in range(n)]
    gna = gna_ref[...]
    for h in range(H):
        sl = slice(h * GLA_DV, (h + 1) * GLA_DV)
        oh = o_intra[h][0:rows] + pick_rows(res, h * tc)
        gate = gla_ref[:, 2 * GLA_QK + GLA_V + h * GLA_DV:2 * GLA_QK + GLA_V + (h + 1) * GLA_DV]
        oa_ref[:, sl] = (_rms(oh, gna) * _silu(gate)).astype(oa_ref.dtype)
    for s in range(n):
        e_col = jnp.exp(bt[:, s * t + t - 1:s * t + t])
        diag = jnp.concatenate(
            [upd[s][h * GLA_DK:(h + 1) * GLA_DK, h * GLA_DV:(h + 1) * GLA_DV] for h in range(H)], axis=0)
        new = hs[s] * e_col + diag
        for h in range(H):
            sta_ref[s, h] = new[h * GLA_DK:(h + 1) * GLA_DK]

    bg = b_scr[:, GDN_CONV_C:GDN_CONV_C + LANE]
    gc = _exact_rows(tri, bg)
    gl = _exact_rows(last, gc)
    gt = gc.T
    heads = range(H)
    sls = [slice(h * GDN_D, (h + 1) * GDN_D) for h in heads]
    kf, kb, qb, vf, beta, gcol, decay = [], [], [], [], [], [], []
    for h in heads:
        qh = b_scr[:, sls[h]]
        kh = b_scr[:, GDN_QK + h * GDN_D:GDN_QK + (h + 1) * GDN_D]
        vf.append(b_scr[:, 2 * GDN_QK + h * GDN_D:2 * GDN_QK + (h + 1) * GDN_D])
        qh = qh * lax.rsqrt(jnp.sum(qh * qh, axis=-1, keepdims=True) + EPS) * (GDN_D ** -0.5)
        kh = kh * lax.rsqrt(jnp.sum(kh * kh, axis=-1, keepdims=True) + EPS)
        kf.append(kh)
        kb.append(kh.astype(BF16))
        qb.append(qh.astype(BF16))
        beta.append(bg[:, h:h + 1])
        gcol.append(gc[:, H + h:H + h + 1])
        grow = gt[H + h:H + h + 1, :]
        decay.append(jnp.where(causal, jnp.exp(jnp.where(causal, gcol[h] - grow, 0.0)), 0.0))
    qk_kk = [_dot_nt(jnp.concatenate([qb[h], kb[h]], axis=0), kb[h]) for h in heads]
    hsb = [[sb_ref[s, h] for h in heads] for s in range(n)]
    kq = [[_dot(jnp.concatenate([kb[h], qb[h]], axis=0), hsb[s][h].astype(BF16)) for s in range(n)]
          for h in heads]
    tinv = _tri_inverse([jnp.where(strict, beta[h] * qk_kk[h][tc:] * decay[h], 0.0) for h in heads], eye, t)
    pad_rows = jnp.zeros((tc - rows, GDN_D), F32)
    eg = [jnp.exp(gcol[h]) for h in heads]
    k_h = [jnp.concatenate([pick_rows(kq[h], 0), pad_rows], axis=0) for h in heads]
    rhs = [beta[h] * (vf[h] - eg[h] * k_h[h]) for h in heads]
    ub = [u.astype(BF16) for u in _dot3_many(tinv, rhs)]
    qku = [_dot((qk_kk[h][:tc] * decay[h]).astype(BF16), ub[h]) for h in heads]
    kdt = [(kf[h] * jnp.exp(gl[:, H + h:H + h + 1] - gcol[h])).T.astype(BF16) for h in heads]
    updb = [[_dot(jnp.where(lane_seq == s, kdt[h], jnp.zeros_like(kdt[h])), ub[h]) for s in range(n)]
            for h in heads]
    gnb = gnb_ref[...]
    for h in heads:
        o = eg[h][0:rows] * pick_rows(kq[h], tc) + qku[h][0:rows]
        z = gdn_ref[:, GDN_CONV_C + h * GDN_D:GDN_CONV_C + (h + 1) * GDN_D]
        ob_ref[:, sls[h]] = (_rms(o, gnb) * _silu(z)).astype(ob_ref.dtype)
        for s in range(n):
            e_last = jnp.exp(gl[s * t:s * t + 1, H + h:H + h + 1])
            stb_ref[s, h] = e_last * hsb[s][h] + updb[h][s]


def _mix_sample(gla2d, gdn2d, hists, sa, sb, cw, gna, gnb, t):
    n_rows = gla2d.shape[0]
    rows = SEQ_PER_STEP * t
    assert SUBLANE % t == 0 and rows <= LANE and n_rows % rows == 0
    bsz = n_rows // t
    r2 = lambda i: (i, 0)
    i4 = lambda i: (i, 0, 0, 0)
    c2 = lambda i: (0, 0)
    return pl.pallas_call(
        functools.partial(_mix_sample_kernel, t=t),
        grid=(n_rows // rows,),
        in_specs=[
            pl.BlockSpec((rows, GLA_COLS), r2),
            pl.BlockSpec((rows, GDN_COLS), r2),
            pl.BlockSpec((SEQ_PER_STEP, GDN_CONV_C), r2),
            pl.BlockSpec((SEQ_PER_STEP, GDN_CONV_C), r2),
            pl.BlockSpec((SEQ_PER_STEP, GDN_CONV_C), r2),
            pl.BlockSpec((SEQ_PER_STEP, H, GLA_DK, GLA_DV), i4),
            pl.BlockSpec((SEQ_PER_STEP, H, GDN_D, GDN_D), i4),
            pl.BlockSpec((GDN_CONV_W, GDN_CONV_C), c2),
            pl.BlockSpec((1, GLA_DV), c2),
            pl.BlockSpec((1, GDN_D), c2),
        ],
        out_specs=[
            pl.BlockSpec((rows, GLA_V), r2),
            pl.BlockSpec((rows, GDN_V), r2),
            pl.BlockSpec((SEQ_PER_STEP, H, GLA_DK, GLA_DV), i4),
            pl.BlockSpec((SEQ_PER_STEP, H, GDN_D, GDN_D), i4),
        ] + [pl.BlockSpec((SEQ_PER_STEP, GDN_CONV_C), r2)] * (GDN_CONV_W - 1),
        out_shape=[
            jax.ShapeDtypeStruct((n_rows, GLA_V), BF16),
            jax.ShapeDtypeStruct((n_rows, GDN_V), BF16),
            jax.ShapeDtypeStruct((bsz, H, GLA_DK, GLA_DV), F32),
            jax.ShapeDtypeStruct((bsz, H, GDN_D, GDN_D), F32),
        ] + [jax.ShapeDtypeStruct((bsz, GDN_CONV_C), F32)] * (GDN_CONV_W - 1),
        scratch_shapes=[pltpu.VMEM((LANE, GLA_COLS), F32),
                        pltpu.VMEM((LANE, GDN_CONV_C + LANE), F32)],
        compiler_params=pltpu.CompilerParams(
            dimension_semantics=("arbitrary",), vmem_limit_bytes=VMEM_LIMIT),
        name="mix_sample",
    )(gla2d, gdn2d, *hists, sa, sb, cw, gna, gnb)


def _ffn_kernel(*refs, seq_tiles, has_hist, seq_len):
    nh = FFN_CONV_W - 1
    if has_hist:
        (x_ref, oa_ref, ob_ref, wo_ref, g2_ref, wu_ref, cw_ref, wd_ref, gf_ref) = refs[:9]
        st_refs = refs[9:9 + nh]
        y_ref = refs[9 + nh]
        last_refs = refs[10 + nh:10 + 2 * nh]
    else:
        (x_ref, oa_ref, ob_ref, wo_ref, g2_ref, wu_ref, cw_ref, wd_ref, gf_ref,
         y_ref, tail_ref, u_scr) = refs
    tm = x_ref.shape[0]

    if not has_hist:
        @pl.when(pl.program_id(0) % seq_tiles == 0)
        def _():
            u_scr[0:SUBLANE, :] = jnp.zeros((SUBLANE, 2 * D_FF), F32)

    x1 = (x_ref[...] + _dot(oa_ref[...].astype(BF16), wo_ref[0:GLA_V, :])
          + _dot(ob_ref[...].astype(BF16), wo_ref[GLA_V:GLA_V + GDN_V, :]))
    hn = _rms(x1, g2_ref[...]).astype(BF16)

    if has_hist:
        nseq = tm // seq_len
        pos = lax.broadcasted_iota(jnp.int32, (tm, FF_CK), 0) % seq_len
        r_ts = lax.broadcasted_iota(jnp.int32, (tm, nseq), 0)
        s_ts = lax.broadcasted_iota(jnp.int32, (tm, nseq), 1)
        spread = [(r_ts == s_ts * seq_len + p).astype(BF16) for p in range(nh)]
        r_st = lax.broadcasted_iota(jnp.int32, (nseq, tm), 1)
        s_st = lax.broadcasted_iota(jnp.int32, (nseq, tm), 0)
        gather = [(r_st == s_st * seq_len + seq_len - nh + j).astype(BF16) for j in range(nh)]

    def conv(u, off):
        cols = slice(off, off + FF_CK)
        if has_hist:
            delayed = []
            for k in range(1, nh + 1):
                hist = _exact_rows(jnp.concatenate(spread[:k], axis=1),
                                   jnp.concatenate([st_refs[nh + p - k][:, cols] for p in range(k)], axis=0))
                delayed.append(jnp.where(pos >= k, pltpu.roll(u, k, 0), hist))
            m1, m2 = delayed
            for j in range(nh):
                last_refs[j][:, cols] = _exact_rows(gather[j], u)
        else:
            u = u_scr[SUBLANE:SUBLANE + tm, cols]
            m1 = u_scr[SUBLANE - 1:SUBLANE - 1 + tm, cols]
            m2 = u_scr[SUBLANE - 2:SUBLANE - 2 + tm, cols]
            tail = u_scr[tm:tm + SUBLANE, cols]
            tail_ref[0, :, cols] = tail
            u_scr[0:SUBLANE, cols] = tail
        return (m2 * cw_ref[0:1, cols] + m1 * cw_ref[1:2, cols] + u * cw_ref[2:3, cols])

    def up(c):
        ua = _dot(hn, wu_ref[:, c:c + FF_CK])
        ub = _dot(hn, wu_ref[:, D_FF + c:D_FF + c + FF_CK])
        if has_hist:
            return ua, ub
        u_scr[SUBLANE:SUBLANE + tm, c:c + FF_CK] = ua
        u_scr[SUBLANE:SUBLANE + tm, D_FF + c:D_FF + c + FF_CK] = ub
        return None, None

    acc = jnp.zeros((tm, D_MODEL), F32)
    ahead = 1 if has_hist else FF_LOOKAHEAD
    steps = list(range(0, D_FF, FF_CK))
    pending = [up(c) for c in steps[:ahead]]
    for i, c in enumerate(steps):
        if i + ahead < len(steps):
            pending.append(up(steps[i + ahead]))
        ua, ub = pending.pop(0)
        a = conv(ua, c)
        b = conv(ub, D_FF + c)
        act = (_silu(a) * b).astype(BF16)
        acc = acc + _dot(act, wd_ref[c:c + FF_CK, :])
    y_ref[...] = _rms(x1 + acc, gf_ref[...])


def _ffn(x2d, oa, ob, wo, g2, wu, cw, wd, gf, tm, seq_len, hist=None):
    n = x2d.shape[0]
    has_hist = hist is not None
    row = lambda i: (i, 0)
    const = lambda i: (0, 0)
    in_specs = [
        pl.BlockSpec((tm, D_MODEL), row),
        pl.BlockSpec((tm, GLA_V), row),
        pl.BlockSpec((tm, GDN_V), row),
        pl.BlockSpec((GLA_V + GDN_V, D_MODEL), const),
        pl.BlockSpec((1, D_MODEL), const),
        pl.BlockSpec((D_MODEL, 2 * D_FF), const),
        pl.BlockSpec((FFN_CONV_W, 2 * D_FF), const),
        pl.BlockSpec((D_FF, D_MODEL), const),
        pl.BlockSpec((1, D_MODEL), const),
    ]
    args = [x2d, oa, ob, wo, g2, wu, cw, wd, gf]
    out_specs = [pl.BlockSpec((tm, D_MODEL), row)]
    out_shape = [jax.ShapeDtypeStruct((n, D_MODEL), F32)]
    scratch = []
    if has_hist:
        assert tm % seq_len == 0
        seq_tiles = 1
        nseq = tm // seq_len
        assert len(hist) == FFN_CONV_W - 1 and seq_len >= FFN_CONV_W - 1
        in_specs += [pl.BlockSpec((nseq, 2 * D_FF), row)] * len(hist)
        args += list(hist)
        out_specs += [pl.BlockSpec((nseq, 2 * D_FF), row)] * len(hist)
        out_shape += [jax.ShapeDtypeStruct((n // seq_len, 2 * D_FF), F32)] * len(hist)
    else:
        assert seq_len % tm == 0
        seq_tiles = seq_len // tm
        out_specs.append(pl.BlockSpec((1, SUBLANE, 2 * D_FF), lambda i: (i, 0, 0)))
        out_shape.append(jax.ShapeDtypeStruct((n // tm, SUBLANE, 2 * D_FF), F32))
        scratch.append(pltpu.VMEM((SUBLANE + tm, 2 * D_FF), F32))
    return pl.pallas_call(
        functools.partial(_ffn_kernel, seq_tiles=seq_tiles, has_hist=has_hist, seq_len=seq_len),
        grid=(n // tm,),
        in_specs=in_specs,
        out_specs=out_specs,
        out_shape=out_shape,
        scratch_shapes=scratch,
        compiler_params=pltpu.CompilerParams(
            dimension_semantics=("arbitrary",), vmem_limit_bytes=VMEM_LIMIT),
        name="ffn_sample" if has_hist else "ffn_prompt",
    )(*args)


def _pad_cols(a, width):
    return jnp.pad(a, ((0, 0), (0, width - a.shape[1])))


def kernel(x_prompt, x_sample, state_gla, state_gdn, state_gdn_conv, state_ffn_conv, norm1_g, w_in, gla_w_a2, gla_b_a, gla_norm_g, gdn_conv_w, gdn_a_log, gdn_dt_bias, gdn_norm_g, w_out, norm2_g, w_up, ffn_conv_w, w_down, norm_f_g):
    assert w_in.shape[0] == 1, "single layer"
    bp, tp, _ = x_prompt.shape
    bs, ts, _ = x_sample.shape

    offs = [0]
    for s in IN_SIZES:
        offs.append(offs[-1] + s)
    wi = w_in[0]
    w1 = (wi[:, offs[0]:offs[4]].astype(BF16),
          wi[:, offs[5]:offs[7]].astype(BF16),
          jnp.concatenate([_pad_cols(wi[:, offs[4]:offs[5]], LANE),
                           _pad_cols(wi[:, offs[7]:offs[9]], LANE)], axis=1).astype(BF16))
    wa2 = jnp.pad(gla_w_a2[0], ((0, LANE - GLA_RANK), (0, 0))).astype(BF16)
    ba = gla_b_a[0][None, :]
    alog = jnp.pad(gdn_a_log[0], (H, LANE - 2 * H))[None, :]
    dtb = jnp.pad(gdn_dt_bias[0], (H, LANE - 2 * H))[None, :]
    g1 = norm1_g[0][None, :]
    g2 = norm2_g[0][None, :]
    gf = norm_f_g[None, :]
    gna = gla_norm_g[0][None, :]
    gnb = gdn_norm_g[0][None, :]
    cwb = gdn_conv_w[0]
    cwf = ffn_conv_w[0]
    wo = w_out[0].astype(BF16)
    wu = w_up[0].astype(BF16)
    wd = w_down[0].astype(BF16)

    xp = x_prompt.reshape(bp * tp, D_MODEL)
    gla_p, gdn_p = _inproj(xp, g1, w1, wa2, ba, alog, dtb, tm=512)
    oa_p, p_gla = _gla_prompt(gla_p, jnp.zeros((bp,) + state_gla.shape[2:], F32), gna, bp, tp)
    ob_p, p_gdn = _gdn_prompt(gdn_p, jnp.zeros((bp,) + state_gdn.shape[2:], F32), cwb, gnb, bp, tp)
    tm_p = 512
    y_p, tail_p = _ffn(xp, oa_p, ob_p, wo, g2, wu, cwf, wd, gf, tm=tm_p, seq_len=tp)
    y_prompt = y_p.reshape(bp, tp, D_MODEL)
    p_conv = gdn_p.reshape(bp, tp, GDN_COLS)[:, tp - (GDN_CONV_W - 1):, :GDN_CONV_C]
    p_ffn = tail_p.reshape(bp, tp // tm_p, SUBLANE, 2 * D_FF)[:, -1, SUBLANE - (FFN_CONV_W - 1):, :]

    xs = x_sample.reshape(bs * ts, D_MODEL)
    gla_s, gdn_s = _inproj(xs, g1, w1, wa2, ba, alog, dtb, tm=256)
    oa_s, ob_s, s_gla, s_gdn, *conv_planes = _mix_sample(
        gla_s, gdn_s, [state_gdn_conv[0][:, j] for j in range(GDN_CONV_W - 1)],
        state_gla[0], state_gdn[0], cwb, gna, gnb, ts)
    y_s, *ffn_planes = _ffn(xs, oa_s, ob_s, wo, g2, wu, cwf, wd, gf, tm=128, seq_len=ts,
                            hist=[state_ffn_conv[0][:, j] for j in range(FFN_CONV_W - 1)])
    y_sample = y_s.reshape(bs, ts, D_MODEL)
    s_conv = jnp.stack(conv_planes, axis=1)
    s_ffn = jnp.stack(ffn_planes, axis=1)

    return (y_prompt, y_sample, p_gla[None], p_gdn[None], p_conv[None], p_ffn[None],
            s_gla[None], s_gdn[None], s_conv[None], s_ffn[None])
```

```python
import functools

import jax
import jax.numpy as jnp
from jax import lax
from jax.experimental import pallas as pl
from jax.experimental.pallas import tpu as pltpu

F32 = jnp.float32
BF16 = jnp.bfloat16

D_MODEL = 1024
H = 4
GLA_DK = 64
GLA_DV = 128
GLA_RANK = 16
GLA_TAU = 16.0
GDN_D = 128
GDN_CONV_W = 4
D_FF = 2816
FFN_CONV_W = 3
EPS = 1e-6

GLA_QK = H * GLA_DK
GLA_V = H * GLA_DV
GDN_QK = H * GDN_D
GDN_V = H * GDN_D
GDN_CONV_C = 2 * GDN_QK + GDN_V
IN_SIZES = (GLA_QK, GLA_QK, GLA_V, GLA_V, GLA_RANK, GDN_CONV_C, GDN_V, H, H)

LANE = 128
SUBLANE = 8
VMEM_LIMIT = 56 * 1024 * 1024

W_GLA = 2 * GLA_QK + 2 * GLA_V
W_GDN = GDN_CONV_C + GDN_V
GLA_COLS = W_GLA + GLA_QK
GDN_COLS = W_GDN + LANE

CHUNK = 128
CHUNKS_PER_STEP = 4
FF_CK = 256
FF_LOOKAHEAD = 11


def _dot(a, b):
    return jnp.dot(a, b, preferred_element_type=F32)


def _dot_nt(a, b):
    return lax.dot_general(a, b, (((1,), (1,)), ((), ())), preferred_element_type=F32)


def _split2(x):
    hi = x.astype(BF16)
    lo = (x - hi.astype(F32)).astype(BF16)
    return hi, lo


def _dot3(a, b):
    ah, al = _split2(a)
    bh, bl = _split2(b)
    return _dot(jnp.concatenate([ah, ah, al], axis=1), jnp.concatenate([bh, bl, bh], axis=0))


def _dot3_many(a_list, b_list):
    lhs = [jnp.concatenate([ah, ah, al], axis=1) for ah, al in map(_split2, a_list)]
    rhs = [jnp.concatenate([bh, bl, bh], axis=0) for bh, bl in map(_split2, b_list)]
    return [_dot(x, y) for x, y in zip(lhs, rhs)]


def _exact_rows(sel_bf16, x):
    h1 = x.astype(BF16)
    r1 = x - h1.astype(F32)
    h2 = r1.astype(BF16)
    h3 = (r1 - h2.astype(F32)).astype(BF16)
    return _dot(jnp.concatenate([sel_bf16] * 3, axis=1), jnp.concatenate([h1, h2, h3], axis=0))


def _sigmoid(x):
    return 1.0 / (1.0 + jnp.exp(-x))


def _silu(x):
    return x * _sigmoid(x)


def _softplus(x):
    return jnp.maximum(x, 0.0) + jnp.log(1.0 + jnp.exp(-jnp.abs(x)))


def _rms(x, g):
    ms = jnp.mean(x * x, axis=-1, keepdims=True)
    return x * lax.rsqrt(ms + EPS) * g


def _shift_rows(x, k, prev8):
    rolled = pltpu.roll(x, k, 0)
    prev_rolled = pltpu.roll(prev8, k, 0)
    row = lax.broadcasted_iota(jnp.int32, (SUBLANE, x.shape[1]), 0)
    first = jnp.where(row < k, prev_rolled, rolled[0:SUBLANE])
    return jnp.concatenate([first, rolled[SUBLANE:]], axis=0)


def _gdn_qkv_activation(y, off):
    a = _silu(y)
    outs = []
    for j in range(0, a.shape[1], GDN_D):
        blk = a[:, j:j + GDN_D]
        if off + j < 2 * GDN_QK:
            blk = blk * lax.rsqrt(jnp.sum(blk * blk, axis=-1, keepdims=True) + EPS)
            if off + j < GDN_QK:
                blk = blk * (GDN_D ** -0.5)
        outs.append(blk)
    return jnp.concatenate(outs, axis=1)


def _inproj_kernel(*refs, seq_tiles):
    if seq_tiles is None:
        (x_ref, g1_ref, wa_ref, wb_ref, ws_ref, wa2_ref, ba_ref, alog_ref, dtb_ref,
         gla_ref, gdn_ref) = refs
    else:
        (x_ref, g1_ref, wa_ref, wb_ref, ws_ref, wa2_ref, ba_ref, alog_ref, dtb_ref, cw_ref,
         gla_ref, gdn_ref, tail_ref, xs_scr) = refs
    tm = x_ref.shape[0]
    hn = _rms(x_ref[...], g1_ref[...]).astype(BF16)
    small = _dot(hn, ws_ref[...])
    if seq_tiles is None:
        for c in range(0, W_GLA, 512):
            gla_ref[:, c:c + 512] = _dot(hn, wa_ref[:, c:c + 512])
        for c in range(0, W_GDN, 512):
            gdn_ref[:, c:c + 512] = _dot(hn, wb_ref[:, c:c + 512])
    else:
        @pl.when(pl.program_id(0) % seq_tiles == 0)
        def _():
            xs_scr[0:SUBLANE, :] = jnp.zeros((SUBLANE, GDN_CONV_C), F32)

        for c in range(0, GDN_CONV_C, 512):
            xs_scr[SUBLANE:SUBLANE + tm, c:c + 512] = _dot(hn, wb_ref[:, c:c + 512])
        gdn_ref[:, GDN_CONV_C:W_GDN] = _dot(hn, wb_ref[:, GDN_CONV_C:W_GDN])
        for c in range(0, GDN_CONV_C, 512):
            gla_ref[:, c:c + 512] = _dot(hn, wa_ref[:, c:c + 512])
            cols = slice(c, c + 512)
            y = xs_scr[SUBLANE:SUBLANE + tm, cols] * cw_ref[GDN_CONV_W - 1:GDN_CONV_W, cols]
            for kk in range(1, GDN_CONV_W):
                y = y + (xs_scr[SUBLANE - kk:SUBLANE - kk + tm, cols]
                         * cw_ref[GDN_CONV_W - 1 - kk:GDN_CONV_W - kk, cols])
            gdn_ref[:, cols] = _gdn_qkv_activation(y, c)
            tail = xs_scr[tm:tm + SUBLANE, cols]
            tail_ref[0, :, cols] = tail
            xs_scr[0:SUBLANE, cols] = tail
    xa = _dot(small[:, 0:LANE].astype(BF16), wa2_ref[...]) + ba_ref[...]
    gla_ref[:, W_GLA:GLA_COLS] = -_softplus(-xa) * (1.0 / GLA_TAU)
    bd = small[:, LANE:2 * LANE]
    lane = lax.broadcasted_iota(jnp.int32, bd.shape, 1)
    beta = _sigmoid(bd)
    g = -jnp.exp(alog_ref[...]) * _softplus(bd + dtb_ref[...])
    gdn_ref[:, W_GDN:GDN_COLS] = jnp.where(lane < H, beta, jnp.where(lane < 2 * H, g, 0.0))


def _inproj(x2d, g1, w_in_parts, wa2, ba, alog, dtb, tm, conv_w=None, seq_len=None):
    n = x2d.shape[0]
    const = lambda i: (0, 0)
    fused_conv = conv_w is not None
    in_specs = [
        pl.BlockSpec((tm, D_MODEL), lambda i: (i, 0)),
        pl.BlockSpec((1, D_MODEL), const),
        pl.BlockSpec((D_MODEL, W_GLA), const),
        pl.BlockSpec((D_MODEL, W_GDN), const),
        pl.BlockSpec((D_MODEL, 2 * LANE), const),
        pl.BlockSpec((LANE, GLA_QK), const),
        pl.BlockSpec((1, GLA_QK), const),
        pl.BlockSpec((1, LANE), const),
        pl.BlockSpec((1, LANE), const),
    ]
    args = [x2d, g1, *w_in_parts, wa2, ba, alog, dtb]
    out_specs = [
        pl.BlockSpec((tm, GLA_COLS), lambda i: (i, 0)),
        pl.BlockSpec((tm, GDN_COLS), lambda i: (i, 0)),
    ]
    out_shape = [
        jax.ShapeDtypeStruct((n, GLA_COLS), F32),
        jax.ShapeDtypeStruct((n, GDN_COLS), F32),
    ]
    scratch = []
    seq_tiles = None
    if fused_conv:
        assert seq_len % tm == 0
        seq_tiles = seq_len // tm
        in_specs.append(pl.BlockSpec((GDN_CONV_W, GDN_CONV_C), const))
        args.append(conv_w)
        out_specs.append(pl.BlockSpec((1, SUBLANE, GDN_CONV_C), lambda i: (i, 0, 0)))
        out_shape.append(jax.ShapeDtypeStruct((n // tm, SUBLANE, GDN_CONV_C), F32))
        scratch.append(pltpu.VMEM((SUBLANE + tm, GDN_CONV_C), F32))
    return pl.pallas_call(
        functools.partial(_inproj_kernel, seq_tiles=seq_tiles),
        grid=(n // tm,),
        in_specs=in_specs,
        out_specs=out_specs,
        out_shape=out_shape,
        scratch_shapes=scratch,
        compiler_params=pltpu.CompilerParams(
            dimension_semantics=("arbitrary" if fused_conv else "parallel",),
            vmem_limit_bytes=VMEM_LIMIT),
        name="inproj_conv" if fused_conv else "inproj",
    )(*args)


def _gla_chunk_kernel(gla_ref, gn_ref, o_ref, st_ref, h_scr):
    tc = CHUNK

    @pl.when(pl.program_id(1) == 0)
    def _():
        h_scr[...] = jnp.zeros_like(h_scr)

    row = lax.broadcasted_iota(jnp.int32, (tc, tc), 0)
    col = lax.broadcasted_iota(jnp.int32, (tc, tc), 1)
    causal = row >= col
    tri = causal.astype(BF16)
    lane_head = lax.broadcasted_iota(jnp.int32, (1, GLA_QK), 1) // GLA_DK
    rblk = lax.broadcasted_iota(jnp.int32, (GLA_QK, GLA_V), 0) // GLA_DK
    cblk = lax.broadcasted_iota(jnp.int32, (GLA_QK, GLA_V), 1) // GLA_DV
    gn = gn_ref[...]
    hbd = h_scr[...]

    for j in range(CHUNKS_PER_STEP):
        rs = slice(j * tc, (j + 1) * tc)
        q = gla_ref[rs, 0:GLA_QK] * (GLA_DK ** -0.5)
        k = gla_ref[rs, GLA_QK:2 * GLA_QK]
        vb = gla_ref[rs, 2 * GLA_QK:2 * GLA_QK + GLA_V].astype(BF16)
        b = _exact_rows(tri, gla_ref[rs, W_GLA:GLA_COLS])
        b_mid = b[tc // 2 - 1:tc // 2, :]
        qm = (q * jnp.exp(b - b_mid)).astype(BF16)
        km = (k * jnp.exp(b_mid - b)).astype(BF16)
        qi = (q * jnp.exp(b)).astype(BF16)
        o_inter = _dot(qi, hbd.astype(BF16))
        scores = [_dot_nt(jnp.where(lane_head == h, qm, jnp.zeros_like(qm)), km) for h in range(H)]
        scores = [jnp.where(causal, s, 0.0).astype(BF16) for s in scores]
        o_intra = [_dot(scores[h], vb[:, h * GLA_DV:(h + 1) * GLA_DV]) for h in range(H)]
        for h in range(H):
            sl = slice(h * GLA_DV, (h + 1) * GLA_DV)
            oh = o_intra[h] + o_inter[:, sl]
            gate = gla_ref[rs, 2 * GLA_QK + GLA_V + h * GLA_DV:2 * GLA_QK + GLA_V + (h + 1) * GLA_DV]
            o_ref[rs, sl] = (_rms(oh, gn) * _silu(gate)).astype(o_ref.dtype)
        bt = b.T
        b_last = bt[:, tc - 1:tc]
        klt = (k.T * jnp.exp(b_last - bt)).astype(BF16)
        hbd = hbd * jnp.exp(b_last) + jnp.where(rblk == cblk, _dot(klt, vb), 0.0)

    h_scr[...] = hbd
    for h in range(H):
        st_ref[0, h] = hbd[h * GLA_DK:(h + 1) * GLA_DK, h * GLA_DV:(h + 1) * GLA_DV]


def _gla_prompt(gla2d, gn, bsz, t):
    step = CHUNKS_PER_STEP * CHUNK
    assert t % step == 0
    nc = t // step
    return pl.pallas_call(
        _gla_chunk_kernel,
        grid=(bsz, nc),
        in_specs=[
            pl.BlockSpec((step, GLA_COLS), lambda b, c: (b * nc + c, 0)),
            pl.BlockSpec((1, GLA_DV), lambda b, c: (0, 0)),
        ],
        out_specs=[
            pl.BlockSpec((step, GLA_V), lambda b, c: (b * nc + c, 0)),
            pl.BlockSpec((1, H, GLA_DK, GLA_DV), lambda b, c: (b, 0, 0, 0)),
        ],
        out_shape=[
            jax.ShapeDtypeStruct((bsz * t, GLA_V), BF16),
            jax.ShapeDtypeStruct((bsz, H, GLA_DK, GLA_DV), F32),
        ],
        scratch_shapes=[pltpu.VMEM((GLA_QK, GLA_V), F32)],
        compiler_params=pltpu.CompilerParams(
            dimension_semantics=("parallel", "arbitrary"), vmem_limit_bytes=VMEM_LIMIT),
        name="gla_prompt",
    )(gla2d, gn)


def _tri_inverse(l_strict, eye, order):
    n = eye.shape[0]
    m = [-l for l in l_strict]
    p = [eye + x for x in m]
    m = _dot3_many(m, m)
    power = 2
    while 2 * power < order:
        pm = _dot3_many([jnp.concatenate([a, b], axis=0) for a, b in zip(p, m)], m)
        p = [a + x[:n] for a, x in zip(p, pm)]
        m = [x[n:] for x in pm]
        power *= 2
    return [a + x for a, x in zip(p, _dot3_many(p, m))]


def _gdn_chunk_kernel(gdn_ref, gn_ref, o_ref, st_ref, h_scr):
    tc = CHUNK

    @pl.when(pl.program_id(1) == 0)
    def _():
        h_scr[...] = jnp.zeros_like(h_scr)

    row = lax.broadcasted_iota(jnp.int32, (tc, tc), 0)
    col = lax.broadcasted_iota(jnp.int32, (tc, tc), 1)
    causal = row >= col
    strict = row > col
    eye = (row == col).astype(F32)
    tri = causal.astype(BF16)
    gn = gn_ref[...]
    heads = range(H)
    sls = [slice(h * GDN_D, (h + 1) * GDN_D) for h in heads]
    hh = [h_scr[sls[h], :] for h in heads]

    for j in range(CHUNKS_PER_STEP):
        rs = slice(j * tc, (j + 1) * tc)
        bg = gdn_ref[rs, W_GDN:GDN_COLS]
        gc = _exact_rows(tri, bg)
        gt = gc.T
        kf, kb, qb, vf, beta, gcol, decay = [], [], [], [], [], [], []
        for h in heads:
            kf.append(gdn_ref[rs, GDN_QK + h * GDN_D:GDN_QK + (h + 1) * GDN_D])
            kb.append(kf[h].astype(BF16))
            qb.append(gdn_ref[rs, sls[h]].astype(BF16))
            vf.append(gdn_ref[rs, 2 * GDN_QK + h * GDN_D:2 * GDN_QK + (h + 1) * GDN_D])
            beta.append(bg[:, h:h + 1])
            gcol.append(gc[:, H + h:H + h + 1])
            grow = gt[H + h:H + h + 1, :]
            decay.append(jnp.where(causal, jnp.exp(jnp.where(causal, gcol[h] - grow, 0.0)), 0.0))
        qk_kk = [_dot_nt(jnp.concatenate([qb[h], kb[h]], axis=0), kb[h]) for h in heads]
        kq_h = [_dot(jnp.concatenate([kb[h], qb[h]], axis=0), hh[h].astype(BF16)) for h in heads]
        tinv = _tri_inverse([jnp.where(strict, beta[h] * qk_kk[h][tc:] * decay[h], 0.0) for h in heads],
                            eye, tc)
        eg = [jnp.exp(gcol[h]) for h in heads]
        rhs = [beta[h] * (vf[h] - eg[h] * kq_h[h][:tc]) for h in heads]
        ub = [u.astype(BF16) for u in _dot3_many(tinv, rhs)]
        ou = []
        for h in heads:
            qk = (qk_kk[h][:tc] * decay[h]).astype(BF16)
            g_last = gcol[h][tc - 1:tc, :]
            kd = (kf[h] * jnp.exp(g_last - gcol[h])).T.astype(BF16)
            ou.append(_dot(jnp.concatenate([qk, kd], axis=0), ub[h]))
        for h in heads:
            o = eg[h] * kq_h[h][tc:] + ou[h][:tc]
            hh[h] = jnp.exp(gcol[h][tc - 1:tc, :]) * hh[h] + ou[h][tc:]
            z = gdn_ref[rs, GDN_CONV_C + h * GDN_D:GDN_CONV_C + (h + 1) * GDN_D]
            o_ref[rs, sls[h]] = (_rms(o, gn) * _silu(z)).astype(o_ref.dtype)

    for h in heads:
        h_scr[sls[h], :] = hh[h]
        st_ref[0, h] = hh[h]


def _gdn_prompt(gdn2d, gn, bsz, t):
    step = CHUNKS_PER_STEP * CHUNK
    assert t % step == 0
    nc = t // step
    return pl.pallas_call(
        _gdn_chunk_kernel,
        grid=(bsz, nc),
        in_specs=[
            pl.BlockSpec((step, GDN_COLS), lambda b, c: (b * nc + c, 0)),
            pl.BlockSpec((1, GDN_D), lambda b, c: (0, 0)),
        ],
        out_specs=[
            pl.BlockSpec((step, GDN_V), lambda b, c: (b * nc + c, 0)),
            pl.BlockSpec((1, H, GDN_D, GDN_D), lambda b, c: (b, 0, 0, 0)),
        ],
        out_shape=[
            jax.ShapeDtypeStruct((bsz * t, GDN_V), BF16),
            jax.ShapeDtypeStruct((bsz, H, GDN_D, GDN_D), F32),
        ],
        scratch_shapes=[pltpu.VMEM((H * GDN_D, GDN_D), F32)],
        compiler_params=pltpu.CompilerParams(
            dimension_semantics=("parallel", "arbitrary"), vmem_limit_bytes=VMEM_LIMIT),
        name="gdn_prompt",
    )(gdn2d, gn)


SEQ_PER_STEP = 16


def _mix_sample_kernel(gla_ref, gdn_ref, h1_ref, h2_ref, h3_ref, sa_ref, sb_ref, cw_ref, gna_ref, gnb_ref,
                       oa_ref, ob_ref, sta_ref, stb_ref, cv1_ref, cv2_ref, cv3_ref, a_scr, b_scr, *, t):
    n = SEQ_PER_STEP
    rows = n * t
    tc = LANE
    spg = SUBLANE // t

    @pl.when(pl.program_id(0) == 0)
    def _():
        a_scr[...] = jnp.zeros_like(a_scr)
        b_scr[...] = jnp.zeros_like(b_scr)

    nh = GDN_CONV_W - 1
    st_refs = (h1_ref, h2_ref, h3_ref)
    cv_refs = (cv1_ref, cv2_ref, cv3_ref)
    x = gdn_ref[:, 0:GDN_CONV_C]
    pos = lax.broadcasted_iota(jnp.int32, x.shape, 0) % t
    r_ts = lax.broadcasted_iota(jnp.int32, (rows, n), 0)
    s_ts = lax.broadcasted_iota(jnp.int32, (rows, n), 1)
    spread = [(r_ts == s_ts * t + p).astype(BF16) for p in range(nh)]
    r_st = lax.broadcasted_iota(jnp.int32, (n, rows), 1)
    s_st = lax.broadcasted_iota(jnp.int32, (n, rows), 0)
    y = x * cw_ref[GDN_CONV_W - 1:GDN_CONV_W, :]
    for kk in range(1, GDN_CONV_W):
        hist = _exact_rows(jnp.concatenate(spread[:kk], axis=1),
                           jnp.concatenate([st_refs[nh + p - kk][...] for p in range(kk)], axis=0))
        m = jnp.where(pos >= kk, pltpu.roll(x, kk, 0), hist)
        y = y + m * cw_ref[GDN_CONV_W - 1 - kk:GDN_CONV_W - kk, :]
    for j in range(nh):
        cv_refs[j][...] = _exact_rows((r_st == s_st * t + t - nh + j).astype(BF16), x)
    b_scr[0:rows, 0:GDN_CONV_C] = _silu(y)
    b_scr[0:rows, GDN_CONV_C:GDN_CONV_C + LANE] = gdn_ref[:, W_GDN:GDN_COLS]
    a_scr[0:rows, :] = gla_ref[...]

    row = lax.broadcasted_iota(jnp.int32, (tc, tc), 0)
    col = lax.broadcasted_iota(jnp.int32, (tc, tc), 1)
    same = (row // t) == (col // t)
    causal = same & (row >= col)
    strict = same & (row > col)
    eye = (row == col).astype(F32)
    tri = causal.astype(BF16)
    last = (col == (row // t) * t + (t - 1)).astype(BF16)
    lane_seq = lax.broadcasted_iota(jnp.int32, (1, tc), 1) // t
    sub = lax.broadcasted_iota(jnp.int32, (SUBLANE, LANE), 0)

    def pick_rows(res, off):
        groups = []
        for g in range(rows // SUBLANE):
            sl = slice(off + g * SUBLANE, off + (g + 1) * SUBLANE)
            piece = res[g * spg + spg - 1][sl]
            for j in reversed(range(spg - 1)):
                piece = jnp.where(sub < (j + 1) * t, res[g * spg + j][sl], piece)
            groups.append(piece)
        return jnp.concatenate(groups, axis=0)

    q = a_scr[:, 0:GLA_QK] * (GLA_DK ** -0.5)
    k = a_scr[:, GLA_QK:2 * GLA_QK]
    vb = a_scr[:, 2 * GLA_QK:2 * GLA_QK + GLA_V].astype(BF16)
    b = _exact_rows(tri, a_scr[:, W_GLA:GLA_COLS])
    b_end = _exact_rows(last, b)
    qe = (q * jnp.exp(b)).astype(BF16)
    ke = (k * jnp.exp(-b)).astype(BF16)
    klt = (k * jnp.exp(b_end - b)).T.astype(BF16)
    bt = b.T
    lane_head = lax.broadcasted_iota(jnp.int32, (1, GLA_QK), 1) // GLA_DK
    qmask = [jnp.where(lane_head == h, qe, jnp.zeros_like(qe)) for h in range(H)]
    scores = [_dot_nt(qmask[h], ke) for h in range(H)]
    scores = [jnp.where(causal, s, 0.0).astype(BF16) for s in scores]
    o_intra = [_dot(scores[h], vb[:, h * GLA_DV:(h + 1) * GLA_DV]) for h in range(H)]
    lhs_q = jnp.concatenate(qmask, axis=0)
    hs = [jnp.concatenate([sa_ref[s, h] for h in range(H)], axis=0) for s in range(n)]
    res = [_dot(lhs_q, hs[s].astype(BF16)) for s in range(n)]
    upd = [_dot(jnp.where(lane_seq == s, klt, jnp.zeros_like(klt)), vb) for s in range(n)]
    gna = gna_ref[...]
    for h in range(H):
        sl = slice(h * GLA_DV, (h + 1) * GLA_DV)
        oh = o_intra[h][0:rows] + pick_rows(res, h * tc)
        gate = gla_ref[:, 2 * GLA_QK + GLA_V + h * GLA_DV:2 * GLA_QK + GLA_V + (h + 1) * GLA_DV]
        oa_ref[:, sl] = (_rms(oh, gna) * _silu(gate)).astype(oa_ref.dtype)
    for s in range(n):
        e_col = jnp.exp(bt[:, s * t + t - 1:s * t + t])
        diag = jnp.concatenate(
            [upd[s][h * GLA_DK:(h + 1) * GLA_DK, h * GLA_DV:(h + 1) * GLA_DV] for h in range(H)], axis=0)
        new = hs[s] * e_col + diag
        for h in range(H):
            sta_ref[s, h] = new[h * GLA_DK:(h + 1) * GLA_DK]

    bg = b_scr[:, GDN_CONV_C:GDN_CONV_C + LANE]
    gc = _exact_rows(tri, bg)
    gl = _exact_rows(last, gc)
    gt = gc.T
    heads = range(H)
    sls = [slice(h * GDN_D, (h + 1) * GDN_D) for h in heads]
    kf, kb, qb, vf, beta, gcol, decay = [], [], [], [], [], [], []
    for h in heads:
        qh = b_scr[:, sls[h]]
        kh = b_scr[:, GDN_QK + h * GDN_D:GDN_QK + (h + 1) * GDN_D]
        vf.append(b_scr[:, 2 * GDN_QK + h * GDN_D:2 * GDN_QK + (h + 1) * GDN_D])
        qh = qh * lax.rsqrt(jnp.sum(qh * qh, axis=-1, keepdims=True) + EPS) * (GDN_D ** -0.5)
        kh = kh * lax.rsqrt(jnp.sum(kh * kh, axis=-1, keepdims=True) + EPS)
        kf.append(kh)
        kb.append(kh.astype(BF16))
        qb.append(qh.astype(BF16))
        beta.append(bg[:, h:h + 1])
        gcol.append(gc[:, H + h:H + h + 1])
        grow = gt[H + h:H + h + 1, :]
        decay.append(jnp.where(causal, jnp.exp(jnp.where(causal, gcol[h] - grow, 0.0)), 0.0))
    qk_kk = [_dot_nt(jnp.concatenate([qb[h], kb[h]], axis=0), kb[h]) for h in heads]
    hsb = [[sb_ref[s, h] for h in heads] for s in range(n)]
    kq = [[_dot(jnp.concatenate([kb[h], qb[h]], axis=0), hsb[s][h].astype(BF16)) for s in range(n)]
          for h in heads]
    tinv = _tri_inverse([jnp.where(strict, beta[h] * qk_kk[h][tc:] * decay[h], 0.0) for h in heads], eye, t)
    pad_rows = jnp.zeros((tc - rows, GDN_D), F32)
    eg = [jnp.exp(gcol[h]) for h in heads]
    k_h = [jnp.concatenate([pick_rows(kq[h], 0), pad_rows], axis=0) for h in heads]
    rhs = [beta[h] * (vf[h] - eg[h] * k_h[h]) for h in heads]
    ub = [u.astype(BF16) for u in _dot3_many(tinv, rhs)]
    qku = [_dot((qk_kk[h][:tc] * decay[h]).astype(BF16), ub[h]) for h in heads]
    kdt = [(kf[h] * jnp.exp(gl[:, H + h:H + h + 1] - gcol[h])).T.astype(BF16) for h in heads]
    updb = [[_dot(jnp.where(lane_seq == s, kdt[h], jnp.zeros_like(kdt[h])), ub[h]) for s in range(n)]
            for h in heads]
    gnb = gnb_ref[...]
    for h in heads:
        o = eg[h][0:rows] * pick_rows(kq[h], tc) + qku[h][0:rows]
        z = gdn_ref[:, GDN_CONV_C + h * GDN_D:GDN_CONV_C + (h + 1) * GDN_D]
        ob_ref[:, sls[h]] = (_rms(o, gnb) * _silu(z)).astype(ob_ref.dtype)
        for s in range(n):
            e_last = jnp.exp(gl[s * t:s * t + 1, H + h:H + h + 1])
            stb_ref[s, h] = e_last * hsb[s][h] + updb[h][s]


def _mix_sample(gla2d, gdn2d, hists, sa, sb, cw, gna, gnb, t):
    n_rows = gla2d.shape[0]
    rows = SEQ_PER_STEP * t
    assert SUBLANE % t == 0 and rows <= LANE and n_rows % rows == 0
    bsz = n_rows // t
    r2 = lambda i: (i, 0)
    i4 = lambda i: (i, 0, 0, 0)
    c2 = lambda i: (0, 0)
    return pl.pallas_call(
        functools.partial(_mix_sample_kernel, t=t),
        grid=(n_rows // rows,),
        in_specs=[
            pl.BlockSpec((rows, GLA_COLS), r2),
            pl.BlockSpec((rows, GDN_COLS), r2),
            pl.BlockSpec((SEQ_PER_STEP, GDN_CONV_C), r2),
            pl.BlockSpec((SEQ_PER_STEP, GDN_CONV_C), r2),
            pl.BlockSpec((SEQ_PER_STEP, GDN_CONV_C), r2),
            pl.BlockSpec((SEQ_PER_STEP, H, GLA_DK, GLA_DV), i4),
            pl.BlockSpec((SEQ_PER_STEP, H, GDN_D, GDN_D), i4),
            pl.BlockSpec((GDN_CONV_W, GDN_CONV_C), c2),
            pl.BlockSpec((1, GLA_DV), c2),
            pl.BlockSpec((1, GDN_D), c2),
        ],
        out_specs=[
            pl.BlockSpec((rows, GLA_V), r2),
            pl.BlockSpec((rows, GDN_V), r2),
            pl.BlockSpec((SEQ_PER_STEP, H, GLA_DK, GLA_DV), i4),
            pl.BlockSpec((SEQ_PER_STEP, H, GDN_D, GDN_D), i4),
        ] + [pl.BlockSpec((SEQ_PER_STEP, GDN_CONV_C), r2)] * (GDN_CONV_W - 1),
        out_shape=[
            jax.ShapeDtypeStruct((n_rows, GLA_V), BF16),
            jax.ShapeDtypeStruct((n_rows, GDN_V), BF16),
            jax.ShapeDtypeStruct((bsz, H, GLA_DK, GLA_DV), F32),
            jax.ShapeDtypeStruct((bsz, H, GDN_D, GDN_D), F32),
        ] + [jax.ShapeDtypeStruct((bsz, GDN_CONV_C), F32)] * (GDN_CONV_W - 1),
        scratch_shapes=[pltpu.VMEM((LANE, GLA_COLS), F32),
                        pltpu.VMEM((LANE, GDN_CONV_C + LANE), F32)],
        compiler_params=pltpu.CompilerParams(
            dimension_semantics=("arbitrary",), vmem_limit_bytes=VMEM_LIMIT),
        name="mix_sample",
    )(gla2d, gdn2d, *hists, sa, sb, cw, gna, gnb)


def _ffn_kernel(*refs, seq_tiles, has_hist, seq_len):
    nh = FFN_CONV_W - 1
    if has_hist:
        (x_ref, oa_ref, ob_ref, wo_ref, g2_ref, wu_ref, cw_ref, wd_ref, gf_ref) = refs[:9]
        st_refs = refs[9:9 + nh]
        y_ref = refs[9 + nh]
        last_refs = refs[10 + nh:10 + 2 * nh]
    else:
        (x_ref, oa_ref, ob_ref, wo_ref, g2_ref, wu_ref, cw_ref, wd_ref, gf_ref,
         y_ref, tail_ref, u_scr) = refs
    tm = x_ref.shape[0]

    if not has_hist:
        @pl.when(pl.program_id(0) % seq_tiles == 0)
        def _():
            u_scr[0:SUBLANE, :] = jnp.zeros((SUBLANE, 2 * D_FF), F32)

    x1 = (x_ref[...] + _dot(oa_ref[...].astype(BF16), wo_ref[0:GLA_V, :])
          + _dot(ob_ref[...].astype(BF16), wo_ref[GLA_V:GLA_V + GDN_V, :]))
    hn = _rms(x1, g2_ref[...]).astype(BF16)

    if has_hist:
        nseq = tm // seq_len
        pos = lax.broadcasted_iota(jnp.int32, (tm, FF_CK), 0) % seq_len
        r_ts = lax.broadcasted_iota(jnp.int32, (tm, nseq), 0)
        s_ts = lax.broadcasted_iota(jnp.int32, (tm, nseq), 1)
        spread = [(r_ts == s_ts * seq_len + p).astype(BF16) for p in range(nh)]
        r_st = lax.broadcasted_iota(jnp.int32, (nseq, tm), 1)
        s_st = lax.broadcasted_iota(jnp.int32, (nseq, tm), 0)
        gather = [(r_st == s_st * seq_len + seq_len - nh + j).astype(BF16) for j in range(nh)]

    def conv(u, off):
        cols = slice(off, off + FF_CK)
        if has_hist:
            delayed = []
            for k in range(1, nh + 1):
                hist = _exact_rows(jnp.concatenate(spread[:k], axis=1),
                                   jnp.concatenate([st_refs[nh + p - k][:, cols] for p in range(k)], axis=0))
                delayed.append(jnp.where(pos >= k, pltpu.roll(u, k, 0), hist))
            m1, m2 = delayed
            for j in range(nh):
                last_refs[j][:, cols] = _exact_rows(gather[j], u)
        else:
            u = u_scr[SUBLANE:SUBLANE + tm, cols]
            m1 = u_scr[SUBLANE - 1:SUBLANE - 1 + tm, cols]
            m2 = u_scr[SUBLANE - 2:SUBLANE - 2 + tm, cols]
            tail = u_scr[tm:tm + SUBLANE, cols]
            tail_ref[0, :, cols] = tail
            u_scr[0:SUBLANE, cols] = tail
        return (m2 * cw_ref[0:1, cols] + m1 * cw_ref[1:2, cols] + u * cw_ref[2:3, cols])

    def up(c):
        ua = _dot(hn, wu_ref[:, c:c + FF_CK])
        ub = _dot(hn, wu_ref[:, D_FF + c:D_FF + c + FF_CK])
        if has_hist:
            return ua, ub
        u_scr[SUBLANE:SUBLANE + tm, c:c + FF_CK] = ua
        u_scr[SUBLANE:SUBLANE + tm, D_FF + c:D_FF + c + FF_CK] = ub
        return None, None

    acc = jnp.zeros((tm, D_MODEL), F32)
    ahead = 1 if has_hist else FF_LOOKAHEAD
    steps = list(range(0, D_FF, FF_CK))
    pending = [up(c) for c in steps[:ahead]]
    for i, c in enumerate(steps):
        if i + ahead < len(steps):
            pending.append(up(steps[i + ahead]))
        ua, ub = pending.pop(0)
        a = conv(ua, c)
        b = conv(ub, D_FF + c)
        act = (_silu(a) * b).astype(BF16)
        acc = acc + _dot(act, wd_ref[c:c + FF_CK, :])
    y_ref[...] = _rms(x1 + acc, gf_ref[...])


def _ffn(x2d, oa, ob, wo, g2, wu, cw, wd, gf, tm, seq_len, hist=None):
    n = x2d.shape[0]
    has_hist = hist is not None
    row = lambda i: (i, 0)
    const = lambda i: (0, 0)
    in_specs = [
        pl.BlockSpec((tm, D_MODEL), row),
        pl.BlockSpec((tm, GLA_V), row),
        pl.BlockSpec((tm, GDN_V), row),
        pl.BlockSpec((GLA_V + GDN_V, D_MODEL), const),
        pl.BlockSpec((1, D_MODEL), const),
        pl.BlockSpec((D_MODEL, 2 * D_FF), const),
        pl.BlockSpec((FFN_CONV_W, 2 * D_FF), const),
        pl.BlockSpec((D_FF, D_MODEL), const),
        pl.BlockSpec((1, D_MODEL), const),
    ]
    args = [x2d, oa, ob, wo, g2, wu, cw, wd, gf]
    out_specs = [pl.BlockSpec((tm, D_MODEL), row)]
    out_shape = [jax.ShapeDtypeStruct((n, D_MODEL), F32)]
    scratch = []
    if has_hist:
        assert tm % seq_len == 0
        seq_tiles = 1
        nseq = tm // seq_len
        assert len(hist) == FFN_CONV_W - 1 and seq_len >= FFN_CONV_W - 1
        in_specs += [pl.BlockSpec((nseq, 2 * D_FF), row)] * len(hist)
        args += list(hist)
        out_specs += [pl.BlockSpec((nseq, 2 * D_FF), row)] * len(hist)
        out_shape += [jax.ShapeDtypeStruct((n // seq_len, 2 * D_FF), F32)] * len(hist)
    else:
        assert seq_len % tm == 0
        seq_tiles = seq_len // tm
        out_specs.append(pl.BlockSpec((1, SUBLANE, 2 * D_FF), lambda i: (i, 0, 0)))
        out_shape.append(jax.ShapeDtypeStruct((n // tm, SUBLANE, 2 * D_FF), F32))
        scratch.append(pltpu.VMEM((SUBLANE + tm, 2 * D_FF), F32))
    return pl.pallas_call(
        functools.partial(_ffn_kernel, seq_tiles=seq_tiles, has_hist=has_hist, seq_len=seq_len),
        grid=(n // tm,),
        in_specs=in_specs,
        out_specs=out_specs,
        out_shape=out_shape,
        scratch_shapes=scratch,
        compiler_params=pltpu.CompilerParams(
            dimension_semantics=("arbitrary",), vmem_limit_bytes=VMEM_LIMIT),
        name="ffn_sample" if has_hist else "ffn_prompt",
    )(*args)


def _pad_cols(a, width):
    return jnp.pad(a, ((0, 0), (0, width - a.shape[1])))


def kernel(x_prompt, x_sample, state_gla, state_gdn, state_gdn_conv, state_ffn_conv, norm1_g, w_in, gla_w_a2, gla_b_a, gla_norm_g, gdn_conv_w, gdn_a_log, gdn_dt_bias, gdn_norm_g, w_out, norm2_g, w_up, ffn_conv_w, w_down, norm_f_g):
    assert w_in.shape[0] == 1, "single layer"
    bp, tp, _ = x_prompt.shape
    bs, ts, _ = x_sample.shape

    offs = [0]
    for s in IN_SIZES:
        offs.append(offs[-1] + s)
    wi = w_in[0]
    w1 = (wi[:, offs[0]:offs[4]].astype(BF16),
          wi[:, offs[5]:offs[7]].astype(BF16),
          jnp.concatenate([_pad_cols(wi[:, offs[4]:offs[5]], LANE),
                           _pad_cols(wi[:, offs[7]:offs[9]], LANE)], axis=1).astype(BF16))
    wa2 = jnp.pad(gla_w_a2[0], ((0, LANE - GLA_RANK), (0, 0))).astype(BF16)
    ba = gla_b_a[0][None, :]
    alog = jnp.pad(gdn_a_log[0], (H, LANE - 2 * H))[None, :]
    dtb = jnp.pad(gdn_dt_bias[0], (H, LANE - 2 * H))[None, :]
    g1 = norm1_g[0][None, :]
    g2 = norm2_g[0][None, :]
    gf = norm_f_g[None, :]
    gna = gla_norm_g[0][None, :]
    gnb = gdn_norm_g[0][None, :]
    cwb = gdn_conv_w[0]
    cwf = ffn_conv_w[0]
    wo = w_out[0].astype(BF16)
    wu = w_up[0].astype(BF16)
    wd = w_down[0].astype(BF16)

    xp = x_prompt.reshape(bp * tp, D_MODEL)
    tm_p = 512
    gla_p, gdn_p, qkv_tail = _inproj(xp, g1, w1, wa2, ba, alog, dtb, tm=tm_p, conv_w=cwb, seq_len=tp)
    oa_p, p_gla = _gla_prompt(gla_p, gna, bp, tp)
    ob_p, p_gdn = _gdn_prompt(gdn_p, gnb, bp, tp)
    y_p, ffn_tail = _ffn(xp, oa_p, ob_p, wo, g2, wu, cwf, wd, gf, tm=tm_p, seq_len=tp)
    y_prompt = y_p.reshape(bp, tp, D_MODEL)
    p_conv = qkv_tail.reshape(bp, tp // tm_p, SUBLANE, GDN_CONV_C)[:, -1, SUBLANE - (GDN_CONV_W - 1):, :]
    p_ffn = ffn_tail.reshape(bp, tp // tm_p, SUBLANE, 2 * D_FF)[:, -1, SUBLANE - (FFN_CONV_W - 1):, :]

    xs = x_sample.reshape(bs * ts, D_MODEL)
    gla_s, gdn_s = _inproj(xs, g1, w1, wa2, ba, alog, dtb, tm=256)
    oa_s, ob_s, s_gla, s_gdn, *conv_planes = _mix_sample(
        gla_s, gdn_s, [state_gdn_conv[0][:, j] for j in range(GDN_CONV_W - 1)],
        state_gla[0], state_gdn[0], cwb, gna, gnb, ts)
    y_s, *ffn_planes = _ffn(xs, oa_s, ob_s, wo, g2, wu, cwf, wd, gf, tm=128, seq_len=ts,
                            hist=[state_ffn_conv[0][:, j] for j in range(FFN_CONV_W - 1)])
    y_sample = y_s.reshape(bs, ts, D_MODEL)
    s_conv = jnp.stack(conv_planes, axis=1)
    s_ffn = jnp.stack(ffn_planes, axis=1)

    return (y_prompt, y_sample, p_gla[None], p_gdn[None], p_conv[None], p_ffn[None],
            s_gla[None], s_gdn[None], s_conv[None], s_ffn[None])
```

```python
import functools

import jax
import jax.numpy as jnp
from jax import lax
from jax.experimental import pallas as pl
from jax.experimental.pallas import tpu as pltpu

F32 = jnp.float32
BF16 = jnp.bfloat16

D_MODEL = 1024
H = 4
GLA_DK = 64
GLA_DV = 128
GLA_RANK = 16
GLA_TAU = 16.0
GDN_D = 128
GDN_CONV_W = 4
D_FF = 2816
FFN_CONV_W = 3
EPS = 1e-6

GLA_QK = H * GLA_DK
GLA_V = H * GLA_DV
GDN_QK = H * GDN_D
GDN_V = H * GDN_D
GDN_CONV_C = 2 * GDN_QK + GDN_V
IN_SIZES = (GLA_QK, GLA_QK, GLA_V, GLA_V, GLA_RANK, GDN_CONV_C, GDN_V, H, H)

LANE = 128
SUBLANE = 8
VMEM_LIMIT = 56 * 1024 * 1024

W_GLA = 2 * GLA_QK + 2 * GLA_V
W_GDN = GDN_CONV_C + GDN_V
GLA_COLS = W_GLA + GLA_QK
GDN_COLS = W_GDN + LANE

CHUNK = 128
CHUNKS_PER_STEP = 4
FF_CK = 256
FF_LOOKAHEAD = 11


def _dot(a, b):
    return jnp.dot(a, b, preferred_element_type=F32)


def _dot_nt(a, b):
    return lax.dot_general(a, b, (((1,), (1,)), ((), ())), preferred_element_type=F32)


def _split2(x):
    hi = x.astype(BF16)
    lo = (x - hi.astype(F32)).astype(BF16)
    return hi, lo


def _dot3(a, b):
    ah, al = _split2(a)
    bh, bl = _split2(b)
    return _dot(jnp.concatenate([ah, ah, al], axis=1), jnp.concatenate([bh, bl, bh], axis=0))


def _dot3_many(a_list, b_list):
    lhs = [jnp.concatenate([ah, ah, al], axis=1) for ah, al in map(_split2, a_list)]
    rhs = [jnp.concatenate([bh, bl, bh], axis=0) for bh, bl in map(_split2, b_list)]
    return [_dot(x, y) for x, y in zip(lhs, rhs)]


def _exact_rows(sel_bf16, x):
    h1 = x.astype(BF16)
    r1 = x - h1.astype(F32)
    h2 = r1.astype(BF16)
    h3 = (r1 - h2.astype(F32)).astype(BF16)
    return _dot(jnp.concatenate([sel_bf16] * 3, axis=1), jnp.concatenate([h1, h2, h3], axis=0))


def _sigmoid(x):
    return 1.0 / (1.0 + jnp.exp(-x))


def _silu(x):
    return x * _sigmoid(x)


def _softplus(x):
    return jnp.maximum(x, 0.0) + jnp.log(1.0 + jnp.exp(-jnp.abs(x)))


def _rms(x, g):
    ms = jnp.mean(x * x, axis=-1, keepdims=True)
    return x * lax.rsqrt(ms + EPS) * g


def _shift_rows(x, k, prev8):
    rolled = pltpu.roll(x, k, 0)
    prev_rolled = pltpu.roll(prev8, k, 0)
    row = lax.broadcasted_iota(jnp.int32, (SUBLANE, x.shape[1]), 0)
    first = jnp.where(row < k, prev_rolled, rolled[0:SUBLANE])
    return jnp.concatenate([first, rolled[SUBLANE:]], axis=0)


def _gdn_qkv_activation(y, off):
    a = _silu(y)
    outs = []
    for j in range(0, a.shape[1], GDN_D):
        blk = a[:, j:j + GDN_D]
        if off + j < 2 * GDN_QK:
            blk = blk * lax.rsqrt(jnp.sum(blk * blk, axis=-1, keepdims=True) + EPS)
            if off + j < GDN_QK:
                blk = blk * (GDN_D ** -0.5)
        outs.append(blk)
    return jnp.concatenate(outs, axis=1)


def _inproj_kernel(*refs, seq_tiles):
    if seq_tiles is None:
        (x_ref, g1_ref, wa_ref, wb_ref, ws_ref, wa2_ref, ba_ref, alog_ref, dtb_ref,
         gla_ref, gdn_ref) = refs
    else:
        (x_ref, g1_ref, wa_ref, wb_ref, ws_ref, wa2_ref, ba_ref, alog_ref, dtb_ref, cw_ref,
         gla_ref, gdn_ref, tail_ref, xs_scr) = refs
    tm = x_ref.shape[0]
    hn = _rms(x_ref[...], g1_ref[...]).astype(BF16)
    small = _dot(hn, ws_ref[...])
    if seq_tiles is None:
        for c in range(0, W_GLA, 512):
            gla_ref[:, c:c + 512] = _dot(hn, wa_ref[:, c:c + 512])
        for c in range(0, W_GDN, 512):
            gdn_ref[:, c:c + 512] = _dot(hn, wb_ref[:, c:c + 512])
    else:
        @pl.when(pl.program_id(0) % seq_tiles == 0)
        def _():
            xs_scr[0:SUBLANE, :] = jnp.zeros((SUBLANE, GDN_CONV_C), F32)

        for c in range(0, GDN_CONV_C, 512):
            xs_scr[SUBLANE:SUBLANE + tm, c:c + 512] = _dot(hn, wb_ref[:, c:c + 512])
        gdn_ref[:, GDN_CONV_C:W_GDN] = _dot(hn, wb_ref[:, GDN_CONV_C:W_GDN])
        for c in range(0, GDN_CONV_C, 512):
            gla_ref[:, c:c + 512] = _dot(hn, wa_ref[:, c:c + 512])
            cols = slice(c, c + 512)
            y = xs_scr[SUBLANE:SUBLANE + tm, cols] * cw_ref[GDN_CONV_W - 1:GDN_CONV_W, cols]
            for kk in range(1, GDN_CONV_W):
                y = y + (xs_scr[SUBLANE - kk:SUBLANE - kk + tm, cols]
                         * cw_ref[GDN_CONV_W - 1 - kk:GDN_CONV_W - kk, cols])
            gdn_ref[:, cols] = _gdn_qkv_activation(y, c)
            tail = xs_scr[tm:tm + SUBLANE, cols]
            tail_ref[0, :, cols] = tail
            xs_scr[0:SUBLANE, cols] = tail
    xa = _dot(small[:, 0:LANE].astype(BF16), wa2_ref[...]) + ba_ref[...]
    gla_ref[:, W_GLA:GLA_COLS] = -_softplus(-xa) * (1.0 / GLA_TAU)
    bd = small[:, LANE:2 * LANE]
    lane = lax.broadcasted_iota(jnp.int32, bd.shape, 1)
    beta = _sigmoid(bd)
    g = -jnp.exp(alog_ref[...]) * _softplus(bd + dtb_ref[...])
    gdn_ref[:, W_GDN:GDN_COLS] = jnp.where(lane < H, beta, jnp.where(lane < 2 * H, g, 0.0))


def _inproj(x2d, g1, w_in_parts, wa2, ba, alog, dtb, tm, conv_w=None, seq_len=None):
    n = x2d.shape[0]
    const = lambda i: (0, 0)
    fused_conv = conv_w is not None
    in_specs = [
        pl.BlockSpec((tm, D_MODEL), lambda i: (i, 0)),
        pl.BlockSpec((1, D_MODEL), const),
        pl.BlockSpec((D_MODEL, W_GLA), const),
        pl.BlockSpec((D_MODEL, W_GDN), const),
        pl.BlockSpec((D_MODEL, 2 * LANE), const),
        pl.BlockSpec((LANE, GLA_QK), const),
        pl.BlockSpec((1, GLA_QK), const),
        pl.BlockSpec((1, LANE), const),
        pl.BlockSpec((1, LANE), const),
    ]
    args = [x2d, g1, *w_in_parts, wa2, ba, alog, dtb]
    out_specs = [
        pl.BlockSpec((tm, GLA_COLS), lambda i: (i, 0)),
        pl.BlockSpec((tm, GDN_COLS), lambda i: (i, 0)),
    ]
    out_shape = [
        jax.ShapeDtypeStruct((n, GLA_COLS), F32),
        jax.ShapeDtypeStruct((n, GDN_COLS), F32),
    ]
    scratch = []
    seq_tiles = None
    if fused_conv:
        assert seq_len % tm == 0
        seq_tiles = seq_len // tm
        in_specs.append(pl.BlockSpec((GDN_CONV_W, GDN_CONV_C), const))
        args.append(conv_w)
        out_specs.append(pl.BlockSpec((1, SUBLANE, GDN_CONV_C), lambda i: (i, 0, 0)))
        out_shape.append(jax.ShapeDtypeStruct((n // tm, SUBLANE, GDN_CONV_C), F32))
        scratch.append(pltpu.VMEM((SUBLANE + tm, GDN_CONV_C), F32))
    return pl.pallas_call(
        functools.partial(_inproj_kernel, seq_tiles=seq_tiles),
        grid=(n // tm,),
        in_specs=in_specs,
        out_specs=out_specs,
        out_shape=out_shape,
        scratch_shapes=scratch,
        compiler_params=pltpu.CompilerParams(
            dimension_semantics=("arbitrary" if fused_conv else "parallel",),
            vmem_limit_bytes=VMEM_LIMIT),
        name="inproj_conv" if fused_conv else "inproj",
    )(*args)


def _gla_chunk_kernel(gla_ref, gn_ref, o_ref, st_ref, h_scr):
    tc = CHUNK

    @pl.when(pl.program_id(1) == 0)
    def _():
        h_scr[...] = jnp.zeros_like(h_scr)

    row = lax.broadcasted_iota(jnp.int32, (tc, tc), 0)
    col = lax.broadcasted_iota(jnp.int32, (tc, tc), 1)
    causal = row >= col
    tri = causal.astype(BF16)
    lane_head = lax.broadcasted_iota(jnp.int32, (1, GLA_QK), 1) // GLA_DK
    rblk = lax.broadcasted_iota(jnp.int32, (GLA_QK, GLA_V), 0) // GLA_DK
    cblk = lax.broadcasted_iota(jnp.int32, (GLA_QK, GLA_V), 1) // GLA_DV
    gn = gn_ref[...]
    hbd = h_scr[...]

    chunks = range(CHUNKS_PER_STEP)
    rows = [slice(j * tc, (j + 1) * tc) for j in chunks]
    b = [_exact_rows(tri, gla_ref[rs, W_GLA:GLA_COLS]) for rs in rows]
    qi, qm, km, vb, klt, e_last = [], [], [], [], [], []
    for j in chunks:
        q = gla_ref[rows[j], 0:GLA_QK] * (GLA_DK ** -0.5)
        k = gla_ref[rows[j], GLA_QK:2 * GLA_QK]
        vb.append(gla_ref[rows[j], 2 * GLA_QK:2 * GLA_QK + GLA_V].astype(BF16))
        b_mid = b[j][tc // 2 - 1:tc // 2, :]
        qm.append((q * jnp.exp(b[j] - b_mid)).astype(BF16))
        km.append((k * jnp.exp(b_mid - b[j])).astype(BF16))
        qi.append((q * jnp.exp(b[j])).astype(BF16))
        bt = b[j].T
        b_last = bt[:, tc - 1:tc]
        klt.append((k.T * jnp.exp(b_last - bt)).astype(BF16))
        e_last.append(jnp.exp(b_last))
    scores = {(j, h): _dot_nt(jnp.where(lane_head == h, qm[j], jnp.zeros_like(qm[j])), km[j])
              for j in chunks for h in range(H)}
    scores = {p: jnp.where(causal, s, 0.0).astype(BF16) for p, s in scores.items()}
    o_intra = {(j, h): _dot(scores[j, h], vb[j][:, h * GLA_DV:(h + 1) * GLA_DV])
               for j in chunks for h in range(H)}
    upd = [jnp.where(rblk == cblk, _dot(klt[j], vb[j]), 0.0) for j in chunks]

    for j in chunks:
        rs = rows[j]
        o_inter = _dot(qi[j], hbd.astype(BF16))
        hbd = hbd * e_last[j] + upd[j]
        for h in range(H):
            sl = slice(h * GLA_DV, (h + 1) * GLA_DV)
            oh = o_intra[j, h] + o_inter[:, sl]
            gate = gla_ref[rs, 2 * GLA_QK + GLA_V + h * GLA_DV:2 * GLA_QK + GLA_V + (h + 1) * GLA_DV]
            o_ref[rs, sl] = (_rms(oh, gn) * _silu(gate)).astype(o_ref.dtype)

    h_scr[...] = hbd
    for h in range(H):
        st_ref[0, h] = hbd[h * GLA_DK:(h + 1) * GLA_DK, h * GLA_DV:(h + 1) * GLA_DV]


def _gla_prompt(gla2d, gn, bsz, t):
    step = CHUNKS_PER_STEP * CHUNK
    assert t % step == 0
    nc = t // step
    return pl.pallas_call(
        _gla_chunk_kernel,
        grid=(bsz, nc),
        in_specs=[
            pl.BlockSpec((step, GLA_COLS), lambda b, c: (b * nc + c, 0)),
            pl.BlockSpec((1, GLA_DV), lambda b, c: (0, 0)),
        ],
        out_specs=[
            pl.BlockSpec((step, GLA_V), lambda b, c: (b * nc + c, 0)),
            pl.BlockSpec((1, H, GLA_DK, GLA_DV), lambda b, c: (b, 0, 0, 0)),
        ],
        out_shape=[
            jax.ShapeDtypeStruct((bsz * t, GLA_V), BF16),
            jax.ShapeDtypeStruct((bsz, H, GLA_DK, GLA_DV), F32),
        ],
        scratch_shapes=[pltpu.VMEM((GLA_QK, GLA_V), F32)],
        compiler_params=pltpu.CompilerParams(
            dimension_semantics=("parallel", "arbitrary"), vmem_limit_bytes=VMEM_LIMIT),
        name="gla_prompt",
    )(gla2d, gn)


def _tri_inverse(l_strict, eye, order):
    n = eye.shape[0]
    refine = order > 2 * SUBLANE

    def dots(a_list, b_list):
        if not refine:
            return _dot3_many(a_list, b_list)
        a_list = [a.astype(BF16) for a in a_list]
        b_list = [b.astype(BF16) for b in b_list]
        return [_dot(a, b) for a, b in zip(a_list, b_list)]

    m = [-l for l in l_strict]
    p = [eye + x for x in m]
    m = dots(m, m)
    power = 2
    while 2 * power < order:
        pm = dots([jnp.concatenate([a, b], axis=0) for a, b in zip(p, m)], m)
        p = [a + x[:n] for a, x in zip(p, pm)]
        m = [x[n:] for x in pm]
        power *= 2
    t = [a + x for a, x in zip(p, dots(p, m))]
    if refine:
        r = [eye - a - x for a, x in zip(t, _dot3_many(l_strict, t))]
        t = [a + x for a, x in zip(t, dots(t, r))]
    return t


def _gdn_chunk_kernel(gdn_ref, gn_ref, o_ref, st_ref, h_scr):
    tc = CHUNK

    @pl.when(pl.program_id(1) == 0)
    def _():
        h_scr[...] = jnp.zeros_like(h_scr)

    row = lax.broadcasted_iota(jnp.int32, (tc, tc), 0)
    col = lax.broadcasted_iota(jnp.int32, (tc, tc), 1)
    causal = row >= col
    strict = row > col
    eye = (row == col).astype(F32)
    tri = causal.astype(BF16)
    gn = gn_ref[...]
    heads = range(H)
    sls = [slice(h * GDN_D, (h + 1) * GDN_D) for h in heads]
    hh = [h_scr[sls[h], :] for h in heads]

    probs = [(j, h) for j in range(CHUNKS_PER_STEP) for h in heads]
    rows = [slice(j * tc, (j + 1) * tc) for j in range(CHUNKS_PER_STEP)]
    bg = [gdn_ref[rs, W_GDN:GDN_COLS] for rs in rows]
    gc = [_exact_rows(tri, x) for x in bg]
    gt = [x.T for x in gc]
    kf, kb, qb, beta, gcol, decay = {}, {}, {}, {}, {}, {}
    for j, h in probs:
        kf[j, h] = gdn_ref[rows[j], GDN_QK + h * GDN_D:GDN_QK + (h + 1) * GDN_D]
        kb[j, h] = kf[j, h].astype(BF16)
        qb[j, h] = gdn_ref[rows[j], sls[h]].astype(BF16)
        beta[j, h] = bg[j][:, h:h + 1]
        gcol[j, h] = gc[j][:, H + h:H + h + 1]
        grow = gt[j][H + h:H + h + 1, :]
        decay[j, h] = jnp.where(causal, jnp.exp(jnp.where(causal, gcol[j, h] - grow, 0.0)), 0.0)
    qk_kk = {p: _dot_nt(jnp.concatenate([qb[p], kb[p]], axis=0), kb[p]) for p in probs}
    tinv = dict(zip(probs, _tri_inverse(
        [jnp.where(strict, beta[p] * qk_kk[p][tc:] * decay[p], 0.0) for p in probs], eye, tc)))

    for j in range(CHUNKS_PER_STEP):
        rs = rows[j]
        kq_h = [_dot(jnp.concatenate([kb[j, h], qb[j, h]], axis=0), hh[h].astype(BF16))
                for h in heads]
        eg = [jnp.exp(gcol[j, h]) for h in heads]
        rhs = [beta[j, h] * (gdn_ref[rs, 2 * GDN_QK + h * GDN_D:2 * GDN_QK + (h + 1) * GDN_D]
                             - eg[h] * kq_h[h][:tc]) for h in heads]
        ub = [u.astype(BF16) for u in _dot3_many([tinv[j, h] for h in heads], rhs)]
        ou = []
        for h in heads:
            qk = (qk_kk[j, h][:tc] * decay[j, h]).astype(BF16)
            g_last = gcol[j, h][tc - 1:tc, :]
            kd = (kf[j, h] * jnp.exp(g_last - gcol[j, h])).T.astype(BF16)
            ou.append(_dot(jnp.concatenate([qk, kd], axis=0), ub[h]))
        for h in heads:
            o = eg[h] * kq_h[h][tc:] + ou[h][:tc]
            hh[h] = jnp.exp(gcol[j, h][tc - 1:tc, :]) * hh[h] + ou[h][tc:]
            z = gdn_ref[rs, GDN_CONV_C + h * GDN_D:GDN_CONV_C + (h + 1) * GDN_D]
            o_ref[rs, sls[h]] = (_rms(o, gn) * _silu(z)).astype(o_ref.dtype)

    for h in heads:
        h_scr[sls[h], :] = hh[h]
        st_ref[0, h] = hh[h]


def _gdn_prompt(gdn2d, gn, bsz, t):
    step = CHUNKS_PER_STEP * CHUNK
    assert t % step == 0
    nc = t // step
    return pl.pallas_call(
        _gdn_chunk_kernel,
        grid=(bsz, nc),
        in_specs=[
            pl.BlockSpec((step, GDN_COLS), lambda b, c: (b * nc + c, 0)),
            pl.BlockSpec((1, GDN_D), lambda b, c: (0, 0)),
        ],
        out_specs=[
            pl.BlockSpec((step, GDN_V), lambda b, c: (b * nc + c, 0)),
            pl.BlockSpec((1, H, GDN_D, GDN_D), lambda b, c: (b, 0, 0, 0)),
        ],
        out_shape=[
            jax.ShapeDtypeStruct((bsz * t, GDN_V), BF16),
            jax.ShapeDtypeStruct((bsz, H, GDN_D, GDN_D), F32),
        ],
        scratch_shapes=[pltpu.VMEM((H * GDN_D, GDN_D), F32)],
        compiler_params=pltpu.CompilerParams(
            dimension_semantics=("parallel", "arbitrary"), vmem_limit_bytes=VMEM_LIMIT),
        name="gdn_prompt",
    )(gdn2d, gn)


SEQ_PER_STEP = 16


def _mix_sample_kernel(gla_ref, gdn_ref, h1_ref, h2_ref, h3_ref, sa_ref, sb_ref, cw_ref, gna_ref, gnb_ref,
                       oa_ref, ob_ref, sta_ref, stb_ref, cv1_ref, cv2_ref, cv3_ref, a_scr, b_scr, *, t):
    n = SEQ_PER_STEP
    rows = n * t
    tc = LANE
    spg = SUBLANE // t

    @pl.when(pl.program_id(0) == 0)
    def _():
        a_scr[...] = jnp.zeros_like(a_scr)
        b_scr[...] = jnp.zeros_like(b_scr)

    nh = GDN_CONV_W - 1
    st_refs = (h1_ref, h2_ref, h3_ref)
    cv_refs = (cv1_ref, cv2_ref, cv3_ref)
    x = gdn_ref[:, 0:GDN_CONV_C]
    pos = lax.broadcasted_iota(jnp.int32, x.shape, 0) % t
    r_ts = lax.broadcasted_iota(jnp.int32, (rows, n), 0)
    s_ts = lax.broadcasted_iota(jnp.int32, (rows, n), 1)
    spread = [(r_ts == s_ts * t + p).astype(BF16) for p in range(nh)]
    r_st = lax.broadcasted_iota(jnp.int32, (n, rows), 1)
    s_st = lax.broadcasted_iota(jnp.int32, (n, rows), 0)
    y = x * cw_ref[GDN_CONV_W - 1:GDN_CONV_W, :]
    for kk in range(1, GDN_CONV_W):
        hist = _exact_rows(jnp.concatenate(spread[:kk], axis=1),
                           jnp.concatenate([st_refs[nh + p - kk][...] for p in range(kk)], axis=0))
        m = jnp.where(pos >= kk, pltpu.roll(x, kk, 0), hist)
        y = y + m * cw_ref[GDN_CONV_W - 1 - kk:GDN_CONV_W - kk, :]
    for j in range(nh):
        cv_refs[j][...] = _exact_rows((r_st == s_st * t + t - nh + j).astype(BF16), x)
    b_scr[0:rows, 0:GDN_CONV_C] = _silu(y)
    b_scr[0:rows, GDN_CONV_C:GDN_CONV_C + LANE] = gdn_ref[:, W_GDN:GDN_COLS]
    a_scr[0:rows, :] = gla_ref[...]

    row = lax.broadcasted_iota(jnp.int32, (tc, tc), 0)
    col = lax.broadcasted_iota(jnp.int32, (tc, tc), 1)
    same = (row // t) == (col // t)
    causal = same & (row >= col)
    strict = same & (row > col)
    eye = (row == col).astype(F32)
    tri = causal.astype(BF16)
    last = (col == (row // t) * t + (t - 1)).astype(BF16)
    lane_seq = lax.broadcasted_iota(jnp.int32, (1, tc), 1) // t
    sub = lax.broadcasted_iota(jnp.int32, (SUBLANE, LANE), 0)

    def pick_rows(res, off):
        groups = []
        for g in range(rows // SUBLANE):
            sl = slice(off + g * SUBLANE, off + (g + 1) * SUBLANE)
            piece = res[g * spg + spg - 1][sl]
            for j in reversed(range(spg - 1)):
                piece = jnp.where(sub < (j + 1) * t, res[g * spg + j][sl], piece)
            groups.append(piece)
        return jnp.concatenate(groups, axis=0)

    q = a_scr[:, 0:GLA_QK] * (GLA_DK ** -0.5)
    k = a_scr[:, GLA_QK:2 * GLA_QK]
    vb = a_scr[:, 2 * GLA_QK:2 * GLA_QK + GLA_V].astype(BF16)
    b = _exact_rows(tri, a_scr[:, W_GLA:GLA_COLS])
    b_end = _exact_rows(last, b)
    qe = (q * jnp.exp(b)).astype(BF16)
    ke = (k * jnp.exp(-b)).astype(BF16)
    klt = (k * jnp.exp(b_end - b)).T.astype(BF16)
    bt = b.T
    lane_head = lax.broadcasted_iota(jnp.int32, (1, GLA_QK), 1) // GLA_DK
    qmask = [jnp.where(lane_head == h, qe, jnp.zeros_like(qe)) for h in range(H)]
    scores = [_dot_nt(qmask[h], ke) for h in range(H)]
    scores = [jnp.where(causal, s, 0.0).astype(BF16) for s in scores]
    o_intra = [_dot(scores[h], vb[:, h * GLA_DV:(h + 1) * GLA_DV]) for h in range(H)]
    lhs_q = jnp.concatenate(qmask, axis=0)
    hs = [jnp.concatenate([sa_ref[s, h] for h in range(H)], axis=0) for s in range(n)]
    res = [_dot(lhs_q, hs[s].astype(BF16)) for s in range(n)]
    upd = [_dot(jnp.where(lane_seq == s, klt, jnp.zeros_like(klt)), vb) for s in range(n)]
    gna = gna_ref[...]
    for h in range(H):
        sl = slice(h * GLA_DV, (h + 1) * GLA_DV)
        oh = o_intra[h][0:rows] + pick_rows(res, h * tc)
        gate = gla_ref[:, 2 * GLA_QK + GLA_V + h * GLA_DV:2 * GLA_QK + GLA_V + (h + 1) * GLA_DV]
        oa_ref[:, sl] = (_rms(oh, gna) * _silu(gate)).astype(oa_ref.dtype)
    for s in range(n):
        e_col = jnp.exp(bt[:, s * t + t - 1:s * t + t])
        diag = jnp.concatenate(
            [upd[s][h * GLA_DK:(h + 1) * GLA_DK, h * GLA_DV:(h + 1) * GLA_DV] for h in range(H)], axis=0)
        new = hs[s] * e_col + diag
        for h in range(H):
            sta_ref[s, h] = new[h * GLA_DK:(h + 1) * GLA_DK]

    bg = b_scr[:, GDN_CONV_C:GDN_CONV_C + LANE]
    gc = _exact_rows(tri, bg)
    gl = _exact_rows(last, gc)
    gt = gc.T
    heads = range(H)
    sls = [slice(h * GDN_D, (h + 1) * GDN_D) for h in heads]
    kf, kb, qb, vf, beta, gcol, decay = [], [], [], [], [], [], []
    for h in heads:
        qh = b_scr[:, sls[h]]
        kh = b_scr[:, GDN_QK + h * GDN_D:GDN_QK + (h + 1) * GDN_D]
        vf.append(b_scr[:, 2 * GDN_QK + h * GDN_D:2 * GDN_QK + (h + 1) * GDN_D])
        qh = qh * lax.rsqrt(jnp.sum(qh * qh, axis=-1, keepdims=True) + EPS) * (GDN_D ** -0.5)
        kh = kh * lax.rsqrt(jnp.sum(kh * kh, axis=-1, keepdims=True) + EPS)
        kf.append(kh)
        kb.append(kh.astype(BF16))
        qb.append(qh.astype(BF16))
        beta.append(bg[:, h:h + 1])
        gcol.append(gc[:, H + h:H + h + 1])
        grow = gt[H + h:H + h + 1, :]
        decay.append(jnp.where(causal, jnp.exp(jnp.where(causal, gcol[h] - grow, 0.0)), 0.0))
    qk_kk = [_dot_nt(jnp.concatenate([qb[h], kb[h]], axis=0), kb[h]) for h in heads]
    hsb = [[sb_ref[s, h] for h in heads] for s in range(n)]
    kq = [[_dot(jnp.concatenate([kb[h], qb[h]], axis=0), hsb[s][h].astype(BF16)) for s in range(n)]
          for h in heads]
    tinv = _tri_inverse([jnp.where(strict, beta[h] * qk_kk[h][tc:] * decay[h], 0.0) for h in heads], eye, t)
    pad_rows = jnp.zeros((tc - rows, GDN_D), F32)
    eg = [jnp.exp(gcol[h]) for h in heads]
    k_h = [jnp.concatenate([pick_rows(kq[h], 0), pad_rows], axis=0) for h in heads]
    rhs = [beta[h] * (vf[h] - eg[h] * k_h[h]) for h in heads]
    ub = [u.astype(BF16) for u in _dot3_many(tinv, rhs)]
    qku = [_dot((qk_kk[h][:tc] * decay[h]).astype(BF16), ub[h]) for h in heads]
    kdt = [(kf[h] * jnp.exp(gl[:, H + h:H + h + 1] - gcol[h])).T.astype(BF16) for h in heads]
    updb = [[_dot(jnp.where(lane_seq == s, kdt[h], jnp.zeros_like(kdt[h])), ub[h]) for s in range(n)]
            for h in heads]
    gnb = gnb_ref[...]
    for h in heads:
        o = eg[h][0:rows] * pick_rows(kq[h], tc) + qku[h][0:rows]
        z = gdn_ref[:, GDN_CONV_C + h * GDN_D:GDN_CONV_C + (h + 1) * GDN_D]
        ob_ref[:, sls[h]] = (_rms(o, gnb) * _silu(z)).astype(ob_ref.dtype)
        for s in range(n):
            e_last = jnp.exp(gl[s * t:s * t + 1, H + h:H + h + 1])
            stb_ref[s, h] = e_last * hsb[s][h] + updb[h][s]


def _mix_sample(gla2d, gdn2d, hists, sa, sb, cw, gna, gnb, t):
    n_rows = gla2d.shape[0]
    rows = SEQ_PER_STEP * t
    assert SUBLANE % t == 0 and rows <= LANE and n_rows % rows == 0
    bsz = n_rows // t
    r2 = lambda i: (i, 0)
    i4 = lambda i: (i, 0, 0, 0)
    c2 = lambda i: (0, 0)
    return pl.pallas_call(
        functools.partial(_mix_sample_kernel, t=t),
        grid=(n_rows // rows,),
        in_specs=[
            pl.BlockSpec((rows, GLA_COLS), r2),
            pl.BlockSpec((rows, GDN_COLS), r2),
            pl.BlockSpec((SEQ_PER_STEP, GDN_CONV_C), r2),
            pl.BlockSpec((SEQ_PER_STEP, GDN_CONV_C), r2),
            pl.BlockSpec((SEQ_PER_STEP, GDN_CONV_C), r2),
            pl.BlockSpec((SEQ_PER_STEP, H, GLA_DK, GLA_DV), i4),
            pl.BlockSpec((SEQ_PER_STEP, H, GDN_D, GDN_D), i4),
            pl.BlockSpec((GDN_CONV_W, GDN_CONV_C), c2),
            pl.BlockSpec((1, GLA_DV), c2),
            pl.BlockSpec((1, GDN_D), c2),
        ],
        out_specs=[
            pl.BlockSpec((rows, GLA_V), r2),
            pl.BlockSpec((rows, GDN_V), r2),
            pl.BlockSpec((SEQ_PER_STEP, H, GLA_DK, GLA_DV), i4),
            pl.BlockSpec((SEQ_PER_STEP, H, GDN_D, GDN_D), i4),
        ] + [pl.BlockSpec((SEQ_PER_STEP, GDN_CONV_C), r2)] * (GDN_CONV_W - 1),
        out_shape=[
            jax.ShapeDtypeStruct((n_rows, GLA_V), BF16),
            jax.ShapeDtypeStruct((n_rows, GDN_V), BF16),
            jax.ShapeDtypeStruct((bsz, H, GLA_DK, GLA_DV), F32),
            jax.ShapeDtypeStruct((bsz, H, GDN_D, GDN_D), F32),
        ] + [jax.ShapeDtypeStruct((bsz, GDN_CONV_C), F32)] * (GDN_CONV_W - 1),
        scratch_shapes=[pltpu.VMEM((LANE, GLA_COLS), F32),
                        pltpu.VMEM((LANE, GDN_CONV_C + LANE), F32)],
        compiler_params=pltpu.CompilerParams(
            dimension_semantics=("arbitrary",), vmem_limit_bytes=VMEM_LIMIT),
        name="mix_sample",
    )(gla2d, gdn2d, *hists, sa, sb, cw, gna, gnb)


def _ffn_kernel(*refs, seq_tiles, has_hist, seq_len):
    nh = FFN_CONV_W - 1
    if has_hist:
        (x_ref, oa_ref, ob_ref, wo_ref, g2_ref, wu_ref, cw_ref, wd_ref, gf_ref) = refs[:9]
        st_refs = refs[9:9 + nh]
        y_ref = refs[9 + nh]
        last_refs = refs[10 + nh:10 + 2 * nh]
    else:
        (x_ref, oa_ref, ob_ref, wo_ref, g2_ref, wu_ref, cw_ref, wd_ref, gf_ref,
         y_ref, tail_ref, u_scr) = refs
    tm = x_ref.shape[0]

    if not has_hist:
        @pl.when(pl.program_id(0) % seq_tiles == 0)
        def _():
            u_scr[0:SUBLANE, :] = jnp.zeros((SUBLANE, 2 * D_FF), F32)

    x1 = (x_ref[...] + _dot(oa_ref[...].astype(BF16), wo_ref[0:GLA_V, :])
          + _dot(ob_ref[...].astype(BF16), wo_ref[GLA_V:GLA_V + GDN_V, :]))
    hn = _rms(x1, g2_ref[...]).astype(BF16)

    if has_hist:
        nseq = tm // seq_len
        pos = lax.broadcasted_iota(jnp.int32, (tm, FF_CK), 0) % seq_len
        r_ts = lax.broadcasted_iota(jnp.int32, (tm, nseq), 0)
        s_ts = lax.broadcasted_iota(jnp.int32, (tm, nseq), 1)
        spread = [(r_ts == s_ts * seq_len + p).astype(BF16) for p in range(nh)]
        r_st = lax.broadcasted_iota(jnp.int32, (nseq, tm), 1)
        s_st = lax.broadcasted_iota(jnp.int32, (nseq, tm), 0)
        gather = [(r_st == s_st * seq_len + seq_len - nh + j).astype(BF16) for j in range(nh)]

    def conv(u, off):
        cols = slice(off, off + FF_CK)
        if has_hist:
            delayed = []
            for k in range(1, nh + 1):
                hist = _exact_rows(jnp.concatenate(spread[:k], axis=1),
                                   jnp.concatenate([st_refs[nh + p - k][:, cols] for p in range(k)], axis=0))
                delayed.append(jnp.where(pos >= k, pltpu.roll(u, k, 0), hist))
            m1, m2 = delayed
            for j in range(nh):
                last_refs[j][:, cols] = _exact_rows(gather[j], u)
        else:
            u = u_scr[SUBLANE:SUBLANE + tm, cols]
            m1 = u_scr[SUBLANE - 1:SUBLANE - 1 + tm, cols]
            m2 = u_scr[SUBLANE - 2:SUBLANE - 2 + tm, cols]
            tail = u_scr[tm:tm + SUBLANE, cols]
            tail_ref[0, :, cols] = tail
            u_scr[0:SUBLANE, cols] = tail
        return (m2 * cw_ref[0:1, cols] + m1 * cw_ref[1:2, cols] + u * cw_ref[2:3, cols])

    def up(c):
        ua = _dot(hn, wu_ref[:, c:c + FF_CK])
        ub = _dot(hn, wu_ref[:, D_FF + c:D_FF + c + FF_CK])
        if has_hist:
            return ua, ub
        u_scr[SUBLANE:SUBLANE + tm, c:c + FF_CK] = ua
        u_scr[SUBLANE:SUBLANE + tm, D_FF + c:D_FF + c + FF_CK] = ub
        return None, None

    acc = jnp.zeros((tm, D_MODEL), F32)
    ahead = 1 if has_hist else FF_LOOKAHEAD
    steps = list(range(0, D_FF, FF_CK))
    pending = [up(c) for c in steps[:ahead]]
    for i, c in enumerate(steps):
        if i + ahead < len(steps):
            pending.append(up(steps[i + ahead]))
        ua, ub = pending.pop(0)
        a = conv(ua, c)
        b = conv(ub, D_FF + c)
        act = (_silu(a) * b).astype(BF16)
        acc = acc + _dot(act, wd_ref[c:c + FF_CK, :])
    y_ref[...] = _rms(x1 + acc, gf_ref[...])


def _ffn(x2d, oa, ob, wo, g2, wu, cw, wd, gf, tm, seq_len, hist=None):
    n = x2d.shape[0]
    has_hist = hist is not None
    row = lambda i: (i, 0)
    const = lambda i: (0, 0)
    in_specs = [
        pl.BlockSpec((tm, D_MODEL), row),
        pl.BlockSpec((tm, GLA_V), row),
        pl.BlockSpec((tm, GDN_V), row),
        pl.BlockSpec((GLA_V + GDN_V, D_MODEL), const),
        pl.BlockSpec((1, D_MODEL), const),
        pl.BlockSpec((D_MODEL, 2 * D_FF), const),
        pl.BlockSpec((FFN_CONV_W, 2 * D_FF), const),
        pl.BlockSpec((D_FF, D_MODEL), const),
        pl.BlockSpec((1, D_MODEL), const),
    ]
    args = [x2d, oa, ob, wo, g2, wu, cw, wd, gf]
    out_specs = [pl.BlockSpec((tm, D_MODEL), row)]
    out_shape = [jax.ShapeDtypeStruct((n, D_MODEL), F32)]
    scratch = []
    if has_hist:
        assert tm % seq_len == 0
        seq_tiles = 1
        nseq = tm // seq_len
        assert len(hist) == FFN_CONV_W - 1 and seq_len >= FFN_CONV_W - 1
        in_specs += [pl.BlockSpec((nseq, 2 * D_FF), row)] * len(hist)
        args += list(hist)
        out_specs += [pl.BlockSpec((nseq, 2 * D_FF), row)] * len(hist)
        out_shape += [jax.ShapeDtypeStruct((n // seq_len, 2 * D_FF), F32)] * len(hist)
    else:
        assert seq_len % tm == 0
        seq_tiles = seq_len // tm
        out_specs.append(pl.BlockSpec((1, SUBLANE, 2 * D_FF), lambda i: (i, 0, 0)))
        out_shape.append(jax.ShapeDtypeStruct((n // tm, SUBLANE, 2 * D_FF), F32))
        scratch.append(pltpu.VMEM((SUBLANE + tm, 2 * D_FF), F32))
    return pl.pallas_call(
        functools.partial(_ffn_kernel, seq_tiles=seq_tiles, has_hist=has_hist, seq_len=seq_len),
        grid=(n // tm,),
        in_specs=in_specs,
        out_specs=out_specs,
        out_shape=out_shape,
        scratch_shapes=scratch,
        compiler_params=pltpu.CompilerParams(
            dimension_semantics=("arbitrary",), vmem_limit_bytes=VMEM_LIMIT),
        name="ffn_sample" if has_hist else "ffn_prompt",
    )(*args)


def _pad_cols(a, width):
    return jnp.pad(a, ((0, 0), (0, width - a.shape[1])))


def kernel(x_prompt, x_sample, state_gla, state_gdn, state_gdn_conv, state_ffn_conv, norm1_g, w_in, gla_w_a2, gla_b_a, gla_norm_g, gdn_conv_w, gdn_a_log, gdn_dt_bias, gdn_norm_g, w_out, norm2_g, w_up, ffn_conv_w, w_down, norm_f_g):
    assert w_in.shape[0] == 1, "single layer"
    bp, tp, _ = x_prompt.shape
    bs, ts, _ = x_sample.shape

    offs = [0]
    for s in IN_SIZES:
        offs.append(offs[-1] + s)
    wi = w_in[0]
    w1 = (wi[:, offs[0]:offs[4]].astype(BF16),
          wi[:, offs[5]:offs[7]].astype(BF16),
          jnp.concatenate([_pad_cols(wi[:, offs[4]:offs[5]], LANE),
                           _pad_cols(wi[:, offs[7]:offs[9]], LANE)], axis=1).astype(BF16))
    wa2 = jnp.pad(gla_w_a2[0], ((0, LANE - GLA_RANK), (0, 0))).astype(BF16)
    ba = gla_b_a[0][None, :]
    alog = jnp.pad(gdn_a_log[0], (H, LANE - 2 * H))[None, :]
    dtb = jnp.pad(gdn_dt_bias[0], (H, LANE - 2 * H))[None, :]
    g1 = norm1_g[0][None, :]
    g2 = norm2_g[0][None, :]
    gf = norm_f_g[None, :]
    gna = gla_norm_g[0][None, :]
    gnb = gdn_norm_g[0][None, :]
    cwb = gdn_conv_w[0]
    cwf = ffn_conv_w[0]
    wo = w_out[0].astype(BF16)
    wu = w_up[0].astype(BF16)
    wd = w_down[0].astype(BF16)

    xp = x_prompt.reshape(bp * tp, D_MODEL)
    tm_p = 512
    gla_p, gdn_p, qkv_tail = _inproj(xp, g1, w1, wa2, ba, alog, dtb, tm=tm_p, conv_w=cwb, seq_len=tp)
    oa_p, p_gla = _gla_prompt(gla_p, gna, bp, tp)
    ob_p, p_gdn = _gdn_prompt(gdn_p, gnb, bp, tp)
    y_p, ffn_tail = _ffn(xp, oa_p, ob_p, wo, g2, wu, cwf, wd, gf, tm=tm_p, seq_len=tp)
    y_prompt = y_p.reshape(bp, tp, D_MODEL)
    p_conv = qkv_tail.reshape(bp, tp // tm_p, SUBLANE, GDN_CONV_C)[:, -1, SUBLANE - (GDN_CONV_W - 1):, :]
    p_ffn = ffn_tail.reshape(bp, tp // tm_p, SUBLANE, 2 * D_FF)[:, -1, SUBLANE - (FFN_CONV_W - 1):, :]

    xs = x_sample.reshape(bs * ts, D_MODEL)
    gla_s, gdn_s = _inproj(xs, g1, w1, wa2, ba, alog, dtb, tm=256)
    oa_s, ob_s, s_gla, s_gdn, *conv_planes = _mix_sample(
        gla_s, gdn_s, [state_gdn_conv[0][:, j] for j in range(GDN_CONV_W - 1)],
        state_gla[0], state_gdn[0], cwb, gna, gnb, ts)
    y_s, *ffn_planes = _ffn(xs, oa_s, ob_s, wo, g2, wu, cwf, wd, gf, tm=128, seq_len=ts,
                            hist=[state_ffn_conv[0][:, j] for j in range(FFN_CONV_W - 1)])
    y_sample = y_s.reshape(bs, ts, D_MODEL)
    s_conv = jnp.stack(conv_planes, axis=1)
    s_ffn = jnp.stack(ffn_planes, axis=1)

    return (y_prompt, y_sample, p_gla[None], p_gdn[None], p_conv[None], p_ffn[None],
            s_gla[None], s_gdn[None], s_conv[None], s_ffn[None])
```

```python
import functools

import jax
import jax.numpy as jnp
from jax import lax
from jax.experimental import pallas as pl
from jax.experimental.pallas import tpu as pltpu

F32 = jnp.float32
BF16 = jnp.bfloat16

D_MODEL = 1024
H = 4
GLA_DK = 64
GLA_DV = 128
GLA_RANK = 16
GLA_TAU = 16.0
GDN_D = 128
GDN_CONV_W = 4
D_FF = 2816
FFN_CONV_W = 3
EPS = 1e-6

GLA_QK = H * GLA_DK
GLA_V = H * GLA_DV
GDN_QK = H * GDN_D
GDN_V = H * GDN_D
GDN_CONV_C = 2 * GDN_QK + GDN_V
IN_SIZES = (GLA_QK, GLA_QK, GLA_V, GLA_V, GLA_RANK, GDN_CONV_C, GDN_V, H, H)

LANE = 128
SUBLANE = 8
VMEM_LIMIT = 56 * 1024 * 1024

W_GLA = 2 * GLA_QK + 2 * GLA_V
W_GDN = GDN_CONV_C + GDN_V
GLA_COLS = W_GLA + GLA_QK
GDN_COLS = W_GDN + LANE

CHUNK = 128
CHUNKS_PER_STEP = 8
FF_CK = 256
FF_LOOKAHEAD = 11


def _dot(a, b):
    return jnp.dot(a, b, preferred_element_type=F32)


def _dot_nt(a, b):
    return lax.dot_general(a, b, (((1,), (1,)), ((), ())), preferred_element_type=F32)


def _split2(x):
    hi = x.astype(BF16)
    lo = (x - hi.astype(F32)).astype(BF16)
    return hi, lo


def _dot3(a, b):
    ah, al = _split2(a)
    bh, bl = _split2(b)
    return _dot(jnp.concatenate([ah, ah, al], axis=1), jnp.concatenate([bh, bl, bh], axis=0))


def _dot3_many(a_list, b_list):
    lhs = [jnp.concatenate([ah, ah, al], axis=1) for ah, al in map(_split2, a_list)]
    rhs = [jnp.concatenate([bh, bl, bh], axis=0) for bh, bl in map(_split2, b_list)]
    return [_dot(x, y) for x, y in zip(lhs, rhs)]


def _exact_rows(sel_bf16, x):
    h1 = x.astype(BF16)
    r1 = x - h1.astype(F32)
    h2 = r1.astype(BF16)
    h3 = (r1 - h2.astype(F32)).astype(BF16)
    return _dot(jnp.concatenate([sel_bf16] * 3, axis=1), jnp.concatenate([h1, h2, h3], axis=0))


def _sigmoid(x):
    return 1.0 / (1.0 + jnp.exp(-x))


def _silu(x):
    return x * _sigmoid(x)


def _softplus(x):
    return jnp.maximum(x, 0.0) + jnp.log(1.0 + jnp.exp(-jnp.abs(x)))


def _rms(x, g):
    ms = jnp.mean(x * x, axis=-1, keepdims=True)
    return x * lax.rsqrt(ms + EPS) * g


def _shift_rows(x, k, prev8):
    rolled = pltpu.roll(x, k, 0)
    prev_rolled = pltpu.roll(prev8, k, 0)
    row = lax.broadcasted_iota(jnp.int32, (SUBLANE, x.shape[1]), 0)
    first = jnp.where(row < k, prev_rolled, rolled[0:SUBLANE])
    return jnp.concatenate([first, rolled[SUBLANE:]], axis=0)


def _gdn_qkv_activation(y, off):
    a = _silu(y)
    outs = []
    for j in range(0, a.shape[1], GDN_D):
        blk = a[:, j:j + GDN_D]
        if off + j < 2 * GDN_QK:
            blk = blk * lax.rsqrt(jnp.sum(blk * blk, axis=-1, keepdims=True) + EPS)
            if off + j < GDN_QK:
                blk = blk * (GDN_D ** -0.5)
        outs.append(blk)
    return jnp.concatenate(outs, axis=1)


def _inproj_kernel(*refs, seq_tiles):
    if seq_tiles is None:
        (x_ref, g1_ref, wa_ref, wb_ref, ws_ref, wa2_ref, ba_ref, alog_ref, dtb_ref,
         gla_ref, gdn_ref) = refs
    else:
        (x_ref, g1_ref, wa_ref, wb_ref, ws_ref, wa2_ref, ba_ref, alog_ref, dtb_ref, cw_ref,
         gla_ref, gdn_ref, tail_ref, xs_scr) = refs
    tm = x_ref.shape[0]
    hn = _rms(x_ref[...], g1_ref[...]).astype(BF16)
    small = _dot(hn, ws_ref[...])
    if seq_tiles is None:
        for c in range(0, W_GLA, 512):
            gla_ref[:, c:c + 512] = _dot(hn, wa_ref[:, c:c + 512])
        for c in range(0, W_GDN, 512):
            gdn_ref[:, c:c + 512] = _dot(hn, wb_ref[:, c:c + 512])
    else:
        @pl.when(pl.program_id(0) % seq_tiles == 0)
        def _():
            xs_scr[0:SUBLANE, :] = jnp.zeros((SUBLANE, GDN_CONV_C), F32)

        for c in range(0, GDN_CONV_C, 512):
            xs_scr[SUBLANE:SUBLANE + tm, c:c + 512] = _dot(hn, wb_ref[:, c:c + 512])
        gdn_ref[:, GDN_CONV_C:W_GDN] = _dot(hn, wb_ref[:, GDN_CONV_C:W_GDN])
        for c in range(0, GDN_CONV_C, 512):
            gla_ref[:, c:c + 512] = _dot(hn, wa_ref[:, c:c + 512])
            cols = slice(c, c + 512)
            y = xs_scr[SUBLANE:SUBLANE + tm, cols] * cw_ref[GDN_CONV_W - 1:GDN_CONV_W, cols]
            for kk in range(1, GDN_CONV_W):
                y = y + (xs_scr[SUBLANE - kk:SUBLANE - kk + tm, cols]
                         * cw_ref[GDN_CONV_W - 1 - kk:GDN_CONV_W - kk, cols])
            gdn_ref[:, cols] = _gdn_qkv_activation(y, c)
            tail = xs_scr[tm:tm + SUBLANE, cols]
            tail_ref[0, :, cols] = tail
            xs_scr[0:SUBLANE, cols] = tail
    xa = _dot(small[:, 0:LANE].astype(BF16), wa2_ref[...]) + ba_ref[...]
    gla_ref[:, W_GLA:GLA_COLS] = -_softplus(-xa) * (1.0 / GLA_TAU)
    bd = small[:, LANE:2 * LANE]
    lane = lax.broadcasted_iota(jnp.int32, bd.shape, 1)
    beta = _sigmoid(bd)
    g = -jnp.exp(alog_ref[...]) * _softplus(bd + dtb_ref[...])
    gdn_ref[:, W_GDN:GDN_COLS] = jnp.where(lane < H, beta, jnp.where(lane < 2 * H, g, 0.0))


def _inproj(x2d, g1, w_in_parts, wa2, ba, alog, dtb, tm, conv_w=None, seq_len=None):
    n = x2d.shape[0]
    const = lambda i: (0, 0)
    fused_conv = conv_w is not None
    in_specs = [
        pl.BlockSpec((tm, D_MODEL), lambda i: (i, 0)),
        pl.BlockSpec((1, D_MODEL), const),
        pl.BlockSpec((D_MODEL, W_GLA), const),
        pl.BlockSpec((D_MODEL, W_GDN), const),
        pl.BlockSpec((D_MODEL, 2 * LANE), const),
        pl.BlockSpec((LANE, GLA_QK), const),
        pl.BlockSpec((1, GLA_QK), const),
        pl.BlockSpec((1, LANE), const),
        pl.BlockSpec((1, LANE), const),
    ]
    args = [x2d, g1, *w_in_parts, wa2, ba, alog, dtb]
    out_specs = [
        pl.BlockSpec((tm, GLA_COLS), lambda i: (i, 0)),
        pl.BlockSpec((tm, GDN_COLS), lambda i: (i, 0)),
    ]
    out_shape = [
        jax.ShapeDtypeStruct((n, GLA_COLS), F32),
        jax.ShapeDtypeStruct((n, GDN_COLS), F32),
    ]
    scratch = []
    seq_tiles = None
    if fused_conv:
        assert seq_len % tm == 0
        seq_tiles = seq_len // tm
        in_specs.append(pl.BlockSpec((GDN_CONV_W, GDN_CONV_C), const))
        args.append(conv_w)
        out_specs.append(pl.BlockSpec((1, SUBLANE, GDN_CONV_C), lambda i: (i, 0, 0)))
        out_shape.append(jax.ShapeDtypeStruct((n // tm, SUBLANE, GDN_CONV_C), F32))
        scratch.append(pltpu.VMEM((SUBLANE + tm, GDN_CONV_C), F32))
    return pl.pallas_call(
        functools.partial(_inproj_kernel, seq_tiles=seq_tiles),
        grid=(n // tm,),
        in_specs=in_specs,
        out_specs=out_specs,
        out_shape=out_shape,
        scratch_shapes=scratch,
        compiler_params=pltpu.CompilerParams(
            dimension_semantics=("arbitrary" if fused_conv else "parallel",),
            vmem_limit_bytes=VMEM_LIMIT),
        name="inproj_conv" if fused_conv else "inproj",
    )(*args)


def _gla_chunk_kernel(gla_ref, gn_ref, o_ref, st_ref, h_scr):
    tc = CHUNK

    @pl.when(pl.program_id(1) == 0)
    def _():
        h_scr[...] = jnp.zeros_like(h_scr)

    row = lax.broadcasted_iota(jnp.int32, (tc, tc), 0)
    col = lax.broadcasted_iota(jnp.int32, (tc, tc), 1)
    causal = row >= col
    tri = causal.astype(BF16)
    lane_head = lax.broadcasted_iota(jnp.int32, (1, GLA_QK), 1) // GLA_DK
    rblk = lax.broadcasted_iota(jnp.int32, (GLA_QK, GLA_V), 0) // GLA_DK
    cblk = lax.broadcasted_iota(jnp.int32, (GLA_QK, GLA_V), 1) // GLA_DV
    gn = gn_ref[...]
    hbd = h_scr[...]

    chunks = range(CHUNKS_PER_STEP)
    rows = [slice(j * tc, (j + 1) * tc) for j in chunks]
    b = [_exact_rows(tri, gla_ref[rs, W_GLA:GLA_COLS]) for rs in rows]
    qi, qm, km, vb, klt, e_last = [], [], [], [], [], []
    for j in chunks:
        q = gla_ref[rows[j], 0:GLA_QK] * (GLA_DK ** -0.5)
        k = gla_ref[rows[j], GLA_QK:2 * GLA_QK]
        vb.append(gla_ref[rows[j], 2 * GLA_QK:2 * GLA_QK + GLA_V].astype(BF16))
        b_mid = b[j][tc // 2 - 1:tc // 2, :]
        qm.append((q * jnp.exp(b[j] - b_mid)).astype(BF16))
        km.append((k * jnp.exp(b_mid - b[j])).astype(BF16))
        qi.append((q * jnp.exp(b[j])).astype(BF16))
        bt = b[j].T
        b_last = bt[:, tc - 1:tc]
        klt.append((k.T * jnp.exp(b_last - bt)).astype(BF16))
        e_last.append(jnp.exp(b_last))
    scores = {(j, h): _dot_nt(jnp.where(lane_head == h, qm[j], jnp.zeros_like(qm[j])), km[j])
              for j in chunks for h in range(H)}
    scores = {p: jnp.where(causal, s, 0.0).astype(BF16) for p, s in scores.items()}
    o_intra = {(j, h): _dot(scores[j, h], vb[j][:, h * GLA_DV:(h + 1) * GLA_DV])
               for j in chunks for h in range(H)}
    upd = [jnp.where(rblk == cblk, _dot(klt[j], vb[j]), 0.0) for j in chunks]

    for j in chunks:
        rs = rows[j]
        o_inter = _dot(qi[j], hbd.astype(BF16))
        hbd = hbd * e_last[j] + upd[j]
        for h in range(H):
            sl = slice(h * GLA_DV, (h + 1) * GLA_DV)
            oh = o_intra[j, h] + o_inter[:, sl]
            gate = gla_ref[rs, 2 * GLA_QK + GLA_V + h * GLA_DV:2 * GLA_QK + GLA_V + (h + 1) * GLA_DV]
            o_ref[rs, sl] = (_rms(oh, gn) * _silu(gate)).astype(o_ref.dtype)

    h_scr[...] = hbd
    for h in range(H):
        st_ref[0, h] = hbd[h * GLA_DK:(h + 1) * GLA_DK, h * GLA_DV:(h + 1) * GLA_DV]


def _gla_prompt(gla2d, gn, bsz, t):
    step = CHUNKS_PER_STEP * CHUNK
    assert t % step == 0
    nc = t // step
    return pl.pallas_call(
        _gla_chunk_kernel,
        grid=(bsz, nc),
        in_specs=[
            pl.BlockSpec((step, GLA_COLS), lambda b, c: (b * nc + c, 0)),
            pl.BlockSpec((1, GLA_DV), lambda b, c: (0, 0)),
        ],
        out_specs=[
            pl.BlockSpec((step, GLA_V), lambda b, c: (b * nc + c, 0)),
            pl.BlockSpec((1, H, GLA_DK, GLA_DV), lambda b, c: (b, 0, 0, 0)),
        ],
        out_shape=[
            jax.ShapeDtypeStruct((bsz * t, GLA_V), BF16),
            jax.ShapeDtypeStruct((bsz, H, GLA_DK, GLA_DV), F32),
        ],
        scratch_shapes=[pltpu.VMEM((GLA_QK, GLA_V), F32)],
        compiler_params=pltpu.CompilerParams(
            dimension_semantics=("parallel", "arbitrary"), vmem_limit_bytes=VMEM_LIMIT),
        name="gla_prompt",
    )(gla2d, gn)


def _tri_inverse(l_strict, eye, order):
    n = eye.shape[0]
    refine = order > 2 * SUBLANE

    def dots(a_list, b_list):
        if not refine:
            return _dot3_many(a_list, b_list)
        a_list = [a.astype(BF16) for a in a_list]
        b_list = [b.astype(BF16) for b in b_list]
        return [_dot(a, b) for a, b in zip(a_list, b_list)]

    m = [-l for l in l_strict]
    p = [eye + x for x in m]
    m = dots(m, m)
    power = 2
    while 2 * power < order:
        pm = dots([jnp.concatenate([a, b], axis=0) for a, b in zip(p, m)], m)
        p = [a + x[:n] for a, x in zip(p, pm)]
        m = [x[n:] for x in pm]
        power *= 2
    t = [a + x for a, x in zip(p, dots(p, m))]
    if refine:
        r = [eye - a - x for a, x in zip(t, _dot3_many(l_strict, t))]
        t = [a + x for a, x in zip(t, dots(t, r))]
    return t


def _gdn_chunk_kernel(gdn_ref, gn_ref, o_ref, st_ref, h_scr):
    tc = CHUNK

    @pl.when(pl.program_id(1) == 0)
    def _():
        h_scr[...] = jnp.zeros_like(h_scr)

    row = lax.broadcasted_iota(jnp.int32, (tc, tc), 0)
    col = lax.broadcasted_iota(jnp.int32, (tc, tc), 1)
    causal = row >= col
    strict = row > col
    eye = (row == col).astype(F32)
    tri = causal.astype(BF16)
    gn = gn_ref[...]
    heads = range(H)
    sls = [slice(h * GDN_D, (h + 1) * GDN_D) for h in heads]
    hh = [h_scr[sls[h], :] for h in heads]

    probs = [(j, h) for j in range(CHUNKS_PER_STEP) for h in heads]
    rows = [slice(j * tc, (j + 1) * tc) for j in range(CHUNKS_PER_STEP)]
    bg = [gdn_ref[rs, W_GDN:GDN_COLS] for rs in rows]
    gc = [_exact_rows(tri, x) for x in bg]
    gt = [x.T for x in gc]
    kf, kb, qb, beta, gcol, decay = {}, {}, {}, {}, {}, {}
    for j, h in probs:
        kf[j, h] = gdn_ref[rows[j], GDN_QK + h * GDN_D:GDN_QK + (h + 1) * GDN_D]
        kb[j, h] = kf[j, h].astype(BF16)
        qb[j, h] = gdn_ref[rows[j], sls[h]].astype(BF16)
        beta[j, h] = bg[j][:, h:h + 1]
        gcol[j, h] = gc[j][:, H + h:H + h + 1]
        grow = gt[j][H + h:H + h + 1, :]
        decay[j, h] = jnp.where(causal, jnp.exp(jnp.where(causal, gcol[j, h] - grow, 0.0)), 0.0)
    qk_kk = {p: _dot_nt(jnp.concatenate([qb[p], kb[p]], axis=0), kb[p]) for p in probs}
    tinv = dict(zip(probs, _tri_inverse(
        [jnp.where(strict, beta[p] * qk_kk[p][tc:] * decay[p], 0.0) for p in probs], eye, tc)))

    for j in range(CHUNKS_PER_STEP):
        rs = rows[j]
        kq_h = [_dot(jnp.concatenate([kb[j, h], qb[j, h]], axis=0), hh[h].astype(BF16))
                for h in heads]
        eg = [jnp.exp(gcol[j, h]) for h in heads]
        rhs = [beta[j, h] * (gdn_ref[rs, 2 * GDN_QK + h * GDN_D:2 * GDN_QK + (h + 1) * GDN_D]
                             - eg[h] * kq_h[h][:tc]) for h in heads]
        ub = [u.astype(BF16) for u in _dot3_many([tinv[j, h] for h in heads], rhs)]
        ou = []
        for h in heads:
            qk = (qk_kk[j, h][:tc] * decay[j, h]).astype(BF16)
            g_last = gcol[j, h][tc - 1:tc, :]
            kd = (kf[j, h] * jnp.exp(g_last - gcol[j, h])).T.astype(BF16)
            ou.append(_dot(jnp.concatenate([qk, kd], axis=0), ub[h]))
        for h in heads:
            o = eg[h] * kq_h[h][tc:] + ou[h][:tc]
            hh[h] = jnp.exp(gcol[j, h][tc - 1:tc, :]) * hh[h] + ou[h][tc:]
            z = gdn_ref[rs, GDN_CONV_C + h * GDN_D:GDN_CONV_C + (h + 1) * GDN_D]
            o_ref[rs, sls[h]] = (_rms(o, gn) * _silu(z)).astype(o_ref.dtype)

    for h in heads:
        h_scr[sls[h], :] = hh[h]
        st_ref[0, h] = hh[h]


def _gdn_prompt(gdn2d, gn, bsz, t):
    step = CHUNKS_PER_STEP * CHUNK
    assert t % step == 0
    nc = t // step
    return pl.pallas_call(
        _gdn_chunk_kernel,
        grid=(bsz, nc),
        in_specs=[
            pl.BlockSpec((step, GDN_COLS), lambda b, c: (b * nc + c, 0)),
            pl.BlockSpec((1, GDN_D), lambda b, c: (0, 0)),
        ],
        out_specs=[
            pl.BlockSpec((step, GDN_V), lambda b, c: (b * nc + c, 0)),
            pl.BlockSpec((1, H, GDN_D, GDN_D), lambda b, c: (b, 0, 0, 0)),
        ],
        out_shape=[
            jax.ShapeDtypeStruct((bsz * t, GDN_V), BF16),
            jax.ShapeDtypeStruct((bsz, H, GDN_D, GDN_D), F32),
        ],
        scratch_shapes=[pltpu.VMEM((H * GDN_D, GDN_D), F32)],
        compiler_params=pltpu.CompilerParams(
            dimension_semantics=("parallel", "arbitrary"), vmem_limit_bytes=VMEM_LIMIT),
        name="gdn_prompt",
    )(gdn2d, gn)


SEQ_PER_STEP = 16


def _mix_sample_kernel(gla_ref, gdn_ref, h1_ref, h2_ref, h3_ref, sa_ref, sb_ref, cw_ref, gna_ref, gnb_ref,
                       oa_ref, ob_ref, sta_ref, stb_ref, cv1_ref, cv2_ref, cv3_ref, a_scr, b_scr, *, t):
    n = SEQ_PER_STEP
    rows = n * t
    tc = LANE
    spg = SUBLANE // t

    @pl.when(pl.program_id(0) == 0)
    def _():
        a_scr[...] = jnp.zeros_like(a_scr)
        b_scr[...] = jnp.zeros_like(b_scr)

    nh = GDN_CONV_W - 1
    st_refs = (h1_ref, h2_ref, h3_ref)
    cv_refs = (cv1_ref, cv2_ref, cv3_ref)
    x = gdn_ref[:, 0:GDN_CONV_C]
    pos = lax.broadcasted_iota(jnp.int32, x.shape, 0) % t
    r_ts = lax.broadcasted_iota(jnp.int32, (rows, n), 0)
    s_ts = lax.broadcasted_iota(jnp.int32, (rows, n), 1)
    spread = [(r_ts == s_ts * t + p).astype(BF16) for p in range(nh)]
    r_st = lax.broadcasted_iota(jnp.int32, (n, rows), 1)
    s_st = lax.broadcasted_iota(jnp.int32, (n, rows), 0)
    y = x * cw_ref[GDN_CONV_W - 1:GDN_CONV_W, :]
    for kk in range(1, GDN_CONV_W):
        hist = _exact_rows(jnp.concatenate(spread[:kk], axis=1),
                           jnp.concatenate([st_refs[nh + p - kk][...] for p in range(kk)], axis=0))
        m = jnp.where(pos >= kk, pltpu.roll(x, kk, 0), hist)
        y = y + m * cw_ref[GDN_CONV_W - 1 - kk:GDN_CONV_W - kk, :]
    for j in range(nh):
        cv_refs[j][...] = _exact_rows((r_st == s_st * t + t - nh + j).astype(BF16), x)
    b_scr[0:rows, 0:GDN_CONV_C] = _silu(y)
    b_scr[0:rows, GDN_CONV_C:GDN_CONV_C + LANE] = gdn_ref[:, W_GDN:GDN_COLS]
    a_scr[0:rows, :] = gla_ref[...]

    row = lax.broadcasted_iota(jnp.int32, (tc, tc), 0)
    col = lax.broadcasted_iota(jnp.int32, (tc, tc), 1)
    same = (row // t) == (col // t)
    causal = same & (row >= col)
    strict = same & (row > col)
    eye = (row == col).astype(F32)
    tri = causal.astype(BF16)
    last = (col == (row // t) * t + (t - 1)).astype(BF16)
    lane_seq = lax.broadcasted_iota(jnp.int32, (1, tc), 1) // t
    sub = lax.broadcasted_iota(jnp.int32, (SUBLANE, LANE), 0)

    def group_of(x, s):
        g = s // spg
        return x[g * SUBLANE:(g + 1) * SUBLANE]

    def pick_rows(res, off):
        groups = []
        for g in range(rows // SUBLANE):
            piece = res[g * spg + spg - 1][off:off + SUBLANE]
            for j in reversed(range(spg - 1)):
                piece = jnp.where(sub < (j + 1) * t, res[g * spg + j][off:off + SUBLANE], piece)
            groups.append(piece)
        return jnp.concatenate(groups, axis=0)

    q = a_scr[:, 0:GLA_QK] * (GLA_DK ** -0.5)
    k = a_scr[:, GLA_QK:2 * GLA_QK]
    vb = a_scr[:, 2 * GLA_QK:2 * GLA_QK + GLA_V].astype(BF16)
    b = _exact_rows(tri, a_scr[:, W_GLA:GLA_COLS])
    b_end = _exact_rows(last, b)
    qe_f = q * jnp.exp(b)
    qe = qe_f.astype(BF16)
    ke = (k * jnp.exp(-b)).astype(BF16)
    klt = (k * jnp.exp(b_end - b)).T.astype(BF16)
    bt = b.T
    lane_head = lax.broadcasted_iota(jnp.int32, (1, GLA_QK), 1) // GLA_DK
    qmask = [jnp.where(lane_head == h, qe, jnp.zeros_like(qe)) for h in range(H)]
    scores = [_dot_nt(qmask[h], ke) for h in range(H)]
    scores = [jnp.where(causal, s, 0.0).astype(BF16) for s in scores]
    o_intra = [_dot(scores[h], vb[:, h * GLA_DV:(h + 1) * GLA_DV]) for h in range(H)]
    hs = [jnp.concatenate([sa_ref[s, h] for h in range(H)], axis=0) for s in range(n)]
    lhs_q = [jnp.concatenate([jnp.where(lane_head == h, group_of(qe_f, s), 0.0) for h in range(H)],
                             axis=0).astype(BF16) for s in range(0, n, spg)]
    res = [_dot(lhs_q[s // spg], hs[s].astype(BF16)) for s in range(n)]
    upd = [_dot(jnp.where(lane_seq == s, klt, jnp.zeros_like(klt)), vb) for s in range(n)]
    gna = gna_ref[...]
    for h in range(H):
        sl = slice(h * GLA_DV, (h + 1) * GLA_DV)
        oh = o_intra[h][0:rows] + pick_rows(res, h * SUBLANE)
        gate = gla_ref[:, 2 * GLA_QK + GLA_V + h * GLA_DV:2 * GLA_QK + GLA_V + (h + 1) * GLA_DV]
        oa_ref[:, sl] = (_rms(oh, gna) * _silu(gate)).astype(oa_ref.dtype)
    for s in range(n):
        e_col = jnp.exp(bt[:, s * t + t - 1:s * t + t])
        diag = jnp.concatenate(
            [upd[s][h * GLA_DK:(h + 1) * GLA_DK, h * GLA_DV:(h + 1) * GLA_DV] for h in range(H)], axis=0)
        new = hs[s] * e_col + diag
        for h in range(H):
            sta_ref[s, h] = new[h * GLA_DK:(h + 1) * GLA_DK]

    bg = b_scr[:, GDN_CONV_C:GDN_CONV_C + LANE]
    gc = _exact_rows(tri, bg)
    gl = _exact_rows(last, gc)
    gt = gc.T
    heads = range(H)
    sls = [slice(h * GDN_D, (h + 1) * GDN_D) for h in heads]
    kf, qf, kb, qb, vf, beta, gcol, decay = [], [], [], [], [], [], [], []
    for h in heads:
        qh = b_scr[:, sls[h]]
        kh = b_scr[:, GDN_QK + h * GDN_D:GDN_QK + (h + 1) * GDN_D]
        vf.append(b_scr[:, 2 * GDN_QK + h * GDN_D:2 * GDN_QK + (h + 1) * GDN_D])
        qh = qh * lax.rsqrt(jnp.sum(qh * qh, axis=-1, keepdims=True) + EPS) * (GDN_D ** -0.5)
        kh = kh * lax.rsqrt(jnp.sum(kh * kh, axis=-1, keepdims=True) + EPS)
        kf.append(kh)
        qf.append(qh)
        kb.append(kh.astype(BF16))
        qb.append(qh.astype(BF16))
        beta.append(bg[:, h:h + 1])
        gcol.append(gc[:, H + h:H + h + 1])
        grow = gt[H + h:H + h + 1, :]
        decay.append(jnp.where(causal, jnp.exp(jnp.where(causal, gcol[h] - grow, 0.0)), 0.0))
    qk_kk = [_dot_nt(jnp.concatenate([qb[h], kb[h]], axis=0), kb[h]) for h in heads]
    hsb = [[sb_ref[s, h] for h in heads] for s in range(n)]
    kq_lhs = [[jnp.concatenate([group_of(kf[h], s), group_of(qf[h], s)], axis=0).astype(BF16)
               for s in range(0, n, spg)] for h in heads]
    kq = [[_dot(kq_lhs[h][s // spg], hsb[s][h].astype(BF16)) for s in range(n)]
          for h in heads]
    tinv = _tri_inverse([jnp.where(strict, beta[h] * qk_kk[h][tc:] * decay[h], 0.0) for h in heads], eye, t)
    pad_rows = jnp.zeros((tc - rows, GDN_D), F32)
    eg = [jnp.exp(gcol[h]) for h in heads]
    k_h = [jnp.concatenate([pick_rows(kq[h], 0), pad_rows], axis=0) for h in heads]
    rhs = [beta[h] * (vf[h] - eg[h] * k_h[h]) for h in heads]
    ub = [u.astype(BF16) for u in _dot3_many(tinv, rhs)]
    qku = [_dot((qk_kk[h][:tc] * decay[h]).astype(BF16), ub[h]) for h in heads]
    kdt = [(kf[h] * jnp.exp(gl[:, H + h:H + h + 1] - gcol[h])).T.astype(BF16) for h in heads]
    updb = [[_dot(jnp.where(lane_seq == s, kdt[h], jnp.zeros_like(kdt[h])), ub[h]) for s in range(n)]
            for h in heads]
    gnb = gnb_ref[...]
    for h in heads:
        o = eg[h][0:rows] * pick_rows(kq[h], SUBLANE) + qku[h][0:rows]
        z = gdn_ref[:, GDN_CONV_C + h * GDN_D:GDN_CONV_C + (h + 1) * GDN_D]
        ob_ref[:, sls[h]] = (_rms(o, gnb) * _silu(z)).astype(ob_ref.dtype)
        for s in range(n):
            e_last = jnp.exp(gl[s * t:s * t + 1, H + h:H + h + 1])
            stb_ref[s, h] = e_last * hsb[s][h] + updb[h][s]


def _mix_sample(gla2d, gdn2d, hists, sa, sb, cw, gna, gnb, t):
    n_rows = gla2d.shape[0]
    rows = SEQ_PER_STEP * t
    assert SUBLANE % t == 0 and rows <= LANE and n_rows % rows == 0
    bsz = n_rows // t
    r2 = lambda i: (i, 0)
    i4 = lambda i: (i, 0, 0, 0)
    c2 = lambda i: (0, 0)
    return pl.pallas_call(
        functools.partial(_mix_sample_kernel, t=t),
        grid=(n_rows // rows,),
        in_specs=[
            pl.BlockSpec((rows, GLA_COLS), r2),
            pl.BlockSpec((rows, GDN_COLS), r2),
            pl.BlockSpec((SEQ_PER_STEP, GDN_CONV_C), r2),
            pl.BlockSpec((SEQ_PER_STEP, GDN_CONV_C), r2),
            pl.BlockSpec((SEQ_PER_STEP, GDN_CONV_C), r2),
            pl.BlockSpec((SEQ_PER_STEP, H, GLA_DK, GLA_DV), i4),
            pl.BlockSpec((SEQ_PER_STEP, H, GDN_D, GDN_D), i4),
            pl.BlockSpec((GDN_CONV_W, GDN_CONV_C), c2),
            pl.BlockSpec((1, GLA_DV), c2),
            pl.BlockSpec((1, GDN_D), c2),
        ],
        out_specs=[
            pl.BlockSpec((rows, GLA_V), r2),
            pl.BlockSpec((rows, GDN_V), r2),
            pl.BlockSpec((SEQ_PER_STEP, H, GLA_DK, GLA_DV), i4),
            pl.BlockSpec((SEQ_PER_STEP, H, GDN_D, GDN_D), i4),
        ] + [pl.BlockSpec((SEQ_PER_STEP, GDN_CONV_C), r2)] * (GDN_CONV_W - 1),
        out_shape=[
            jax.ShapeDtypeStruct((n_rows, GLA_V), BF16),
            jax.ShapeDtypeStruct((n_rows, GDN_V), BF16),
            jax.ShapeDtypeStruct((bsz, H, GLA_DK, GLA_DV), F32),
            jax.ShapeDtypeStruct((bsz, H, GDN_D, GDN_D), F32),
        ] + [jax.ShapeDtypeStruct((bsz, GDN_CONV_C), F32)] * (GDN_CONV_W - 1),
        scratch_shapes=[pltpu.VMEM((LANE, GLA_COLS), F32),
                        pltpu.VMEM((LANE, GDN_CONV_C + LANE), F32)],
        compiler_params=pltpu.CompilerParams(
            dimension_semantics=("arbitrary",), vmem_limit_bytes=VMEM_LIMIT),
        name="mix_sample",
    )(gla2d, gdn2d, *hists, sa, sb, cw, gna, gnb)


def _ffn_kernel(*refs, seq_tiles, has_hist, seq_len):
    nh = FFN_CONV_W - 1
    if has_hist:
        (x_ref, oa_ref, ob_ref, wo_ref, g2_ref, wu_ref, cw_ref, wd_ref, gf_ref) = refs[:9]
        st_refs = refs[9:9 + nh]
        y_ref = refs[9 + nh]
        last_refs = refs[10 + nh:10 + 2 * nh]
        u_scr = refs[10 + 2 * nh]
    else:
        (x_ref, oa_ref, ob_ref, wo_ref, g2_ref, wu_ref, cw_ref, wd_ref, gf_ref,
         y_ref, tail_ref, u_scr) = refs
    tm = x_ref.shape[0]

    if not has_hist:
        @pl.when(pl.program_id(0) % seq_tiles == 0)
        def _():
            u_scr[0:SUBLANE, :] = jnp.zeros((SUBLANE, 2 * D_FF), F32)

    x1 = (x_ref[...] + _dot(oa_ref[...].astype(BF16), wo_ref[0:GLA_V, :])
          + _dot(ob_ref[...].astype(BF16), wo_ref[GLA_V:GLA_V + GDN_V, :]))
    hn = _rms(x1, g2_ref[...]).astype(BF16)

    if has_hist:
        nseq = tm // seq_len
        pos = lax.broadcasted_iota(jnp.int32, (tm, FF_CK), 0) % seq_len
        r_ts = lax.broadcasted_iota(jnp.int32, (tm, nseq), 0)
        s_ts = lax.broadcasted_iota(jnp.int32, (tm, nseq), 1)
        spread = [(r_ts == s_ts * seq_len + p).astype(BF16) for p in range(nh)]
        r_st = lax.broadcasted_iota(jnp.int32, (nseq, tm), 1)
        s_st = lax.broadcasted_iota(jnp.int32, (nseq, tm), 0)
        gather = [(r_st == s_st * seq_len + seq_len - nh + j).astype(BF16) for j in range(nh)]
        hist = [_exact_rows(jnp.concatenate(spread[:k], axis=1),
                            jnp.concatenate([st_refs[nh + p - k][...] for p in range(k)], axis=0))
                for k in range(1, nh + 1)]

    def conv(u, off):
        cols = slice(off, off + FF_CK)
        if has_hist:
            m1, m2 = [jnp.where(pos >= k, pltpu.roll(u, k, 0), hist[k - 1][:, cols])
                      for k in range(1, nh + 1)]
            u_scr[:, cols] = u
        else:
            u = u_scr[SUBLANE:SUBLANE + tm, cols]
            m1 = u_scr[SUBLANE - 1:SUBLANE - 1 + tm, cols]
            m2 = u_scr[SUBLANE - 2:SUBLANE - 2 + tm, cols]
            tail = u_scr[tm:tm + SUBLANE, cols]
            tail_ref[0, :, cols] = tail
            u_scr[0:SUBLANE, cols] = tail
        return (m2 * cw_ref[0:1, cols] + m1 * cw_ref[1:2, cols] + u * cw_ref[2:3, cols])

    def up(c):
        ua = _dot(hn, wu_ref[:, c:c + FF_CK])
        ub = _dot(hn, wu_ref[:, D_FF + c:D_FF + c + FF_CK])
        if has_hist:
            return ua, ub
        u_scr[SUBLANE:SUBLANE + tm, c:c + FF_CK] = ua
        u_scr[SUBLANE:SUBLANE + tm, D_FF + c:D_FF + c + FF_CK] = ub
        return None, None

    acc = jnp.zeros((tm, D_MODEL), F32)
    ahead = FF_LOOKAHEAD
    steps = list(range(0, D_FF, FF_CK))
    pending = [up(c) for c in steps[:ahead]]
    for i, c in enumerate(steps):
        if i + ahead < len(steps):
            pending.append(up(steps[i + ahead]))
        ua, ub = pending.pop(0)
        a = conv(ua, c)
        b = conv(ub, D_FF + c)
        act = (_silu(a) * b).astype(BF16)
        acc = acc + _dot(act, wd_ref[c:c + FF_CK, :])
    y_ref[...] = _rms(x1 + acc, gf_ref[...])
    if has_hist:
        for j in range(nh):
            last_refs[j][...] = _exact_rows(gather[j], u_scr[...])


def _ffn(x2d, oa, ob, wo, g2, wu, cw, wd, gf, tm, seq_len, hist=None):
    n = x2d.shape[0]
    has_hist = hist is not None
    row = lambda i: (i, 0)
    const = lambda i: (0, 0)
    in_specs = [
        pl.BlockSpec((tm, D_MODEL), row),
        pl.BlockSpec((tm, GLA_V), row),
        pl.BlockSpec((tm, GDN_V), row),
        pl.BlockSpec((GLA_V + GDN_V, D_MODEL), const),
        pl.BlockSpec((1, D_MODEL), const),
        pl.BlockSpec((D_MODEL, 2 * D_FF), const),
        pl.BlockSpec((FFN_CONV_W, 2 * D_FF), const),
        pl.BlockSpec((D_FF, D_MODEL), const),
        pl.BlockSpec((1, D_MODEL), const),
    ]
    args = [x2d, oa, ob, wo, g2, wu, cw, wd, gf]
    out_specs = [pl.BlockSpec((tm, D_MODEL), row)]
    out_shape = [jax.ShapeDtypeStruct((n, D_MODEL), F32)]
    scratch = []
    if has_hist:
        assert tm % seq_len == 0
        seq_tiles = 1
        nseq = tm // seq_len
        assert len(hist) == FFN_CONV_W - 1 and seq_len >= FFN_CONV_W - 1
        in_specs += [pl.BlockSpec((nseq, 2 * D_FF), row)] * len(hist)
        args += list(hist)
        out_specs += [pl.BlockSpec((nseq, 2 * D_FF), row)] * len(hist)
        out_shape += [jax.ShapeDtypeStruct((n // seq_len, 2 * D_FF), F32)] * len(hist)
        scratch.append(pltpu.VMEM((tm, 2 * D_FF), F32))
    else:
        assert seq_len % tm == 0
        seq_tiles = seq_len // tm
        out_specs.append(pl.BlockSpec((1, SUBLANE, 2 * D_FF), lambda i: (i, 0, 0)))
        out_shape.append(jax.ShapeDtypeStruct((n // tm, SUBLANE, 2 * D_FF), F32))
        scratch.append(pltpu.VMEM((SUBLANE + tm, 2 * D_FF), F32))
    return pl.pallas_call(
        functools.partial(_ffn_kernel, seq_tiles=seq_tiles, has_hist=has_hist, seq_len=seq_len),
        grid=(n // tm,),
        in_specs=in_specs,
        out_specs=out_specs,
        out_shape=out_shape,
        scratch_shapes=scratch,
        compiler_params=pltpu.CompilerParams(
            dimension_semantics=("arbitrary",), vmem_limit_bytes=VMEM_LIMIT),
        name="ffn_sample" if has_hist else "ffn_prompt",
    )(*args)


def _pad_cols(a, width):
    return jnp.pad(a, ((0, 0), (0, width - a.shape[1])))


def kernel(x_prompt, x_sample, state_gla, state_gdn, state_gdn_conv, state_ffn_conv, norm1_g, w_in, gla_w_a2, gla_b_a, gla_norm_g, gdn_conv_w, gdn_a_log, gdn_dt_bias, gdn_norm_g, w_out, norm2_g, w_up, ffn_conv_w, w_down, norm_f_g):
    assert w_in.shape[0] == 1, "single layer"
    bp, tp, _ = x_prompt.shape
    bs, ts, _ = x_sample.shape

    offs = [0]
    for s in IN_SIZES:
        offs.append(offs[-1] + s)
    wi = w_in[0]
    w1 = (wi[:, offs[0]:offs[4]].astype(BF16),
          wi[:, offs[5]:offs[7]].astype(BF16),
          jnp.concatenate([_pad_cols(wi[:, offs[4]:offs[5]], LANE),
                           _pad_cols(wi[:, offs[7]:offs[9]], LANE)], axis=1).astype(BF16))
    wa2 = jnp.pad(gla_w_a2[0], ((0, LANE - GLA_RANK), (0, 0))).astype(BF16)
    ba = gla_b_a[0][None, :]
    alog = jnp.pad(gdn_a_log[0], (H, LANE - 2 * H))[None, :]
    dtb = jnp.pad(gdn_dt_bias[0], (H, LANE - 2 * H))[None, :]
    g1 = norm1_g[0][None, :]
    g2 = norm2_g[0][None, :]
    gf = norm_f_g[None, :]
    gna = gla_norm_g[0][None, :]
    gnb = gdn_norm_g[0][None, :]
    cwb = gdn_conv_w[0]
    cwf = ffn_conv_w[0]
    wo = w_out[0].astype(BF16)
    wu = w_up[0].astype(BF16)
    wd = w_down[0].astype(BF16)

    xp = x_prompt.reshape(bp * tp, D_MODEL)
    tm_p = 512
    gla_p, gdn_p, qkv_tail = _inproj(xp, g1, w1, wa2, ba, alog, dtb, tm=tm_p, conv_w=cwb, seq_len=tp)
    oa_p, p_gla = _gla_prompt(gla_p, gna, bp, tp)
    ob_p, p_gdn = _gdn_prompt(gdn_p, gnb, bp, tp)
    y_p, ffn_tail = _ffn(xp, oa_p, ob_p, wo, g2, wu, cwf, wd, gf, tm=tm_p, seq_len=tp)
    y_prompt = y_p.reshape(bp, tp, D_MODEL)
    p_conv = qkv_tail.reshape(bp, tp // tm_p, SUBLANE, GDN_CONV_C)[:, -1, SUBLANE - (GDN_CONV_W - 1):, :]
    p_ffn = ffn_tail.reshape(bp, tp // tm_p, SUBLANE, 2 * D_FF)[:, -1, SUBLANE - (FFN_CONV_W - 1):, :]

    xs = x_sample.reshape(bs * ts, D_MODEL)
    gla_s, gdn_s = _inproj(xs, g1, w1, wa2, ba, alog, dtb, tm=256)
    oa_s, ob_s, s_gla, s_gdn, *conv_planes = _mix_sample(
        gla_s, gdn_s, [state_gdn_conv[0][:, j] for j in range(GDN_CONV_W - 1)],
        state_gla[0], state_gdn[0], cwb, gna, gnb, ts)
    y_s, *ffn_planes = _ffn(xs, oa_s, ob_s, wo, g2, wu, cwf, wd, gf, tm=128, seq_len=ts,
                            hist=[state_ffn_conv[0][:, j] for j in range(FFN_CONV_W - 1)])
    y_sample = y_s.reshape(bs, ts, D_MODEL)
    s_conv = jnp.stack(conv_planes, axis=1)
    s_ffn = jnp.stack(ffn_planes, axis=1)

    return (y_prompt, y_sample, p_gla[None], p_gdn[None], p_conv[None], p_ffn[None],
            s_gla[None], s_gdn[None], s_conv[None], s_ffn[None])
```

```python
import functools

import jax
import jax.numpy as jnp
from jax import lax
from jax.experimental import pallas as pl
from jax.experimental.pallas import tpu as pltpu

F32 = jnp.float32
BF16 = jnp.bfloat16

D_MODEL = 1024
H = 4
GLA_DK = 64
GLA_DV = 128
GLA_RANK = 16
GLA_TAU = 16.0
GDN_D = 128
GDN_CONV_W = 4
D_FF = 2816
FFN_CONV_W = 3
EPS = 1e-6

GLA_QK = H * GLA_DK
GLA_V = H * GLA_DV
GDN_QK = H * GDN_D
GDN_V = H * GDN_D
GDN_CONV_C = 2 * GDN_QK + GDN_V
IN_SIZES = (GLA_QK, GLA_QK, GLA_V, GLA_V, GLA_RANK, GDN_CONV_C, GDN_V, H, H)

LANE = 128
SUBLANE = 8
VMEM_LIMIT = 56 * 1024 * 1024

W_GLA = 2 * GLA_QK + 2 * GLA_V
W_GDN = GDN_CONV_C + GDN_V
GLA_COLS = W_GLA + GLA_QK
GDN_COLS = W_GDN + LANE

PROJ_CK = 512
CHUNK = 128
CHUNKS_PER_STEP = 8
FF_CK = 256
FF_LOOKAHEAD = 11


def _dot(a, b):
    return jnp.dot(a, b, preferred_element_type=F32)


def _dot_nt(a, b):
    return lax.dot_general(a, b, (((1,), (1,)), ((), ())), preferred_element_type=F32)


def _split2(x):
    hi = x.astype(BF16)
    lo = (x - hi.astype(F32)).astype(BF16)
    return hi, lo


def _dot3(a, b):
    ah, al = _split2(a)
    bh, bl = _split2(b)
    return _dot(jnp.concatenate([ah, ah, al], axis=1), jnp.concatenate([bh, bl, bh], axis=0))


def _dot3_many(a_list, b_list):
    lhs = [jnp.concatenate([ah, ah, al], axis=1) for ah, al in map(_split2, a_list)]
    rhs = [jnp.concatenate([bh, bl, bh], axis=0) for bh, bl in map(_split2, b_list)]
    return [_dot(x, y) for x, y in zip(lhs, rhs)]


def _exact_rows(sel_bf16, x):
    h1 = x.astype(BF16)
    r1 = x - h1.astype(F32)
    h2 = r1.astype(BF16)
    h3 = (r1 - h2.astype(F32)).astype(BF16)
    return _dot(jnp.concatenate([sel_bf16] * 3, axis=1), jnp.concatenate([h1, h2, h3], axis=0))


def _sigmoid(x):
    return 0.5 * jnp.tanh(0.5 * x) + 0.5


def _silu(x):
    return x * _sigmoid(x)


def _softplus(x):
    return jnp.maximum(x, 0.0) + jnp.log(1.0 + jnp.exp(-jnp.abs(x)))


def _rms(x, g):
    ms = jnp.mean(x * x, axis=-1, keepdims=True)
    return x * lax.rsqrt(ms + EPS) * g


def _gdn_qkv_activation(y, off):
    a = _silu(y)
    outs = []
    for j in range(0, a.shape[1], GDN_D):
        blk = a[:, j:j + GDN_D]
        if off + j < 2 * GDN_QK:
            blk = blk * lax.rsqrt(jnp.sum(blk * blk, axis=-1, keepdims=True) + EPS)
            if off + j < GDN_QK:
                blk = blk * (GDN_D ** -0.5)
        outs.append(blk)
    return jnp.concatenate(outs, axis=1)


def _inproj_kernel(*refs, seq_tiles):
    if seq_tiles is None:
        (x_ref, g1_ref, wa_ref, wb_ref, ws_ref, wa2_ref, ba_ref, alog_ref, dtb_ref,
         gla_ref, gdn_ref) = refs
    else:
        (x_ref, g1_ref, wa_ref, wb_ref, ws_ref, wa2_ref, ba_ref, alog_ref, dtb_ref, cw_ref,
         gla_ref, gdn_ref, tail_ref, xs_scr) = refs
    tm = x_ref.shape[0]
    hn = _rms(x_ref[...], g1_ref[...]).astype(BF16)
    small = _dot(hn, ws_ref[...])
    ck = PROJ_CK
    if seq_tiles is None:
        for c in range(0, W_GLA, ck):
            gla_ref[:, c:c + ck] = _dot(hn, wa_ref[:, c:c + ck])
        for c in range(0, W_GDN, ck):
            gdn_ref[:, c:c + ck] = _dot(hn, wb_ref[:, c:c + ck])
    else:
        @pl.when(pl.program_id(0) % seq_tiles == 0)
        def _():
            xs_scr[0:SUBLANE, :] = jnp.zeros((SUBLANE, GDN_CONV_C), F32)

        for c in range(0, GDN_CONV_C, ck):
            xs_scr[SUBLANE:SUBLANE + tm, c:c + ck] = _dot(hn, wb_ref[:, c:c + ck])
        gdn_ref[:, GDN_CONV_C:W_GDN] = _dot(hn, wb_ref[:, GDN_CONV_C:W_GDN])
        for c in range(0, GDN_CONV_C, ck):
            gla_ref[:, c:c + ck] = _dot(hn, wa_ref[:, c:c + ck])
            cols = slice(c, c + ck)
            y = xs_scr[SUBLANE:SUBLANE + tm, cols] * cw_ref[GDN_CONV_W - 1:GDN_CONV_W, cols]
            for kk in range(1, GDN_CONV_W):
                y = y + (xs_scr[SUBLANE - kk:SUBLANE - kk + tm, cols]
                         * cw_ref[GDN_CONV_W - 1 - kk:GDN_CONV_W - kk, cols])
            gdn_ref[:, cols] = _gdn_qkv_activation(y, c)
            tail = xs_scr[tm:tm + SUBLANE, cols]
            tail_ref[0, :, cols] = tail
            xs_scr[0:SUBLANE, cols] = tail
    xa = _dot(small[:, 0:LANE].astype(BF16), wa2_ref[...]) + ba_ref[...]
    gla_ref[:, W_GLA:GLA_COLS] = -_softplus(-xa) * (1.0 / GLA_TAU)
    bd = small[:, LANE:2 * LANE]
    lane = lax.broadcasted_iota(jnp.int32, bd.shape, 1)
    beta = _sigmoid(bd)
    g = -jnp.exp(alog_ref[...]) * _softplus(bd + dtb_ref[...])
    gdn_ref[:, W_GDN:GDN_COLS] = jnp.where(lane < H, beta, jnp.where(lane < 2 * H, g, 0.0))


def _inproj(x2d, g1, w_in_parts, wa2, ba, alog, dtb, tm, conv_w=None, seq_len=None):
    n = x2d.shape[0]
    const = lambda i: (0, 0)
    fused_conv = conv_w is not None
    in_specs = [
        pl.BlockSpec((tm, D_MODEL), lambda i: (i, 0)),
        pl.BlockSpec((1, D_MODEL), const),
        pl.BlockSpec((D_MODEL, W_GLA), const),
        pl.BlockSpec((D_MODEL, W_GDN), const),
        pl.BlockSpec((D_MODEL, 2 * LANE), const),
        pl.BlockSpec((LANE, GLA_QK), const),
        pl.BlockSpec((1, GLA_QK), const),
        pl.BlockSpec((1, LANE), const),
        pl.BlockSpec((1, LANE), const),
    ]
    args = [x2d, g1, *w_in_parts, wa2, ba, alog, dtb]
    out_specs = [
        pl.BlockSpec((tm, GLA_COLS), lambda i: (i, 0)),
        pl.BlockSpec((tm, GDN_COLS), lambda i: (i, 0)),
    ]
    out_shape = [
        jax.ShapeDtypeStruct((n, GLA_COLS), F32),
        jax.ShapeDtypeStruct((n, GDN_COLS), F32),
    ]
    scratch = []
    seq_tiles = None
    if fused_conv:
        assert seq_len % tm == 0
        seq_tiles = seq_len // tm
        in_specs.append(pl.BlockSpec((GDN_CONV_W, GDN_CONV_C), const))
        args.append(conv_w)
        out_specs.append(pl.BlockSpec((1, SUBLANE, GDN_CONV_C), lambda i: (i, 0, 0)))
        out_shape.append(jax.ShapeDtypeStruct((n // tm, SUBLANE, GDN_CONV_C), F32))
        scratch.append(pltpu.VMEM((SUBLANE + tm, GDN_CONV_C), F32))
    return pl.pallas_call(
        functools.partial(_inproj_kernel, seq_tiles=seq_tiles),
        grid=(n // tm,),
        in_specs=in_specs,
        out_specs=out_specs,
        out_shape=out_shape,
        scratch_shapes=scratch,
        compiler_params=pltpu.CompilerParams(
            dimension_semantics=("arbitrary" if fused_conv else "parallel",),
            vmem_limit_bytes=VMEM_LIMIT),
        name="inproj_conv" if fused_conv else "inproj",
    )(*args)


def _gla_chunk_kernel(gla_ref, gn_ref, o_ref, st_ref, h_scr):
    tc = CHUNK

    @pl.when(pl.program_id(1) == 0)
    def _():
        h_scr[...] = jnp.zeros_like(h_scr)

    row = lax.broadcasted_iota(jnp.int32, (tc, tc), 0)
    col = lax.broadcasted_iota(jnp.int32, (tc, tc), 1)
    causal = row >= col
    tri = causal.astype(BF16)
    lane_head = lax.broadcasted_iota(jnp.int32, (1, GLA_QK), 1) // GLA_DK
    rblk = lax.broadcasted_iota(jnp.int32, (GLA_QK, GLA_V), 0) // GLA_DK
    cblk = lax.broadcasted_iota(jnp.int32, (GLA_QK, GLA_V), 1) // GLA_DV
    gn = gn_ref[...]
    hbd = h_scr[...]

    chunks = range(CHUNKS_PER_STEP)
    rows = [slice(j * tc, (j + 1) * tc) for j in chunks]
    b = [_exact_rows(tri, gla_ref[rs, W_GLA:GLA_COLS]) for rs in rows]
    qi, qm, km, vb, klt, e_last = [], [], [], [], [], []
    for j in chunks:
        q = gla_ref[rows[j], 0:GLA_QK] * (GLA_DK ** -0.5)
        k = gla_ref[rows[j], GLA_QK:2 * GLA_QK]
        vb.append(gla_ref[rows[j], 2 * GLA_QK:2 * GLA_QK + GLA_V].astype(BF16))
        b_mid = b[j][tc // 2 - 1:tc // 2, :]
        qm.append((q * jnp.exp(b[j] - b_mid)).astype(BF16))
        km.append((k * jnp.exp(b_mid - b[j])).astype(BF16))
        qi.append((q * jnp.exp(b[j])).astype(BF16))
        bt = b[j].T
        b_last = bt[:, tc - 1:tc]
        klt.append((k.T * jnp.exp(b_last - bt)).astype(BF16))
        e_last.append(jnp.exp(b_last))
    scores = {(j, h): _dot_nt(jnp.where(lane_head == h, qm[j], jnp.zeros_like(qm[j])), km[j])
              for j in chunks for h in range(H)}
    scores = {p: jnp.where(causal, s, 0.0).astype(BF16) for p, s in scores.items()}
    o_intra = {(j, h): _dot(scores[j, h], vb[j][:, h * GLA_DV:(h + 1) * GLA_DV])
               for j in chunks for h in range(H)}
    upd = [jnp.where(rblk == cblk, _dot(klt[j], vb[j]), 0.0) for j in chunks]

    for j in chunks:
        rs = rows[j]
        o_inter = _dot(qi[j], hbd.astype(BF16))
        hbd = hbd * e_last[j] + upd[j]
        for h in range(H):
            sl = slice(h * GLA_DV, (h + 1) * GLA_DV)
            oh = o_intra[j, h] + o_inter[:, sl]
            gate = gla_ref[rs, 2 * GLA_QK + GLA_V + h * GLA_DV:2 * GLA_QK + GLA_V + (h + 1) * GLA_DV]
            o_ref[rs, sl] = (_rms(oh, gn) * _silu(gate)).astype(o_ref.dtype)

    h_scr[...] = hbd
    for h in range(H):
        st_ref[0, h] = hbd[h * GLA_DK:(h + 1) * GLA_DK, h * GLA_DV:(h + 1) * GLA_DV]


def _gla_prompt(gla2d, gn, bsz, t):
    step = CHUNKS_PER_STEP * CHUNK
    assert t % step == 0
    nc = t // step
    return pl.pallas_call(
        _gla_chunk_kernel,
        grid=(bsz, nc),
        in_specs=[
            pl.BlockSpec((step, GLA_COLS), lambda b, c: (b * nc + c, 0)),
            pl.BlockSpec((1, GLA_DV), lambda b, c: (0, 0)),
        ],
        out_specs=[
            pl.BlockSpec((step, GLA_V), lambda b, c: (b * nc + c, 0)),
            pl.BlockSpec((1, H, GLA_DK, GLA_DV), lambda b, c: (b, 0, 0, 0)),
        ],
        out_shape=[
            jax.ShapeDtypeStruct((bsz * t, GLA_V), BF16),
            jax.ShapeDtypeStruct((bsz, H, GLA_DK, GLA_DV), F32),
        ],
        scratch_shapes=[pltpu.VMEM((GLA_QK, GLA_V), F32)],
        compiler_params=pltpu.CompilerParams(
            dimension_semantics=("parallel", "arbitrary"), vmem_limit_bytes=VMEM_LIMIT),
        name="gla_prompt",
    )(gla2d, gn)


def _tri_inverse(l_strict, eye, order):
    n = eye.shape[0]
    refine = order > 2 * SUBLANE

    def dots(a_list, b_list):
        if not refine:
            return _dot3_many(a_list, b_list)
        a_list = [a.astype(BF16) for a in a_list]
        b_list = [b.astype(BF16) for b in b_list]
        return [_dot(a, b) for a, b in zip(a_list, b_list)]

    m = [-l for l in l_strict]
    p = [eye + x for x in m]
    m = dots(m, m)
    power = 2
    while 2 * power < order:
        pm = dots([jnp.concatenate([a, b], axis=0) for a, b in zip(p, m)], m)
        p = [a + x[:n] for a, x in zip(p, pm)]
        m = [x[n:] for x in pm]
        power *= 2
    t = [a + x for a, x in zip(p, dots(p, m))]
    if refine:
        r = [eye - a - x for a, x in zip(t, _dot3_many(l_strict, t))]
        t = [a + x for a, x in zip(t, dots(t, r))]
    return t


def _gdn_chunk_kernel(gdn_ref, gn_ref, o_ref, st_ref, h_scr):
    tc = CHUNK

    @pl.when(pl.program_id(1) == 0)
    def _():
        h_scr[...] = jnp.zeros_like(h_scr)

    row = lax.broadcasted_iota(jnp.int32, (tc, tc), 0)
    col = lax.broadcasted_iota(jnp.int32, (tc, tc), 1)
    causal = row >= col
    strict = row > col
    eye = (row == col).astype(F32)
    tri = causal.astype(BF16)
    gn = gn_ref[...]
    heads = range(H)
    sls = [slice(h * GDN_D, (h + 1) * GDN_D) for h in heads]
    hh = [h_scr[sls[h], :] for h in heads]

    probs = [(j, h) for j in range(CHUNKS_PER_STEP) for h in heads]
    rows = [slice(j * tc, (j + 1) * tc) for j in range(CHUNKS_PER_STEP)]
    bg = [gdn_ref[rs, W_GDN:GDN_COLS] for rs in rows]
    gc = [_exact_rows(tri, x) for x in bg]
    gt = [x.T for x in gc]
    kf, kb, qb, beta, gcol, decay = {}, {}, {}, {}, {}, {}
    for j, h in probs:
        kf[j, h] = gdn_ref[rows[j], GDN_QK + h * GDN_D:GDN_QK + (h + 1) * GDN_D]
        kb[j, h] = kf[j, h].astype(BF16)
        qb[j, h] = gdn_ref[rows[j], sls[h]].astype(BF16)
        beta[j, h] = bg[j][:, h:h + 1]
        gcol[j, h] = gc[j][:, H + h:H + h + 1]
        grow = gt[j][H + h:H + h + 1, :]
        decay[j, h] = jnp.where(causal, jnp.exp(jnp.where(causal, gcol[j, h] - grow, 0.0)), 0.0)
    qk_kk = {p: _dot_nt(jnp.concatenate([qb[p], kb[p]], axis=0), kb[p]) for p in probs}
    tinv = dict(zip(probs, _tri_inverse(
        [jnp.where(strict, beta[p] * qk_kk[p][tc:] * decay[p], 0.0) for p in probs], eye, tc)))

    for j in range(CHUNKS_PER_STEP):
        rs = rows[j]
        kq_h = [_dot(jnp.concatenate([kb[j, h], qb[j, h]], axis=0), hh[h].astype(BF16))
                for h in heads]
        eg = [jnp.exp(gcol[j, h]) for h in heads]
        rhs = [beta[j, h] * (gdn_ref[rs, 2 * GDN_QK + h * GDN_D:2 * GDN_QK + (h + 1) * GDN_D]
                             - eg[h] * kq_h[h][:tc]) for h in heads]
        ub = [u.astype(BF16) for u in _dot3_many([tinv[j, h] for h in heads], rhs)]
        ou = []
        for h in heads:
            qk = (qk_kk[j, h][:tc] * decay[j, h]).astype(BF16)
            g_last = gcol[j, h][tc - 1:tc, :]
            kd = (kf[j, h] * jnp.exp(g_last - gcol[j, h])).T.astype(BF16)
            ou.append(_dot(jnp.concatenate([qk, kd], axis=0), ub[h]))
        for h in heads:
            o = eg[h] * kq_h[h][tc:] + ou[h][:tc]
            hh[h] = jnp.exp(gcol[j, h][tc - 1:tc, :]) * hh[h] + ou[h][tc:]
            z = gdn_ref[rs, GDN_CONV_C + h * GDN_D:GDN_CONV_C + (h + 1) * GDN_D]
            o_ref[rs, sls[h]] = (_rms(o, gn) * _silu(z)).astype(o_ref.dtype)

    for h in heads:
        h_scr[sls[h], :] = hh[h]
        st_ref[0, h] = hh[h]


def _gdn_prompt(gdn2d, gn, bsz, t):
    step = CHUNKS_PER_STEP * CHUNK
    assert t % step == 0
    nc = t // step
    return pl.pallas_call(
        _gdn_chunk_kernel,
        grid=(bsz, nc),
        in_specs=[
            pl.BlockSpec((step, GDN_COLS), lambda b, c: (b * nc + c, 0)),
            pl.BlockSpec((1, GDN_D), lambda b, c: (0, 0)),
        ],
        out_specs=[
            pl.BlockSpec((step, GDN_V), lambda b, c: (b * nc + c, 0)),
            pl.BlockSpec((1, H, GDN_D, GDN_D), lambda b, c: (b, 0, 0, 0)),
        ],
        out_shape=[
            jax.ShapeDtypeStruct((bsz * t, GDN_V), BF16),
            jax.ShapeDtypeStruct((bsz, H, GDN_D, GDN_D), F32),
        ],
        scratch_shapes=[pltpu.VMEM((H * GDN_D, GDN_D), F32)],
        compiler_params=pltpu.CompilerParams(
            dimension_semantics=("parallel", "arbitrary"), vmem_limit_bytes=VMEM_LIMIT),
        name="gdn_prompt",
    )(gdn2d, gn)


SEQ_PER_STEP = 16


def _mix_sample_kernel(gla_ref, gdn_ref, h1_ref, h2_ref, h3_ref, sa_ref, sb_ref, cw_ref, gna_ref, gnb_ref,
                       oa_ref, ob_ref, sta_ref, stb_ref, cv1_ref, cv2_ref, cv3_ref, a_scr, b_scr, *, t):
    n = SEQ_PER_STEP
    rows = n * t
    tc = LANE
    spg = SUBLANE // t

    @pl.when(pl.program_id(0) == 0)
    def _():
        a_scr[...] = jnp.zeros_like(a_scr)
        b_scr[...] = jnp.zeros_like(b_scr)

    nh = GDN_CONV_W - 1
    st_refs = (h1_ref, h2_ref, h3_ref)
    cv_refs = (cv1_ref, cv2_ref, cv3_ref)
    x = gdn_ref[:, 0:GDN_CONV_C]
    pos = lax.broadcasted_iota(jnp.int32, x.shape, 0) % t
    r_ts = lax.broadcasted_iota(jnp.int32, (rows, n), 0)
    s_ts = lax.broadcasted_iota(jnp.int32, (rows, n), 1)
    spread = [(r_ts == s_ts * t + p).astype(BF16) for p in range(nh)]
    r_st = lax.broadcasted_iota(jnp.int32, (n, rows), 1)
    s_st = lax.broadcasted_iota(jnp.int32, (n, rows), 0)
    y = x * cw_ref[GDN_CONV_W - 1:GDN_CONV_W, :]
    for kk in range(1, GDN_CONV_W):
        hist = _exact_rows(jnp.concatenate(spread[:kk], axis=1),
                           jnp.concatenate([st_refs[nh + p - kk][...] for p in range(kk)], axis=0))
        m = jnp.where(pos >= kk, pltpu.roll(x, kk, 0), hist)
        y = y + m * cw_ref[GDN_CONV_W - 1 - kk:GDN_CONV_W - kk, :]
    for j in range(nh):
        cv_refs[j][...] = _exact_rows((r_st == s_st * t + t - nh + j).astype(BF16), x)
    b_scr[0:rows, 0:GDN_CONV_C] = _silu(y)
    b_scr[0:rows, GDN_CONV_C:GDN_CONV_C + LANE] = gdn_ref[:, W_GDN:GDN_COLS]
    a_scr[0:rows, :] = gla_ref[...]

    row = lax.broadcasted_iota(jnp.int32, (tc, tc), 0)
    col = lax.broadcasted_iota(jnp.int32, (tc, tc), 1)
    same = (row // t) == (col // t)
    causal = same & (row >= col)
    strict = same & (row > col)
    eye = (row == col).astype(F32)
    tri = causal.astype(BF16)
    last = (col == (row // t) * t + (t - 1)).astype(BF16)
    lane_seq = lax.broadcasted_iota(jnp.int32, (1, tc), 1) // t
    sub = lax.broadcasted_iota(jnp.int32, (SUBLANE, LANE), 0)

    def group_of(x, s):
        g = s // spg
        return x[g * SUBLANE:(g + 1) * SUBLANE]

    def pick_rows(res, off):
        groups = []
        for g in range(rows // SUBLANE):
            piece = res[g * spg + spg - 1][off:off + SUBLANE]
            for j in reversed(range(spg - 1)):
                piece = jnp.where(sub < (j + 1) * t, res[g * spg + j][off:off + SUBLANE], piece)
            groups.append(piece)
        return jnp.concatenate(groups, axis=0)

    q = a_scr[:, 0:GLA_QK] * (GLA_DK ** -0.5)
    k = a_scr[:, GLA_QK:2 * GLA_QK]
    vb = a_scr[:, 2 * GLA_QK:2 * GLA_QK + GLA_V].astype(BF16)
    b = _exact_rows(tri, a_scr[:, W_GLA:GLA_COLS])
    b_end = _exact_rows(last, b)
    qe_f = q * jnp.exp(b)
    qe = qe_f.astype(BF16)
    ke = (k * jnp.exp(-b)).astype(BF16)
    klt = (k * jnp.exp(b_end - b)).T.astype(BF16)
    bt = b.T
    lane_head = lax.broadcasted_iota(jnp.int32, (1, GLA_QK), 1) // GLA_DK
    qmask = [jnp.where(lane_head == h, qe, jnp.zeros_like(qe)) for h in range(H)]
    scores = [_dot_nt(qmask[h], ke) for h in range(H)]
    scores = [jnp.where(causal, s, 0.0).astype(BF16) for s in scores]
    o_intra = [_dot(scores[h], vb[:, h * GLA_DV:(h + 1) * GLA_DV]) for h in range(H)]
    hs = [jnp.concatenate([sa_ref[s, h] for h in range(H)], axis=0) for s in range(n)]
    lhs_q = [jnp.concatenate([jnp.where(lane_head == h, group_of(qe_f, s), 0.0) for h in range(H)],
                             axis=0).astype(BF16) for s in range(0, n, spg)]
    res = [_dot(lhs_q[s // spg], hs[s].astype(BF16)) for s in range(n)]
    upd = [_dot(jnp.where(lane_seq == s, klt, jnp.zeros_like(klt)), vb) for s in range(n)]
    gna = gna_ref[...]
    for h in range(H):
        sl = slice(h * GLA_DV, (h + 1) * GLA_DV)
        oh = o_intra[h][0:rows] + pick_rows(res, h * SUBLANE)
        gate = gla_ref[:, 2 * GLA_QK + GLA_V + h * GLA_DV:2 * GLA_QK + GLA_V + (h + 1) * GLA_DV]
        oa_ref[:, sl] = (_rms(oh, gna) * _silu(gate)).astype(oa_ref.dtype)
    for s in range(n):
        e_col = jnp.exp(bt[:, s * t + t - 1:s * t + t])
        diag = jnp.concatenate(
            [upd[s][h * GLA_DK:(h + 1) * GLA_DK, h * GLA_DV:(h + 1) * GLA_DV] for h in range(H)], axis=0)
        new = hs[s] * e_col + diag
        for h in range(H):
            sta_ref[s, h] = new[h * GLA_DK:(h + 1) * GLA_DK]

    bg = b_scr[:, GDN_CONV_C:GDN_CONV_C + LANE]
    gc = _exact_rows(tri, bg)
    gl = _exact_rows(last, gc)
    gt = gc.T
    heads = range(H)
    sls = [slice(h * GDN_D, (h + 1) * GDN_D) for h in heads]
    kf, qf, kb, qb, vf, beta, gcol, decay = [], [], [], [], [], [], [], []
    for h in heads:
        qh = b_scr[:, sls[h]]
        kh = b_scr[:, GDN_QK + h * GDN_D:GDN_QK + (h + 1) * GDN_D]
        vf.append(b_scr[:, 2 * GDN_QK + h * GDN_D:2 * GDN_QK + (h + 1) * GDN_D])
        qh = qh * lax.rsqrt(jnp.sum(qh * qh, axis=-1, keepdims=True) + EPS) * (GDN_D ** -0.5)
        kh = kh * lax.rsqrt(jnp.sum(kh * kh, axis=-1, keepdims=True) + EPS)
        kf.append(kh)
        qf.append(qh)
        kb.append(kh.astype(BF16))
        qb.append(qh.astype(BF16))
        beta.append(bg[:, h:h + 1])
        gcol.append(gc[:, H + h:H + h + 1])
        grow = gt[H + h:H + h + 1, :]
        decay.append(jnp.where(causal, jnp.exp(jnp.where(causal, gcol[h] - grow, 0.0)), 0.0))
    qk_kk = [_dot_nt(jnp.concatenate([qb[h], kb[h]], axis=0), kb[h]) for h in heads]
    hsb = [[sb_ref[s, h] for h in heads] for s in range(n)]
    kq_lhs = [[jnp.concatenate([group_of(kf[h], s), group_of(qf[h], s)], axis=0).astype(BF16)
               for s in range(0, n, spg)] for h in heads]
    kq = [[_dot(kq_lhs[h][s // spg], hsb[s][h].astype(BF16)) for s in range(n)]
          for h in heads]
    tinv = _tri_inverse([jnp.where(strict, beta[h] * qk_kk[h][tc:] * decay[h], 0.0) for h in heads], eye, t)
    pad_rows = jnp.zeros((tc - rows, GDN_D), F32)
    eg = [jnp.exp(gcol[h]) for h in heads]
    k_h = [jnp.concatenate([pick_rows(kq[h], 0), pad_rows], axis=0) for h in heads]
    rhs = [beta[h] * (vf[h] - eg[h] * k_h[h]) for h in heads]
    ub = [u.astype(BF16) for u in _dot3_many(tinv, rhs)]
    qku = [_dot((qk_kk[h][:tc] * decay[h]).astype(BF16), ub[h]) for h in heads]
    kdt = [(kf[h] * jnp.exp(gl[:, H + h:H + h + 1] - gcol[h])).T.astype(BF16) for h in heads]
    updb = [[_dot(jnp.where(lane_seq == s, kdt[h], jnp.zeros_like(kdt[h])), ub[h]) for s in range(n)]
            for h in heads]
    gnb = gnb_ref[...]
    for h in heads:
        o = eg[h][0:rows] * pick_rows(kq[h], SUBLANE) + qku[h][0:rows]
        z = gdn_ref[:, GDN_CONV_C + h * GDN_D:GDN_CONV_C + (h + 1) * GDN_D]
        ob_ref[:, sls[h]] = (_rms(o, gnb) * _silu(z)).astype(ob_ref.dtype)
        for s in range(n):
            e_last = jnp.exp(gl[s * t:s * t + 1, H + h:H + h + 1])
            stb_ref[s, h] = e_last * hsb[s][h] + updb[h][s]


def _mix_sample(gla2d, gdn2d, hists, sa, sb, cw, gna, gnb, t):
    n_rows = gla2d.shape[0]
    rows = SEQ_PER_STEP * t
    assert SUBLANE % t == 0 and rows <= LANE and n_rows % rows == 0
    bsz = n_rows // t
    r2 = lambda i: (i, 0)
    i4 = lambda i: (i, 0, 0, 0)
    c2 = lambda i: (0, 0)
    return pl.pallas_call(
        functools.partial(_mix_sample_kernel, t=t),
        grid=(n_rows // rows,),
        in_specs=[
            pl.BlockSpec((rows, GLA_COLS), r2),
            pl.BlockSpec((rows, GDN_COLS), r2),
            pl.BlockSpec((SEQ_PER_STEP, GDN_CONV_C), r2),
            pl.BlockSpec((SEQ_PER_STEP, GDN_CONV_C), r2),
            pl.BlockSpec((SEQ_PER_STEP, GDN_CONV_C), r2),
            pl.BlockSpec((SEQ_PER_STEP, H, GLA_DK, GLA_DV), i4),
            pl.BlockSpec((SEQ_PER_STEP, H, GDN_D, GDN_D), i4),
            pl.BlockSpec((GDN_CONV_W, GDN_CONV_C), c2),
            pl.BlockSpec((1, GLA_DV), c2),
            pl.BlockSpec((1, GDN_D), c2),
        ],
        out_specs=[
            pl.BlockSpec((rows, GLA_V), r2),
            pl.BlockSpec((rows, GDN_V), r2),
            pl.BlockSpec((SEQ_PER_STEP, H, GLA_DK, GLA_DV), i4),
            pl.BlockSpec((SEQ_PER_STEP, H, GDN_D, GDN_D), i4),
        ] + [pl.BlockSpec((SEQ_PER_STEP, GDN_CONV_C), r2)] * (GDN_CONV_W - 1),
        out_shape=[
            jax.ShapeDtypeStruct((n_rows, GLA_V), BF16),
            jax.ShapeDtypeStruct((n_rows, GDN_V), BF16),
            jax.ShapeDtypeStruct((bsz, H, GLA_DK, GLA_DV), F32),
            jax.ShapeDtypeStruct((bsz, H, GDN_D, GDN_D), F32),
        ] + [jax.ShapeDtypeStruct((bsz, GDN_CONV_C), F32)] * (GDN_CONV_W - 1),
        scratch_shapes=[pltpu.VMEM((LANE, GLA_COLS), F32),
                        pltpu.VMEM((LANE, GDN_CONV_C + LANE), F32)],
        compiler_params=pltpu.CompilerParams(
            dimension_semantics=("arbitrary",), vmem_limit_bytes=VMEM_LIMIT),
        name="mix_sample",
    )(gla2d, gdn2d, *hists, sa, sb, cw, gna, gnb)


def _ffn_kernel(*refs, seq_tiles, has_hist, seq_len):
    nh = FFN_CONV_W - 1
    if has_hist:
        (x_ref, oa_ref, ob_ref, wo_ref, g2_ref, wu_ref, cw_ref, wd_ref, gf_ref) = refs[:9]
        st_refs = refs[9:9 + nh]
        y_ref = refs[9 + nh]
        last_refs = refs[10 + nh:10 + 2 * nh]
        u_scr = refs[10 + 2 * nh]
    else:
        (x_ref, oa_ref, ob_ref, wo_ref, g2_ref, wu_ref, cw_ref, wd_ref, gf_ref,
         y_ref, tail_ref, u_scr) = refs
    tm = x_ref.shape[0]

    if not has_hist:
        @pl.when(pl.program_id(0) % seq_tiles == 0)
        def _():
            u_scr[0:SUBLANE, :] = jnp.zeros((SUBLANE, 2 * D_FF), F32)

    x1 = (x_ref[...] + _dot(oa_ref[...].astype(BF16), wo_ref[0:GLA_V, :])
          + _dot(ob_ref[...].astype(BF16), wo_ref[GLA_V:GLA_V + GDN_V, :]))
    hn = _rms(x1, g2_ref[...]).astype(BF16)

    if has_hist:
        nseq = tm // seq_len
        pos = lax.broadcasted_iota(jnp.int32, (tm, FF_CK), 0) % seq_len
        r_ts = lax.broadcasted_iota(jnp.int32, (tm, nseq), 0)
        s_ts = lax.broadcasted_iota(jnp.int32, (tm, nseq), 1)
        spread = [(r_ts == s_ts * seq_len + p).astype(BF16) for p in range(nh)]
        r_st = lax.broadcasted_iota(jnp.int32, (nseq, tm), 1)
        s_st = lax.broadcasted_iota(jnp.int32, (nseq, tm), 0)
        gather = [(r_st == s_st * seq_len + seq_len - nh + j).astype(BF16) for j in range(nh)]
        hist = [_exact_rows(jnp.concatenate(spread[:k], axis=1),
                            jnp.concatenate([st_refs[nh + p - k][...] for p in range(k)], axis=0))
                for k in range(1, nh + 1)]

    def conv(u, off):
        cols = slice(off, off + FF_CK)
        if has_hist:
            m1, m2 = [jnp.where(pos >= k, pltpu.roll(u, k, 0), hist[k - 1][:, cols])
                      for k in range(1, nh + 1)]
            u_scr[:, cols] = u
        else:
            u = u_scr[SUBLANE:SUBLANE + tm, cols]
            m1 = u_scr[SUBLANE - 1:SUBLANE - 1 + tm, cols]
            m2 = u_scr[SUBLANE - 2:SUBLANE - 2 + tm, cols]
            tail = u_scr[tm:tm + SUBLANE, cols]
            tail_ref[0, :, cols] = tail
            u_scr[0:SUBLANE, cols] = tail
        return (m2 * cw_ref[0:1, cols] + m1 * cw_ref[1:2, cols] + u * cw_ref[2:3, cols])

    def up(c):
        ua = _dot(hn, wu_ref[:, c:c + FF_CK])
        ub = _dot(hn, wu_ref[:, D_FF + c:D_FF + c + FF_CK])
        if has_hist:
            return ua, ub
        u_scr[SUBLANE:SUBLANE + tm, c:c + FF_CK] = ua
        u_scr[SUBLANE:SUBLANE + tm, D_FF + c:D_FF + c + FF_CK] = ub
        return None, None

    acc = jnp.zeros((tm, D_MODEL), F32)
    ahead = FF_LOOKAHEAD
    steps = list(range(0, D_FF, FF_CK))
    pending = [up(c) for c in steps[:ahead]]
    for i, c in enumerate(steps):
        if i + ahead < len(steps):
            pending.append(up(steps[i + ahead]))
        ua, ub = pending.pop(0)
        a = conv(ua, c)
        b = conv(ub, D_FF + c)
        act = (_silu(a) * b).astype(BF16)
        acc = acc + _dot(act, wd_ref[c:c + FF_CK, :])
    y_ref[...] = _rms(x1 + acc, gf_ref[...])
    if has_hist:
        for j in range(nh):
            last_refs[j][...] = _exact_rows(gather[j], u_scr[...])


def _ffn(x2d, oa, ob, wo, g2, wu, cw, wd, gf, tm, seq_len, hist=None):
    n = x2d.shape[0]
    has_hist = hist is not None
    row = lambda i: (i, 0)
    const = lambda i: (0, 0)
    in_specs = [
        pl.BlockSpec((tm, D_MODEL), row),
        pl.BlockSpec((tm, GLA_V), row),
        pl.BlockSpec((tm, GDN_V), row),
        pl.BlockSpec((GLA_V + GDN_V, D_MODEL), const),
        pl.BlockSpec((1, D_MODEL), const),
        pl.BlockSpec((D_MODEL, 2 * D_FF), const),
        pl.BlockSpec((FFN_CONV_W, 2 * D_FF), const),
        pl.BlockSpec((D_FF, D_MODEL), const),
        pl.BlockSpec((1, D_MODEL), const),
    ]
    args = [x2d, oa, ob, wo, g2, wu, cw, wd, gf]
    out_specs = [pl.BlockSpec((tm, D_MODEL), row)]
    out_shape = [jax.ShapeDtypeStruct((n, D_MODEL), F32)]
    scratch = []
    if has_hist:
        assert tm % seq_len == 0
        seq_tiles = 1
        nseq = tm // seq_len
        assert len(hist) == FFN_CONV_W - 1 and seq_len >= FFN_CONV_W - 1
        in_specs += [pl.BlockSpec((nseq, 2 * D_FF), row)] * len(hist)
        args += list(hist)
        out_specs += [pl.BlockSpec((nseq, 2 * D_FF), row)] * len(hist)
        out_shape += [jax.ShapeDtypeStruct((n // seq_len, 2 * D_FF), F32)] * len(hist)
        scratch.append(pltpu.VMEM((tm, 2 * D_FF), F32))
    else:
        assert seq_len % tm == 0
        seq_tiles = seq_len // tm
        out_specs.append(pl.BlockSpec((1, SUBLANE, 2 * D_FF), lambda i: (i, 0, 0)))
        out_shape.append(jax.ShapeDtypeStruct((n // tm, SUBLANE, 2 * D_FF), F32))
        scratch.append(pltpu.VMEM((SUBLANE + tm, 2 * D_FF), F32))
    return pl.pallas_call(
        functools.partial(_ffn_kernel, seq_tiles=seq_tiles, has_hist=has_hist, seq_len=seq_len),
        grid=(n // tm,),
        in_specs=in_specs,
        out_specs=out_specs,
        out_shape=out_shape,
        scratch_shapes=scratch,
        compiler_params=pltpu.CompilerParams(
            dimension_semantics=("arbitrary",), vmem_limit_bytes=VMEM_LIMIT),
        name="ffn_sample" if has_hist else "ffn_prompt",
    )(*args)


def _pad_cols(a, width):
    return jnp.pad(a, ((0, 0), (0, width - a.shape[1])))


def kernel(x_prompt, x_sample, state_gla, state_gdn, state_gdn_conv, state_ffn_conv, norm1_g, w_in, gla_w_a2, gla_b_a, gla_norm_g, gdn_conv_w, gdn_a_log, gdn_dt_bias, gdn_norm_g, w_out, norm2_g, w_up, ffn_conv_w, w_down, norm_f_g):
    assert w_in.shape[0] == 1, "single layer"
    bp, tp, _ = x_prompt.shape
    bs, ts, _ = x_sample.shape

    offs = [0]
    for s in IN_SIZES:
        offs.append(offs[-1] + s)
    wi = w_in[0]
    w1 = (wi[:, offs[0]:offs[4]].astype(BF16),
          wi[:, offs[5]:offs[7]].astype(BF16),
          jnp.concatenate([_pad_cols(wi[:, offs[4]:offs[5]], LANE),
                           _pad_cols(wi[:, offs[7]:offs[9]], LANE)], axis=1).astype(BF16))
    wa2 = jnp.pad(gla_w_a2[0], ((0, LANE - GLA_RANK), (0, 0))).astype(BF16)
    ba = gla_b_a[0][None, :]
    alog = jnp.pad(gdn_a_log[0], (H, LANE - 2 * H))[None, :]
    dtb = jnp.pad(gdn_dt_bias[0], (H, LANE - 2 * H))[None, :]
    g1 = norm1_g[0][None, :]
    g2 = norm2_g[0][None, :]
    gf = norm_f_g[None, :]
    gna = gla_norm_g[0][None, :]
    gnb = gdn_norm_g[0][None, :]
    cwb = gdn_conv_w[0]
    cwf = ffn_conv_w[0]
    wo = w_out[0].astype(BF16)
    wu = w_up[0].astype(BF16)
    wd = w_down[0].astype(BF16)

    xp = x_prompt.reshape(bp * tp, D_MODEL)
    tm_p = 512
    gla_p, gdn_p, qkv_tail = _inproj(xp, g1, w1, wa2, ba, alog, dtb, tm=tm_p, conv_w=cwb, seq_len=tp)
    oa_p, p_gla = _gla_prompt(gla_p, gna, bp, tp)
    ob_p, p_gdn = _gdn_prompt(gdn_p, gnb, bp, tp)
    y_p, ffn_tail = _ffn(xp, oa_p, ob_p, wo, g2, wu, cwf, wd, gf, tm=tm_p, seq_len=tp)
    y_prompt = y_p.reshape(bp, tp, D_MODEL)
    p_conv = qkv_tail.reshape(bp, tp // tm_p, SUBLANE, GDN_CONV_C)[:, -1, SUBLANE - (GDN_CONV_W - 1):, :]
    p_ffn = ffn_tail.reshape(bp, tp // tm_p, SUBLANE, 2 * D_FF)[:, -1, SUBLANE - (FFN_CONV_W - 1):, :]

    xs = x_sample.reshape(bs * ts, D_MODEL)
    gla_s, gdn_s = _inproj(xs, g1, w1, wa2, ba, alog, dtb, tm=256)
    oa_s, ob_s, s_gla, s_gdn, *conv_planes = _mix_sample(
        gla_s, gdn_s, [state_gdn_conv[0][:, j] for j in range(GDN_CONV_W - 1)],
        state_gla[0], state_gdn[0], cwb, gna, gnb, ts)
    y_s, *ffn_planes = _ffn(xs, oa_s, ob_s, wo, g2, wu, cwf, wd, gf, tm=128, seq_len=ts,
                            hist=[state_ffn_conv[0][:, j] for j in range(FFN_CONV_W - 1)])
    y_sample = y_s.reshape(bs, ts, D_MODEL)
    s_conv = jnp.stack(conv_planes, axis=1)
    s_ffn = jnp.stack(ffn_planes, axis=1)

    return (y_prompt, y_sample, p_gla[None], p_gdn[None], p_conv[None], p_ffn[None],
            s_gla[None], s_gdn[None], s_conv[None], s_ffn[None])
```

```python
import functools

import jax
import jax.numpy as jnp
from jax import lax
from jax.experimental import pallas as pl
from jax.experimental.pallas import tpu as pltpu

F32 = jnp.float32
BF16 = jnp.bfloat16

D_MODEL = 1024
H = 4
GLA_DK = 64
GLA_DV = 128
GLA_RANK = 16
GLA_TAU = 16.0
GDN_D = 128
GDN_CONV_W = 4
D_FF = 2816
FFN_CONV_W = 3
EPS = 1e-6

GLA_QK = H * GLA_DK
GLA_V = H * GLA_DV
GDN_QK = H * GDN_D
GDN_V = H * GDN_D
GDN_CONV_C = 2 * GDN_QK + GDN_V
IN_SIZES = (GLA_QK, GLA_QK, GLA_V, GLA_V, GLA_RANK, GDN_CONV_C, GDN_V, H, H)

LANE = 128
SUBLANE = 8
VMEM_LIMIT = 56 * 1024 * 1024

W_GLA = 2 * GLA_QK + 2 * GLA_V
W_GDN = GDN_CONV_C + GDN_V
GLA_COLS = W_GLA + GLA_QK
GDN_COLS = W_GDN + LANE

PROJ_CK = 512
CHUNK = 128
CHUNKS_PER_STEP = 8
FF_CK = 256
FF_LOOKAHEAD = 11


def _dot(a, b):
    return jnp.dot(a, b, preferred_element_type=F32)


def _dot_nt(a, b):
    return lax.dot_general(a, b, (((1,), (1,)), ((), ())), preferred_element_type=F32)


def _split2(x):
    hi = x.astype(BF16)
    lo = (x - hi.astype(F32)).astype(BF16)
    return hi, lo


def _dot3(a, b):
    ah, al = _split2(a)
    bh, bl = _split2(b)
    return _dot(jnp.concatenate([ah, ah, al], axis=1), jnp.concatenate([bh, bl, bh], axis=0))


def _dot3_many(a_list, b_list):
    lhs = [jnp.concatenate([ah, ah, al], axis=1) for ah, al in map(_split2, a_list)]
    rhs = [jnp.concatenate([bh, bl, bh], axis=0) for bh, bl in map(_split2, b_list)]
    return [_dot(x, y) for x, y in zip(lhs, rhs)]


def _exact_rows(sel_bf16, x):
    h1 = x.astype(BF16)
    r1 = x - h1.astype(F32)
    h2 = r1.astype(BF16)
    h3 = (r1 - h2.astype(F32)).astype(BF16)
    return _dot(jnp.concatenate([sel_bf16] * 3, axis=1), jnp.concatenate([h1, h2, h3], axis=0))


def _sigmoid(x):
    return 0.5 * jnp.tanh(0.5 * x) + 0.5


def _silu(x):
    return x * _sigmoid(x)


def _softplus(x):
    return jnp.maximum(x, 0.0) + jnp.log(1.0 + jnp.exp(-jnp.abs(x)))


def _rms(x, g):
    ms = jnp.mean(x * x, axis=-1, keepdims=True)
    return x * lax.rsqrt(ms + EPS) * g


def _gdn_qkv_activation(y, off):
    a = _silu(y)
    outs = []
    for j in range(0, a.shape[1], GDN_D):
        blk = a[:, j:j + GDN_D]
        if off + j < 2 * GDN_QK:
            blk = blk * lax.rsqrt(jnp.sum(blk * blk, axis=-1, keepdims=True) + EPS)
            if off + j < GDN_QK:
                blk = blk * (GDN_D ** -0.5)
        outs.append(blk)
    return jnp.concatenate(outs, axis=1)


def _inproj_kernel(*refs, seq_tiles):
    if seq_tiles is None:
        (x_ref, g1_ref, wa_ref, wb_ref, ws_ref, wa2_ref, ba_ref, alog_ref, dtb_ref,
         gla_ref, gdn_ref) = refs
    else:
        (x_ref, g1_ref, win_ref, wa2_ref, ba_ref, alog_ref, dtb_ref, cw_ref,
         gla_ref, gdn_ref, tail_ref, wa_ref, wb_ref, ws_ref, xs_scr) = refs

        @pl.when(pl.program_id(0) == 0)
        def _():
            offs = [sum(IN_SIZES[:i]) for i in range(len(IN_SIZES) + 1)]
            wa_ref[...] = win_ref[:, offs[0]:offs[4]].astype(BF16)
            wb_ref[...] = win_ref[:, offs[5]:offs[7]].astype(BF16)
            ws_ref[...] = jnp.zeros_like(ws_ref)
            ws_ref[:, 0:GLA_RANK] = win_ref[:, offs[4]:offs[5]].astype(BF16)
            ws_ref[:, LANE:LANE + 2 * H] = win_ref[:, offs[7]:offs[9]].astype(BF16)
    tm = x_ref.shape[0]
    hn = _rms(x_ref[...], g1_ref[...]).astype(BF16)
    small = _dot(hn, ws_ref[...])
    ck = PROJ_CK
    if seq_tiles is None:
        for c in range(0, W_GLA, ck):
            gla_ref[:, c:c + ck] = _dot(hn, wa_ref[:, c:c + ck])
        for c in range(0, W_GDN, ck):
            gdn_ref[:, c:c + ck] = _dot(hn, wb_ref[:, c:c + ck])
    else:
        @pl.when(pl.program_id(0) % seq_tiles == 0)
        def _():
            xs_scr[0:SUBLANE, :] = jnp.zeros((SUBLANE, GDN_CONV_C), F32)

        for c in range(0, GDN_CONV_C, ck):
            xs_scr[SUBLANE:SUBLANE + tm, c:c + ck] = _dot(hn, wb_ref[:, c:c + ck])
        gdn_ref[:, GDN_CONV_C:W_GDN] = _dot(hn, wb_ref[:, GDN_CONV_C:W_GDN])
        for c in range(0, GDN_CONV_C, ck):
            gla_ref[:, c:c + ck] = _dot(hn, wa_ref[:, c:c + ck])
            cols = slice(c, c + ck)
            y = xs_scr[SUBLANE:SUBLANE + tm, cols] * cw_ref[GDN_CONV_W - 1:GDN_CONV_W, cols]
            for kk in range(1, GDN_CONV_W):
                y = y + (xs_scr[SUBLANE - kk:SUBLANE - kk + tm, cols]
                         * cw_ref[GDN_CONV_W - 1 - kk:GDN_CONV_W - kk, cols])
            gdn_ref[:, cols] = _gdn_qkv_activation(y, c)
            tail = xs_scr[tm:tm + SUBLANE, cols]
            tail_ref[0, :, cols] = tail
            xs_scr[0:SUBLANE, cols] = tail
    xa = _dot(small[:, 0:LANE].astype(BF16), wa2_ref[...]) + ba_ref[...]
    gla_ref[:, W_GLA:GLA_COLS] = -_softplus(-xa) * (1.0 / GLA_TAU)
    bd = small[:, LANE:2 * LANE]
    lane = lax.broadcasted_iota(jnp.int32, bd.shape, 1)
    beta = _sigmoid(bd)
    g = -jnp.exp(alog_ref[...]) * _softplus(bd + dtb_ref[...])
    gdn_ref[:, W_GDN:GDN_COLS] = jnp.where(lane < H, beta, jnp.where(lane < 2 * H, g, 0.0))


def _inproj(x2d, g1, w_in, wa2, ba, alog, dtb, tm, conv_w=None, seq_len=None):
    n = x2d.shape[0]
    const = lambda i: (0, 0)
    fused_conv = conv_w is not None
    packed_specs = [
        pl.BlockSpec((D_MODEL, W_GLA), const),
        pl.BlockSpec((D_MODEL, W_GDN), const),
        pl.BlockSpec((D_MODEL, 2 * LANE), const),
    ]
    weight_specs = [pl.BlockSpec(w_in.shape, const)] if fused_conv else packed_specs
    in_specs = [
        pl.BlockSpec((tm, D_MODEL), lambda i: (i, 0)),
        pl.BlockSpec((1, D_MODEL), const),
        *weight_specs,
        pl.BlockSpec((LANE, GLA_QK), const),
        pl.BlockSpec((1, GLA_QK), const),
        pl.BlockSpec((1, LANE), const),
        pl.BlockSpec((1, LANE), const),
    ]
    args = [x2d, g1, *([w_in] if fused_conv else w_in), wa2, ba, alog, dtb]
    out_specs = [
        pl.BlockSpec((tm, GLA_COLS), lambda i: (i, 0)),
        pl.BlockSpec((tm, GDN_COLS), lambda i: (i, 0)),
    ]
    out_shape = [
        jax.ShapeDtypeStruct((n, GLA_COLS), F32),
        jax.ShapeDtypeStruct((n, GDN_COLS), F32),
    ]
    scratch = []
    seq_tiles = None
    if fused_conv:
        assert seq_len % tm == 0
        seq_tiles = seq_len // tm
        in_specs.append(pl.BlockSpec((GDN_CONV_W, GDN_CONV_C), const))
        args.append(conv_w)
        out_specs.append(pl.BlockSpec((1, SUBLANE, GDN_CONV_C), lambda i: (i, 0, 0)))
        out_shape.append(jax.ShapeDtypeStruct((n // tm, SUBLANE, GDN_CONV_C), F32))
        out_specs += packed_specs
        out_shape += [jax.ShapeDtypeStruct((D_MODEL, w), BF16) for w in (W_GLA, W_GDN, 2 * LANE)]
        scratch.append(pltpu.VMEM((SUBLANE + tm, GDN_CONV_C), F32))
    return pl.pallas_call(
        functools.partial(_inproj_kernel, seq_tiles=seq_tiles),
        grid=(n // tm,),
        in_specs=in_specs,
        out_specs=out_specs,
        out_shape=out_shape,
        scratch_shapes=scratch,
        compiler_params=pltpu.CompilerParams(
            dimension_semantics=("arbitrary" if fused_conv else "parallel",),
            vmem_limit_bytes=VMEM_LIMIT),
        name="inproj_conv" if fused_conv else "inproj",
    )(*args)


def _gla_chunk_kernel(gla_ref, gn_ref, o_ref, st_ref, h_scr):
    tc = CHUNK

    @pl.when(pl.program_id(1) == 0)
    def _():
        h_scr[...] = jnp.zeros_like(h_scr)

    row = lax.broadcasted_iota(jnp.int32, (tc, tc), 0)
    col = lax.broadcasted_iota(jnp.int32, (tc, tc), 1)
    causal = row >= col
    tri = causal.astype(BF16)
    lane_head = lax.broadcasted_iota(jnp.int32, (1, GLA_QK), 1) // GLA_DK
    rblk = lax.broadcasted_iota(jnp.int32, (GLA_QK, GLA_V), 0) // GLA_DK
    cblk = lax.broadcasted_iota(jnp.int32, (GLA_QK, GLA_V), 1) // GLA_DV
    gn = gn_ref[...]
    hbd = h_scr[...]

    chunks = range(CHUNKS_PER_STEP)
    rows = [slice(j * tc, (j + 1) * tc) for j in chunks]
    b = [_exact_rows(tri, gla_ref[rs, W_GLA:GLA_COLS]) for rs in rows]
    qi, qm, km, vb, klt, e_last = [], [], [], [], [], []
    for j in chunks:
        q = gla_ref[rows[j], 0:GLA_QK] * (GLA_DK ** -0.5)
        k = gla_ref[rows[j], GLA_QK:2 * GLA_QK]
        vb.append(gla_ref[rows[j], 2 * GLA_QK:2 * GLA_QK + GLA_V].astype(BF16))
        b_mid = b[j][tc // 2 - 1:tc // 2, :]
        qm.append((q * jnp.exp(b[j] - b_mid)).astype(BF16))
        km.append((k * jnp.exp(b_mid - b[j])).astype(BF16))
        qi.append((q * jnp.exp(b[j])).astype(BF16))
        bt = b[j].T
        b_last = bt[:, tc - 1:tc]
        klt.append((k.T * jnp.exp(b_last - bt)).astype(BF16))
        e_last.append(jnp.exp(b_last))
    scores = {(j, h): _dot_nt(jnp.where(lane_head == h, qm[j], jnp.zeros_like(qm[j])), km[j])
              for j in chunks for h in range(H)}
    scores = {p: jnp.where(causal, s, 0.0).astype(BF16) for p, s in scores.items()}
    o_intra = {(j, h): _dot(scores[j, h], vb[j][:, h * GLA_DV:(h + 1) * GLA_DV])
               for j in chunks for h in range(H)}
    upd = [jnp.where(rblk == cblk, _dot(klt[j], vb[j]), 0.0) for j in chunks]

    for j in chunks:
        rs = rows[j]
        o_inter = _dot(qi[j], hbd.astype(BF16))
        hbd = hbd * e_last[j] + upd[j]
        for h in range(H):
            sl = slice(h * GLA_DV, (h + 1) * GLA_DV)
            oh = o_intra[j, h] + o_inter[:, sl]
            gate = gla_ref[rs, 2 * GLA_QK + GLA_V + h * GLA_DV:2 * GLA_QK + GLA_V + (h + 1) * GLA_DV]
            o_ref[rs, sl] = (_rms(oh, gn) * _silu(gate)).astype(o_ref.dtype)

    h_scr[...] = hbd
    for h in range(H):
        st_ref[0, h] = hbd[h * GLA_DK:(h + 1) * GLA_DK, h * GLA_DV:(h + 1) * GLA_DV]


def _gla_prompt(gla2d, gn, bsz, t):
    step = CHUNKS_PER_STEP * CHUNK
    assert t % step == 0
    nc = t // step
    return pl.pallas_call(
        _gla_chunk_kernel,
        grid=(bsz, nc),
        in_specs=[
            pl.BlockSpec((step, GLA_COLS), lambda b, c: (b * nc + c, 0)),
            pl.BlockSpec((1, GLA_DV), lambda b, c: (0, 0)),
        ],
        out_specs=[
            pl.BlockSpec((step, GLA_V), lambda b, c: (b * nc + c, 0)),
            pl.BlockSpec((1, H, GLA_DK, GLA_DV), lambda b, c: (b, 0, 0, 0)),
        ],
        out_shape=[
            jax.ShapeDtypeStruct((bsz * t, GLA_V), BF16),
            jax.ShapeDtypeStruct((bsz, H, GLA_DK, GLA_DV), F32),
        ],
        scratch_shapes=[pltpu.VMEM((GLA_QK, GLA_V), F32)],
        compiler_params=pltpu.CompilerParams(
            dimension_semantics=("parallel", "arbitrary"), vmem_limit_bytes=VMEM_LIMIT),
        name="gla_prompt",
    )(gla2d, gn)


def _tri_inverse(l_strict, eye, order):
    n = eye.shape[0]
    refine = order > 2 * SUBLANE

    def dots(a_list, b_list):
        if not refine:
            return _dot3_many(a_list, b_list)
        a_list = [a.astype(BF16) for a in a_list]
        b_list = [b.astype(BF16) for b in b_list]
        return [_dot(a, b) for a, b in zip(a_list, b_list)]

    m = [-l for l in l_strict]
    p = [eye + x for x in m]
    m = dots(m, m)
    power = 2
    while 2 * power < order:
        pm = dots([jnp.concatenate([a, b], axis=0) for a, b in zip(p, m)], m)
        p = [a + x[:n] for a, x in zip(p, pm)]
        m = [x[n:] for x in pm]
        power *= 2
    t = [a + x for a, x in zip(p, dots(p, m))]
    if refine:
        r = [eye - a - x for a, x in zip(t, _dot3_many(l_strict, t))]
        t = [a + x for a, x in zip(t, dots(t, r))]
    return t


def _gdn_chunk_kernel(gdn_ref, gn_ref, o_ref, st_ref, h_scr):
    tc = CHUNK

    @pl.when(pl.program_id(1) == 0)
    def _():
        h_scr[...] = jnp.zeros_like(h_scr)

    row = lax.broadcasted_iota(jnp.int32, (tc, tc), 0)
    col = lax.broadcasted_iota(jnp.int32, (tc, tc), 1)
    causal = row >= col
    strict = row > col
    eye = (row == col).astype(F32)
    tri = causal.astype(BF16)
    gn = gn_ref[...]
    heads = range(H)
    sls = [slice(h * GDN_D, (h + 1) * GDN_D) for h in heads]
    hh = [h_scr[sls[h], :] for h in heads]

    probs = [(j, h) for j in range(CHUNKS_PER_STEP) for h in heads]
    rows = [slice(j * tc, (j + 1) * tc) for j in range(CHUNKS_PER_STEP)]
    bg = [gdn_ref[rs, W_GDN:GDN_COLS] for rs in rows]
    gc = [_exact_rows(tri, x) for x in bg]
    gt = [x.T for x in gc]
    kf, kb, qb, beta, gcol, decay = {}, {}, {}, {}, {}, {}
    for j, h in probs:
        kf[j, h] = gdn_ref[rows[j], GDN_QK + h * GDN_D:GDN_QK + (h + 1) * GDN_D]
        kb[j, h] = kf[j, h].astype(BF16)
        qb[j, h] = gdn_ref[rows[j], sls[h]].astype(BF16)
        beta[j, h] = bg[j][:, h:h + 1]
        gcol[j, h] = gc[j][:, H + h:H + h + 1]
        grow = gt[j][H + h:H + h + 1, :]
        decay[j, h] = jnp.where(causal, jnp.exp(jnp.where(causal, gcol[j, h] - grow, 0.0)), 0.0)
    qk_kk = {p: _dot_nt(jnp.concatenate([qb[p], kb[p]], axis=0), kb[p]) for p in probs}
    tinv = dict(zip(probs, _tri_inverse(
        [jnp.where(strict, beta[p] * qk_kk[p][tc:] * decay[p], 0.0) for p in probs], eye, tc)))

    for j in range(CHUNKS_PER_STEP):
        rs = rows[j]
        kq_h = [_dot(jnp.concatenate([kb[j, h], qb[j, h]], axis=0), hh[h].astype(BF16))
                for h in heads]
        eg = [jnp.exp(gcol[j, h]) for h in heads]
        rhs = [beta[j, h] * (gdn_ref[rs, 2 * GDN_QK + h * GDN_D:2 * GDN_QK + (h + 1) * GDN_D]
                             - eg[h] * kq_h[h][:tc]) for h in heads]
        ub = [u.astype(BF16) for u in _dot3_many([tinv[j, h] for h in heads], rhs)]
        ou = []
        for h in heads:
            qk = (qk_kk[j, h][:tc] * decay[j, h]).astype(BF16)
            g_last = gcol[j, h][tc - 1:tc, :]
            kd = (kf[j, h] * jnp.exp(g_last - gcol[j, h])).T.astype(BF16)
            ou.append(_dot(jnp.concatenate([qk, kd], axis=0), ub[h]))
        for h in heads:
            o = eg[h] * kq_h[h][tc:] + ou[h][:tc]
            hh[h] = jnp.exp(gcol[j, h][tc - 1:tc, :]) * hh[h] + ou[h][tc:]
            z = gdn_ref[rs, GDN_CONV_C + h * GDN_D:GDN_CONV_C + (h + 1) * GDN_D]
            o_ref[rs, sls[h]] = (_rms(o, gn) * _silu(z)).astype(o_ref.dtype)

    for h in heads:
        h_scr[sls[h], :] = hh[h]
        st_ref[0, h] = hh[h]


def _gdn_prompt(gdn2d, gn, bsz, t):
    step = CHUNKS_PER_STEP * CHUNK
    assert t % step == 0
    nc = t // step
    return pl.pallas_call(
        _gdn_chunk_kernel,
        grid=(bsz, nc),
        in_specs=[
            pl.BlockSpec((step, GDN_COLS), lambda b, c: (b * nc + c, 0)),
            pl.BlockSpec((1, GDN_D), lambda b, c: (0, 0)),
        ],
        out_specs=[
            pl.BlockSpec((step, GDN_V), lambda b, c: (b * nc + c, 0)),
            pl.BlockSpec((1, H, GDN_D, GDN_D), lambda b, c: (b, 0, 0, 0)),
        ],
        out_shape=[
            jax.ShapeDtypeStruct((bsz * t, GDN_V), BF16),
            jax.ShapeDtypeStruct((bsz, H, GDN_D, GDN_D), F32),
        ],
        scratch_shapes=[pltpu.VMEM((H * GDN_D, GDN_D), F32)],
        compiler_params=pltpu.CompilerParams(
            dimension_semantics=("parallel", "arbitrary"), vmem_limit_bytes=VMEM_LIMIT),
        name="gdn_prompt",
    )(gdn2d, gn)


SEQ_PER_STEP = 16


def _mix_sample_kernel(gla_ref, gdn_ref, h1_ref, h2_ref, h3_ref, sa_ref, sb_ref, cw_ref, gna_ref, gnb_ref,
                       oa_ref, ob_ref, sta_ref, stb_ref, cv1_ref, cv2_ref, cv3_ref, a_scr, b_scr, *, t):
    n = SEQ_PER_STEP
    rows = n * t
    tc = LANE
    spg = SUBLANE // t

    @pl.when(pl.program_id(0) == 0)
    def _():
        a_scr[...] = jnp.zeros_like(a_scr)
        b_scr[...] = jnp.zeros_like(b_scr)

    nh = GDN_CONV_W - 1
    st_refs = (h1_ref, h2_ref, h3_ref)
    cv_refs = (cv1_ref, cv2_ref, cv3_ref)
    x = gdn_ref[:, 0:GDN_CONV_C]
    pos = lax.broadcasted_iota(jnp.int32, x.shape, 0) % t
    r_ts = lax.broadcasted_iota(jnp.int32, (rows, n), 0)
    s_ts = lax.broadcasted_iota(jnp.int32, (rows, n), 1)
    spread = [(r_ts == s_ts * t + p).astype(BF16) for p in range(nh)]
    r_st = lax.broadcasted_iota(jnp.int32, (n, rows), 1)
    s_st = lax.broadcasted_iota(jnp.int32, (n, rows), 0)
    y = x * cw_ref[GDN_CONV_W - 1:GDN_CONV_W, :]
    for kk in range(1, GDN_CONV_W):
        hist = _exact_rows(jnp.concatenate(spread[:kk], axis=1),
                           jnp.concatenate([st_refs[nh + p - kk][...] for p in range(kk)], axis=0))
        m = jnp.where(pos >= kk, pltpu.roll(x, kk, 0), hist)
        y = y + m * cw_ref[GDN_CONV_W - 1 - kk:GDN_CONV_W - kk, :]
    for j in range(nh):
        cv_refs[j][...] = _exact_rows((r_st == s_st * t + t - nh + j).astype(BF16), x)
    b_scr[0:rows, 0:GDN_CONV_C] = _silu(y)
    b_scr[0:rows, GDN_CONV_C:GDN_CONV_C + LANE] = gdn_ref[:, W_GDN:GDN_COLS]
    a_scr[0:rows, :] = gla_ref[...]

    row = lax.broadcasted_iota(jnp.int32, (tc, tc), 0)
    col = lax.broadcasted_iota(jnp.int32, (tc, tc), 1)
    same = (row // t) == (col // t)
    causal = same & (row >= col)
    strict = same & (row > col)
    eye = (row == col).astype(F32)
    tri = causal.astype(BF16)
    last = (col == (row // t) * t + (t - 1)).astype(BF16)
    lane_seq = lax.broadcasted_iota(jnp.int32, (1, tc), 1) // t
    sub = lax.broadcasted_iota(jnp.int32, (SUBLANE, LANE), 0)

    def group_of(x, s):
        g = s // spg
        return x[g * SUBLANE:(g + 1) * SUBLANE]

    def pick_rows(res, off):
        groups = []
        for g in range(rows // SUBLANE):
            piece = res[g * spg + spg - 1][off:off + SUBLANE]
            for j in reversed(range(spg - 1)):
                piece = jnp.where(sub < (j + 1) * t, res[g * spg + j][off:off + SUBLANE], piece)
            groups.append(piece)
        return jnp.concatenate(groups, axis=0)

    q = a_scr[:, 0:GLA_QK] * (GLA_DK ** -0.5)
    k = a_scr[:, GLA_QK:2 * GLA_QK]
    vb = a_scr[:, 2 * GLA_QK:2 * GLA_QK + GLA_V].astype(BF16)
    b = _exact_rows(tri, a_scr[:, W_GLA:GLA_COLS])
    b_end = _exact_rows(last, b)
    qe_f = q * jnp.exp(b)
    qe = qe_f.astype(BF16)
    ke = (k * jnp.exp(-b)).astype(BF16)
    klt = (k * jnp.exp(b_end - b)).T.astype(BF16)
    bt = b.T
    lane_head = lax.broadcasted_iota(jnp.int32, (1, GLA_QK), 1) // GLA_DK
    qmask = [jnp.where(lane_head == h, qe, jnp.zeros_like(qe)) for h in range(H)]
    scores = [_dot_nt(qmask[h], ke) for h in range(H)]
    scores = [jnp.where(causal, s, 0.0).astype(BF16) for s in scores]
    o_intra = [_dot(scores[h], vb[:, h * GLA_DV:(h + 1) * GLA_DV]) for h in range(H)]
    hs = [jnp.concatenate([sa_ref[s, h] for h in range(H)], axis=0) for s in range(n)]
    lhs_q = [jnp.concatenate([jnp.where(lane_head == h, group_of(qe_f, s), 0.0) for h in range(H)],
                             axis=0).astype(BF16) for s in range(0, n, spg)]
    res = [_dot(lhs_q[s // spg], hs[s].astype(BF16)) for s in range(n)]
    upd = [_dot(jnp.where(lane_seq == s, klt, jnp.zeros_like(klt)), vb) for s in range(n)]
    gna = gna_ref[...]
    for h in range(H):
        sl = slice(h * GLA_DV, (h + 1) * GLA_DV)
        oh = o_intra[h][0:rows] + pick_rows(res, h * SUBLANE)
        gate = gla_ref[:, 2 * GLA_QK + GLA_V + h * GLA_DV:2 * GLA_QK + GLA_V + (h + 1) * GLA_DV]
        oa_ref[:, sl] = (_rms(oh, gna) * _silu(gate)).astype(oa_ref.dtype)
    for s in range(n):
        e_col = jnp.exp(bt[:, s * t + t - 1:s * t + t])
        diag = jnp.concatenate(
            [upd[s][h * GLA_DK:(h + 1) * GLA_DK, h * GLA_DV:(h + 1) * GLA_DV] for h in range(H)], axis=0)
        new = hs[s] * e_col + diag
        for h in range(H):
            sta_ref[s, h] = new[h * GLA_DK:(h + 1) * GLA_DK]

    bg = b_scr[:, GDN_CONV_C:GDN_CONV_C + LANE]
    gc = _exact_rows(tri, bg)
    gl = _exact_rows(last, gc)
    gt = gc.T
    heads = range(H)
    sls = [slice(h * GDN_D, (h + 1) * GDN_D) for h in heads]
    kf, qf, kb, qb, vf, beta, gcol, decay = [], [], [], [], [], [], [], []
    for h in heads:
        qh = b_scr[:, sls[h]]
        kh = b_scr[:, GDN_QK + h * GDN_D:GDN_QK + (h + 1) * GDN_D]
        vf.append(b_scr[:, 2 * GDN_QK + h * GDN_D:2 * GDN_QK + (h + 1) * GDN_D])
        qh = qh * lax.rsqrt(jnp.sum(qh * qh, axis=-1, keepdims=True) + EPS) * (GDN_D ** -0.5)
        kh = kh * lax.rsqrt(jnp.sum(kh * kh, axis=-1, keepdims=True) + EPS)
        kf.append(kh)
        qf.append(qh)
        kb.append(kh.astype(BF16))
        qb.append(qh.astype(BF16))
        beta.append(bg[:, h:h + 1])
        gcol.append(gc[:, H + h:H + h + 1])
        grow = gt[H + h:H + h + 1, :]
        decay.append(jnp.where(causal, jnp.exp(jnp.where(causal, gcol[h] - grow, 0.0)), 0.0))
    qk_kk = [_dot_nt(jnp.concatenate([qb[h], kb[h]], axis=0), kb[h]) for h in heads]
    hsb = [[sb_ref[s, h] for h in heads] for s in range(n)]
    kq_lhs = [[jnp.concatenate([group_of(kf[h], s), group_of(qf[h], s)], axis=0).astype(BF16)
               for s in range(0, n, spg)] for h in heads]
    kq = [[_dot(kq_lhs[h][s // spg], hsb[s][h].astype(BF16)) for s in range(n)]
          for h in heads]
    tinv = _tri_inverse([jnp.where(strict, beta[h] * qk_kk[h][tc:] * decay[h], 0.0) for h in heads], eye, t)
    pad_rows = jnp.zeros((tc - rows, GDN_D), F32)
    eg = [jnp.exp(gcol[h]) for h in heads]
    k_h = [jnp.concatenate([pick_rows(kq[h], 0), pad_rows], axis=0) for h in heads]
    rhs = [beta[h] * (vf[h] - eg[h] * k_h[h]) for h in heads]
    ub = [u.astype(BF16) for u in _dot3_many(tinv, rhs)]
    qku = [_dot((qk_kk[h][:tc] * decay[h]).astype(BF16), ub[h]) for h in heads]
    kdt = [(kf[h] * jnp.exp(gl[:, H + h:H + h + 1] - gcol[h])).T.astype(BF16) for h in heads]
    updb = [[_dot(jnp.where(lane_seq == s, kdt[h], jnp.zeros_like(kdt[h])), ub[h]) for s in range(n)]
            for h in heads]
    gnb = gnb_ref[...]
    for h in heads:
        o = eg[h][0:rows] * pick_rows(kq[h], SUBLANE) + qku[h][0:rows]
        z = gdn_ref[:, GDN_CONV_C + h * GDN_D:GDN_CONV_C + (h + 1) * GDN_D]
        ob_ref[:, sls[h]] = (_rms(o, gnb) * _silu(z)).astype(ob_ref.dtype)
        for s in range(n):
            e_last = jnp.exp(gl[s * t:s * t + 1, H + h:H + h + 1])
            stb_ref[s, h] = e_last * hsb[s][h] + updb[h][s]


def _mix_sample(gla2d, gdn2d, hists, sa, sb, cw, gna, gnb, t):
    n_rows = gla2d.shape[0]
    rows = SEQ_PER_STEP * t
    assert SUBLANE % t == 0 and rows <= LANE and n_rows % rows == 0
    bsz = n_rows // t
    r2 = lambda i: (i, 0)
    i4 = lambda i: (i, 0, 0, 0)
    c2 = lambda i: (0, 0)
    return pl.pallas_call(
        functools.partial(_mix_sample_kernel, t=t),
        grid=(n_rows // rows,),
        in_specs=[
            pl.BlockSpec((rows, GLA_COLS), r2),
            pl.BlockSpec((rows, GDN_COLS), r2),
            pl.BlockSpec((SEQ_PER_STEP, GDN_CONV_C), r2),
            pl.BlockSpec((SEQ_PER_STEP, GDN_CONV_C), r2),
            pl.BlockSpec((SEQ_PER_STEP, GDN_CONV_C), r2),
            pl.BlockSpec((SEQ_PER_STEP, H, GLA_DK, GLA_DV), i4),
            pl.BlockSpec((SEQ_PER_STEP, H, GDN_D, GDN_D), i4),
            pl.BlockSpec((GDN_CONV_W, GDN_CONV_C), c2),
            pl.BlockSpec((1, GLA_DV), c2),
            pl.BlockSpec((1, GDN_D), c2),
        ],
        out_specs=[
            pl.BlockSpec((rows, GLA_V), r2),
            pl.BlockSpec((rows, GDN_V), r2),
            pl.BlockSpec((SEQ_PER_STEP, H, GLA_DK, GLA_DV), i4),
            pl.BlockSpec((SEQ_PER_STEP, H, GDN_D, GDN_D), i4),
        ] + [pl.BlockSpec((SEQ_PER_STEP, GDN_CONV_C), r2)] * (GDN_CONV_W - 1),
        out_shape=[
            jax.ShapeDtypeStruct((n_rows, GLA_V), BF16),
            jax.ShapeDtypeStruct((n_rows, GDN_V), BF16),
            jax.ShapeDtypeStruct((bsz, H, GLA_DK, GLA_DV), F32),
            jax.ShapeDtypeStruct((bsz, H, GDN_D, GDN_D), F32),
        ] + [jax.ShapeDtypeStruct((bsz, GDN_CONV_C), F32)] * (GDN_CONV_W - 1),
        scratch_shapes=[pltpu.VMEM((LANE, GLA_COLS), F32),
                        pltpu.VMEM((LANE, GDN_CONV_C + LANE), F32)],
        compiler_params=pltpu.CompilerParams(
            dimension_semantics=("arbitrary",), vmem_limit_bytes=VMEM_LIMIT),
        name="mix_sample",
    )(gla2d, gdn2d, *hists, sa, sb, cw, gna, gnb)


def _ffn_kernel(*refs, seq_tiles, has_hist, seq_len):
    nh = FFN_CONV_W - 1
    if has_hist:
        (x_ref, oa_ref, ob_ref, wo_ref, g2_ref, wu_ref, cw_ref, wd_ref, gf_ref) = refs[:9]
        st_refs = refs[9:9 + nh]
        y_ref = refs[9 + nh]
        last_refs = refs[10 + nh:10 + 2 * nh]
        u_scr = refs[10 + 2 * nh]
    else:
        (x_ref, oa_ref, ob_ref, wo_ref, g2_ref, wu_ref, cw_ref, wd_ref, gf_ref,
         y_ref, tail_ref, u_scr) = refs
    tm = x_ref.shape[0]

    if not has_hist:
        @pl.when(pl.program_id(0) % seq_tiles == 0)
        def _():
            u_scr[0:SUBLANE, :] = jnp.zeros((SUBLANE, 2 * D_FF), F32)

    x1 = (x_ref[...] + _dot(oa_ref[...].astype(BF16), wo_ref[0:GLA_V, :])
          + _dot(ob_ref[...].astype(BF16), wo_ref[GLA_V:GLA_V + GDN_V, :]))
    hn = _rms(x1, g2_ref[...]).astype(BF16)

    if has_hist:
        nseq = tm // seq_len
        pos = lax.broadcasted_iota(jnp.int32, (tm, FF_CK), 0) % seq_len
        r_ts = lax.broadcasted_iota(jnp.int32, (tm, nseq), 0)
        s_ts = lax.broadcasted_iota(jnp.int32, (tm, nseq), 1)
        spread = [(r_ts == s_ts * seq_len + p).astype(BF16) for p in range(nh)]
        r_st = lax.broadcasted_iota(jnp.int32, (nseq, tm), 1)
        s_st = lax.broadcasted_iota(jnp.int32, (nseq, tm), 0)
        gather = [(r_st == s_st * seq_len + seq_len - nh + j).astype(BF16) for j in range(nh)]
        hist = [_exact_rows(jnp.concatenate(spread[:k], axis=1),
                            jnp.concatenate([st_refs[nh + p - k][...] for p in range(k)], axis=0))
                for k in range(1, nh + 1)]

    def conv(u, off):
        cols = slice(off, off + FF_CK)
        if has_hist:
            m1, m2 = [jnp.where(pos >= k, pltpu.roll(u, k, 0), hist[k - 1][:, cols])
                      for k in range(1, nh + 1)]
            u_scr[:, cols] = u
        else:
            u = u_scr[SUBLANE:SUBLANE + tm, cols]
            m1 = u_scr[SUBLANE - 1:SUBLANE - 1 + tm, cols]
            m2 = u_scr[SUBLANE - 2:SUBLANE - 2 + tm, cols]
            tail = u_scr[tm:tm + SUBLANE, cols]
            tail_ref[0, :, cols] = tail
            u_scr[0:SUBLANE, cols] = tail
        return (m2 * cw_ref[0:1, cols] + m1 * cw_ref[1:2, cols] + u * cw_ref[2:3, cols])

    def up(c):
        ua = _dot(hn, wu_ref[:, c:c + FF_CK])
        ub = _dot(hn, wu_ref[:, D_FF + c:D_FF + c + FF_CK])
        if has_hist:
            return ua, ub
        u_scr[SUBLANE:SUBLANE + tm, c:c + FF_CK] = ua
        u_scr[SUBLANE:SUBLANE + tm, D_FF + c:D_FF + c + FF_CK] = ub
        return None, None

    acc = jnp.zeros((tm, D_MODEL), F32)
    ahead = FF_LOOKAHEAD
    steps = list(range(0, D_FF, FF_CK))
    pending = [up(c) for c in steps[:ahead]]
    for i, c in enumerate(steps):
        if i + ahead < len(steps):
            pending.append(up(steps[i + ahead]))
        ua, ub = pending.pop(0)
        a = conv(ua, c)
        b = conv(ub, D_FF + c)
        act = (_silu(a) * b).astype(BF16)
        acc = acc + _dot(act, wd_ref[c:c + FF_CK, :])
    y_ref[...] = _rms(x1 + acc, gf_ref[...])
    if has_hist:
        for j in range(nh):
            last_refs[j][...] = _exact_rows(gather[j], u_scr[...])


def _ffn(x2d, oa, ob, wo, g2, wu, cw, wd, gf, tm, seq_len, hist=None):
    n = x2d.shape[0]
    has_hist = hist is not None
    row = lambda i: (i, 0)
    const = lambda i: (0, 0)
    in_specs = [
        pl.BlockSpec((tm, D_MODEL), row),
        pl.BlockSpec((tm, GLA_V), row),
        pl.BlockSpec((tm, GDN_V), row),
        pl.BlockSpec((GLA_V + GDN_V, D_MODEL), const),
        pl.BlockSpec((1, D_MODEL), const),
        pl.BlockSpec((D_MODEL, 2 * D_FF), const),
        pl.BlockSpec((FFN_CONV_W, 2 * D_FF), const),
        pl.BlockSpec((D_FF, D_MODEL), const),
        pl.BlockSpec((1, D_MODEL), const),
    ]
    args = [x2d, oa, ob, wo, g2, wu, cw, wd, gf]
    out_specs = [pl.BlockSpec((tm, D_MODEL), row)]
    out_shape = [jax.ShapeDtypeStruct((n, D_MODEL), F32)]
    scratch = []
    if has_hist:
        assert tm % seq_len == 0
        seq_tiles = 1
        nseq = tm // seq_len
        assert len(hist) == FFN_CONV_W - 1 and seq_len >= FFN_CONV_W - 1
        in_specs += [pl.BlockSpec((nseq, 2 * D_FF), row)] * len(hist)
        args += list(hist)
        out_specs += [pl.BlockSpec((nseq, 2 * D_FF), row)] * len(hist)
        out_shape += [jax.ShapeDtypeStruct((n // seq_len, 2 * D_FF), F32)] * len(hist)
        scratch.append(pltpu.VMEM((tm, 2 * D_FF), F32))
    else:
        assert seq_len % tm == 0
        seq_tiles = seq_len // tm
        out_specs.append(pl.BlockSpec((1, SUBLANE, 2 * D_FF), lambda i: (i, 0, 0)))
        out_shape.append(jax.ShapeDtypeStruct((n // tm, SUBLANE, 2 * D_FF), F32))
        scratch.append(pltpu.VMEM((SUBLANE + tm, 2 * D_FF), F32))
    return pl.pallas_call(
        functools.partial(_ffn_kernel, seq_tiles=seq_tiles, has_hist=has_hist, seq_len=seq_len),
        grid=(n // tm,),
        in_specs=in_specs,
        out_specs=out_specs,
        out_shape=out_shape,
        scratch_shapes=scratch,
        compiler_params=pltpu.CompilerParams(
            dimension_semantics=("arbitrary",), vmem_limit_bytes=VMEM_LIMIT),
        name="ffn_sample" if has_hist else "ffn_prompt",
    )(*args)


def kernel(x_prompt, x_sample, state_gla, state_gdn, state_gdn_conv, state_ffn_conv, norm1_g, w_in, gla_w_a2, gla_b_a, gla_norm_g, gdn_conv_w, gdn_a_log, gdn_dt_bias, gdn_norm_g, w_out, norm2_g, w_up, ffn_conv_w, w_down, norm_f_g):
    assert w_in.shape[0] == 1, "single layer"
    bp, tp, _ = x_prompt.shape
    bs, ts, _ = x_sample.shape

    wa2 =jnp.pad(gla_w_a2[0], ((0, LANE - GLA_RANK), (0, 0))).astype(BF16)
    ba = gla_b_a[0][None, :]
    alog = jnp.pad(gdn_a_log[0], (H, LANE - 2 * H))[None, :]
    dtb = jnp.pad(gdn_dt_bias[0], (H, LANE - 2 * H))[None, :]
    g1 = norm1_g[0][None, :]
    g2 = norm2_g[0][None, :]
    gf = norm_f_g[None, :]
    gna = gla_norm_g[0][None, :]
    gnb = gdn_norm_g[0][None, :]
    cwb = gdn_conv_w[0]
    cwf = ffn_conv_w[0]
    wo = w_out[0].astype(BF16)
    wu = w_up[0].astype(BF16)
    wd = w_down[0].astype(BF16)

    xp = x_prompt.reshape(bp * tp, D_MODEL)
    tm_p = 512
    gla_p, gdn_p, qkv_tail, *w1 = _inproj(xp, g1, w_in[0], wa2, ba, alog, dtb, tm=tm_p, conv_w=cwb,
                                          seq_len=tp)
    oa_p, p_gla = _gla_prompt(gla_p, gna, bp, tp)
    ob_p, p_gdn = _gdn_prompt(gdn_p, gnb, bp, tp)
    y_p, ffn_tail = _ffn(xp, oa_p, ob_p, wo, g2, wu, cwf, wd, gf, tm=tm_p, seq_len=tp)
    y_prompt = y_p.reshape(bp, tp, D_MODEL)
    p_conv = qkv_tail.reshape(bp, tp // tm_p, SUBLANE, GDN_CONV_C)[:, -1, SUBLANE - (GDN_CONV_W - 1):, :]
    p_ffn = ffn_tail.reshape(bp, tp // tm_p, SUBLANE, 2 * D_FF)[:, -1, SUBLANE - (FFN_CONV_W - 1):, :]

    xs = x_sample.reshape(bs * ts, D_MODEL)
    gla_s, gdn_s = _inproj(xs, g1, w1, wa2, ba, alog, dtb, tm=256)
    oa_s, ob_s, s_gla, s_gdn, *conv_planes = _mix_sample(
        gla_s, gdn_s, [state_gdn_conv[0][:, j] for j in range(GDN_CONV_W - 1)],
        state_gla[0], state_gdn[0], cwb, gna, gnb, ts)
    y_s, *ffn_planes = _ffn(xs, oa_s, ob_s, wo, g2, wu, cwf, wd, gf, tm=128, seq_len=ts,
                            hist=[state_ffn_conv[0][:, j] for j in range(FFN_CONV_W - 1)])
    y_sample = y_s.reshape(bs, ts, D_MODEL)
    s_conv = jnp.stack(conv_planes, axis=1)
    s_ffn = jnp.stack(ffn_planes, axis=1)

    return (y_prompt, y_sample, p_gla[None], p_gdn[None], p_conv[None], p_ffn[None],
            s_gla[None], s_gdn[None], s_conv[None], s_ffn[None])
```

```python
import functools

import jax
import jax.numpy as jnp
from jax import lax
from jax.experimental import pallas as pl
from jax.experimental.pallas import tpu as pltpu

F32 = jnp.float32
BF16 = jnp.bfloat16

D_MODEL = 1024
H = 4
GLA_DK = 64
GLA_DV = 128
GLA_RANK = 16
GLA_TAU = 16.0
GDN_D = 128
GDN_CONV_W = 4
D_FF = 2816
FFN_CONV_W = 3
EPS = 1e-6

GLA_QK = H * GLA_DK
GLA_V = H * GLA_DV
GDN_QK = H * GDN_D
GDN_V = H * GDN_D
GDN_CONV_C = 2 * GDN_QK + GDN_V
IN_SIZES = (GLA_QK, GLA_QK, GLA_V, GLA_V, GLA_RANK, GDN_CONV_C, GDN_V, H, H)

LANE = 128
SUBLANE = 8
VMEM_LIMIT = 56 * 1024 * 1024

W_GLA = 2 * GLA_QK + 2 * GLA_V
W_GDN = GDN_CONV_C + GDN_V
GLA_COLS = W_GLA + GLA_QK
GDN_COLS = W_GDN + LANE

PROJ_CK = 512
CHUNK = 128
CHUNKS_PER_STEP = 8
FF_CK = 256
FF_LOOKAHEAD = 11


def _dot(a, b):
    return jnp.dot(a, b, preferred_element_type=F32)


def _dot_nt(a, b):
    return lax.dot_general(a, b, (((1,), (1,)), ((), ())), preferred_element_type=F32)


def _split2(x):
    hi = x.astype(BF16)
    lo = (x - hi.astype(F32)).astype(BF16)
    return hi, lo


def _dot3(a, b):
    ah, al = _split2(a)
    bh, bl = _split2(b)
    return _dot(jnp.concatenate([ah, ah, al], axis=1), jnp.concatenate([bh, bl, bh], axis=0))


def _dot3_many(a_list, b_list):
    lhs = [jnp.concatenate([ah, ah, al], axis=1) for ah, al in map(_split2, a_list)]
    rhs = [jnp.concatenate([bh, bl, bh], axis=0) for bh, bl in map(_split2, b_list)]
    return [_dot(x, y) for x, y in zip(lhs, rhs)]


def _exact_rows(sel_bf16, x):
    h1 = x.astype(BF16)
    r1 = x - h1.astype(F32)
    h2 = r1.astype(BF16)
    h3 = (r1 - h2.astype(F32)).astype(BF16)
    return _dot(jnp.concatenate([sel_bf16] * 3, axis=1), jnp.concatenate([h1, h2, h3], axis=0))


def _sigmoid(x):
    return 0.5 * jnp.tanh(0.5 * x) + 0.5


def _silu_of_double(h):
    return h * (jnp.tanh(h) + 1.0)


def _silu(x):
    return x * _sigmoid(x)


def _softplus(x):
    return jnp.maximum(x, 0.0) + jnp.log(1.0 + jnp.exp(-jnp.abs(x)))


def _rms(x, g):
    ms = jnp.mean(x * x, axis=-1, keepdims=True)
    return x * lax.rsqrt(ms + EPS) * g


def _gdn_qkv_activation(y, off):
    a = _silu_of_double(y)
    outs = []
    for j in range(0, a.shape[1], GDN_D):
        blk = a[:, j:j + GDN_D]
        if off + j < 2 * GDN_QK:
            blk = blk * lax.rsqrt(jnp.sum(blk * blk, axis=-1, keepdims=True) + EPS)
            if off + j < GDN_QK:
                blk = blk * (GDN_D ** -0.5)
        outs.append(blk)
    return jnp.concatenate(outs, axis=1)


def _inproj_kernel(*refs, seq_tiles):
    if seq_tiles is None:
        (x_ref, g1_ref, wa_ref, wb_ref, ws_ref, wa2_ref, ba_ref, alog_ref, dtb_ref,
         gla_ref, gdn_ref) = refs
    else:
        (x_ref, g1_ref, wa_ref, wb_ref, ws_ref, wa2_ref, ba_ref, alog_ref, dtb_ref, cw_ref,
         gla_ref, gdn_ref, tail_ref, xs_scr) = refs
    tm = x_ref.shape[0]
    hn = _rms(x_ref[...], g1_ref[...]).astype(BF16)
    small = _dot(hn, ws_ref[...])
    ck = PROJ_CK
    if seq_tiles is None:
        for c in range(0, W_GLA, ck):
            gla_ref[:, c:c + ck] = _dot(hn, wa_ref[:, c:c + ck])
        for c in range(0, W_GDN, ck):
            gdn_ref[:, c:c + ck] = _dot(hn, wb_ref[:, c:c + ck])
    else:
        @pl.when(pl.program_id(0) % seq_tiles == 0)
        def _():
            xs_scr[0:SUBLANE, :] = jnp.zeros((SUBLANE, GDN_CONV_C), F32)

        for c in range(0, GDN_CONV_C, ck):
            xs_scr[SUBLANE:SUBLANE + tm, c:c + ck] = _dot(hn, wb_ref[:, c:c + ck])
        gdn_ref[:, GDN_CONV_C:W_GDN] = _dot(hn, wb_ref[:, GDN_CONV_C:W_GDN])
        for c in range(0, GDN_CONV_C, ck):
            gla_ref[:, c:c + ck] = _dot(hn, wa_ref[:, c:c + ck])
            cols = slice(c, c + ck)
            y = xs_scr[SUBLANE:SUBLANE + tm, cols] * cw_ref[GDN_CONV_W - 1:GDN_CONV_W, cols]
            for kk in range(1, GDN_CONV_W):
                y = y + (xs_scr[SUBLANE - kk:SUBLANE - kk + tm, cols]
                         * cw_ref[GDN_CONV_W - 1 - kk:GDN_CONV_W - kk, cols])
            gdn_ref[:, cols] = _gdn_qkv_activation(y, c)
            tail = xs_scr[tm:tm + SUBLANE, cols]
            tail_ref[0, :, cols] = tail
            xs_scr[0:SUBLANE, cols] = tail
    xa = _dot(small[:, 0:LANE].astype(BF16), wa2_ref[...]) + ba_ref[...]
    gla_ref[:, W_GLA:GLA_COLS] = -_softplus(-xa) * (1.0 / GLA_TAU)
    bd = small[:, LANE:2 * LANE]
    lane = lax.broadcasted_iota(jnp.int32, bd.shape, 1)
    beta = _sigmoid(bd)
    g = -jnp.exp(alog_ref[...]) * _softplus(bd + dtb_ref[...])
    gdn_ref[:, W_GDN:GDN_COLS] = jnp.where(lane < H, beta, jnp.where(lane < 2 * H, g, 0.0))


def _inproj(x2d, g1, w_in_parts, wa2, ba, alog, dtb, tm, conv_w=None, seq_len=None):
    n = x2d.shape[0]
    const = lambda i: (0, 0)
    fused_conv = conv_w is not None
    in_specs = [
        pl.BlockSpec((tm, D_MODEL), lambda i: (i, 0)),
        pl.BlockSpec((1, D_MODEL), const),
        pl.BlockSpec((D_MODEL, W_GLA), const),
        pl.BlockSpec((D_MODEL, W_GDN), const),
        pl.BlockSpec((D_MODEL, 2 * LANE), const),
        pl.BlockSpec((LANE, GLA_QK), const),
        pl.BlockSpec((1, GLA_QK), const),
        pl.BlockSpec((1, LANE), const),
        pl.BlockSpec((1, LANE), const),
    ]
    args = [x2d, g1, *w_in_parts, wa2, ba, alog, dtb]
    out_specs = [
        pl.BlockSpec((tm, GLA_COLS), lambda i: (i, 0)),
        pl.BlockSpec((tm, GDN_COLS), lambda i: (i, 0)),
    ]
    out_shape = [
        jax.ShapeDtypeStruct((n, GLA_COLS), F32),
        jax.ShapeDtypeStruct((n, GDN_COLS), F32),
    ]
    scratch = []
    seq_tiles = None
    if fused_conv:
        assert seq_len % tm == 0
        seq_tiles = seq_len // tm
        in_specs.append(pl.BlockSpec((GDN_CONV_W, GDN_CONV_C), const))
        args.append(conv_w)
        out_specs.append(pl.BlockSpec((1, SUBLANE, GDN_CONV_C), lambda i: (i, 0, 0)))
        out_shape.append(jax.ShapeDtypeStruct((n // tm, SUBLANE, GDN_CONV_C), F32))
        scratch.append(pltpu.VMEM((SUBLANE + tm, GDN_CONV_C), F32))
    return pl.pallas_call(
        functools.partial(_inproj_kernel, seq_tiles=seq_tiles),
        grid=(n // tm,),
        in_specs=in_specs,
        out_specs=out_specs,
        out_shape=out_shape,
        scratch_shapes=scratch,
        compiler_params=pltpu.CompilerParams(
            dimension_semantics=("arbitrary" if fused_conv else "parallel",),
            vmem_limit_bytes=VMEM_LIMIT),
        name="inproj_conv" if fused_conv else "inproj",
    )(*args)


def _gla_chunk_kernel(gla_ref, gn_ref, o_ref, st_ref, h_scr):
    tc = CHUNK

    @pl.when(pl.program_id(1) == 0)
    def _():
        h_scr[...] = jnp.zeros_like(h_scr)

    row = lax.broadcasted_iota(jnp.int32, (tc, tc), 0)
    col = lax.broadcasted_iota(jnp.int32, (tc, tc), 1)
    causal = row >= col
    tri = causal.astype(BF16)
    lane_head = lax.broadcasted_iota(jnp.int32, (1, GLA_QK), 1) // GLA_DK
    rblk = lax.broadcasted_iota(jnp.int32, (GLA_QK, GLA_V), 0) // GLA_DK
    cblk = lax.broadcasted_iota(jnp.int32, (GLA_QK, GLA_V), 1) // GLA_DV
    gn = gn_ref[...]
    hbd = h_scr[...]

    chunks = range(CHUNKS_PER_STEP)
    rows = [slice(j * tc, (j + 1) * tc) for j in chunks]
    b = [_exact_rows(tri, gla_ref[rs, W_GLA:GLA_COLS]) for rs in rows]
    qi, qm, km, vb, klt, e_last = [], [], [], [], [], []
    for j in chunks:
        q = gla_ref[rows[j], 0:GLA_QK] * (GLA_DK ** -0.5)
        k = gla_ref[rows[j], GLA_QK:2 * GLA_QK]
        vb.append(gla_ref[rows[j], 2 * GLA_QK:2 * GLA_QK + GLA_V].astype(BF16))
        b_mid = b[j][tc // 2 - 1:tc // 2, :]
        qm.append((q * jnp.exp(b[j] - b_mid)).astype(BF16))
        km.append((k * jnp.exp(b_mid - b[j])).astype(BF16))
        qi.append((q * jnp.exp(b[j])).astype(BF16))
        bt = b[j].T
        b_last = bt[:, tc - 1:tc]
        klt.append((k.T * jnp.exp(b_last - bt)).astype(BF16))
        e_last.append(jnp.exp(b_last))
    scores = {(j, h): _dot_nt(jnp.where(lane_head == h, qm[j], jnp.zeros_like(qm[j])), km[j])
              for j in chunks for h in range(H)}
    scores = {p: jnp.where(causal, s, 0.0).astype(BF16) for p, s in scores.items()}
    o_intra = {(j, h): _dot(scores[j, h], vb[j][:, h * GLA_DV:(h + 1) * GLA_DV])
               for j in chunks for h in range(H)}
    upd = [jnp.where(rblk == cblk, _dot(klt[j], vb[j]), 0.0) for j in chunks]

    for j in chunks:
        rs = rows[j]
        o_inter = _dot(qi[j], hbd.astype(BF16))
        hbd = hbd * e_last[j] + upd[j]
        for h in range(H):
            sl = slice(h * GLA_DV, (h + 1) * GLA_DV)
            oh = o_intra[j, h] + o_inter[:, sl]
            gate = gla_ref[rs, 2 * GLA_QK + GLA_V + h * GLA_DV:2 * GLA_QK + GLA_V + (h + 1) * GLA_DV]
            o_ref[rs, sl] = (_rms(oh, gn) * _silu(gate)).astype(o_ref.dtype)

    h_scr[...] = hbd
    for h in range(H):
        st_ref[0, h] = hbd[h * GLA_DK:(h + 1) * GLA_DK, h * GLA_DV:(h + 1) * GLA_DV]


def _gla_prompt(gla2d, gn, bsz, t):
    step = CHUNKS_PER_STEP * CHUNK
    assert t % step == 0
    nc = t // step
    return pl.pallas_call(
        _gla_chunk_kernel,
        grid=(bsz, nc),
        in_specs=[
            pl.BlockSpec((step, GLA_COLS), lambda b, c: (b * nc + c, 0)),
            pl.BlockSpec((1, GLA_DV), lambda b, c: (0, 0)),
        ],
        out_specs=[
            pl.BlockSpec((step, GLA_V), lambda b, c: (b * nc + c, 0)),
            pl.BlockSpec((1, H, GLA_DK, GLA_DV), lambda b, c: (b, 0, 0, 0)),
        ],
        out_shape=[
            jax.ShapeDtypeStruct((bsz * t, GLA_V), BF16),
            jax.ShapeDtypeStruct((bsz, H, GLA_DK, GLA_DV), F32),
        ],
        scratch_shapes=[pltpu.VMEM((GLA_QK, GLA_V), F32)],
        compiler_params=pltpu.CompilerParams(
            dimension_semantics=("parallel", "arbitrary"), vmem_limit_bytes=VMEM_LIMIT),
        name="gla_prompt",
    )(gla2d, gn)


def _tri_inverse(l_strict, eye, order):
    n = eye.shape[0]
    refine = order > 2 * SUBLANE

    def dots(a_list, b_list):
        if not refine:
            return _dot3_many(a_list, b_list)
        a_list = [a.astype(BF16) for a in a_list]
        b_list = [b.astype(BF16) for b in b_list]
        return [_dot(a, b) for a, b in zip(a_list, b_list)]

    m = [-l for l in l_strict]
    p = [eye + x for x in m]
    m = dots(m, m)
    power = 2
    while 2 * power < order:
        pm = dots([jnp.concatenate([a, b], axis=0) for a, b in zip(p, m)], m)
        p = [a + x[:n] for a, x in zip(p, pm)]
        m = [x[n:] for x in pm]
        power *= 2
    t = [a + x for a, x in zip(p, dots(p, m))]
    if refine:
        r = [eye - a - x for a, x in zip(t, _dot3_many(l_strict, t))]
        t = [a + x for a, x in zip(t, dots(t, r))]
    return t


def _gdn_chunk_kernel(gdn_ref, gn_ref, o_ref, st_ref, h_scr):
    tc = CHUNK

    @pl.when(pl.program_id(1) == 0)
    def _():
        h_scr[...] = jnp.zeros_like(h_scr)

    row = lax.broadcasted_iota(jnp.int32, (tc, tc), 0)
    col = lax.broadcasted_iota(jnp.int32, (tc, tc), 1)
    causal = row >= col
    strict = row > col
    eye = (row == col).astype(F32)
    tri = causal.astype(BF16)
    gn = gn_ref[...]
    heads = range(H)
    sls = [slice(h * GDN_D, (h + 1) * GDN_D) for h in heads]
    hh = [h_scr[sls[h], :] for h in heads]

    probs = [(j, h) for j in range(CHUNKS_PER_STEP) for h in heads]
    rows = [slice(j * tc, (j + 1) * tc) for j in range(CHUNKS_PER_STEP)]
    bg = [gdn_ref[rs, W_GDN:GDN_COLS] for rs in rows]
    gc = [_exact_rows(tri, x) for x in bg]
    gt = [x.T for x in gc]
    kf, kb, qb, beta, gcol, decay = {}, {}, {}, {}, {}, {}
    for j, h in probs:
        kf[j, h] = gdn_ref[rows[j], GDN_QK + h * GDN_D:GDN_QK + (h + 1) * GDN_D]
        kb[j, h] = kf[j, h].astype(BF16)
        qb[j, h] = gdn_ref[rows[j], sls[h]].astype(BF16)
        beta[j, h] = bg[j][:, h:h + 1]
        gcol[j, h] = gc[j][:, H + h:H + h + 1]
        grow = gt[j][H + h:H + h + 1, :]
        decay[j, h] = jnp.where(causal, jnp.exp(jnp.where(causal, gcol[j, h] - grow, 0.0)), 0.0)
    qk_kk = {p: _dot_nt(jnp.concatenate([qb[p], kb[p]], axis=0), kb[p]) for p in probs}
    tinv = dict(zip(probs, _tri_inverse(
        [jnp.where(strict, beta[p] * qk_kk[p][tc:] * decay[p], 0.0) for p in probs], eye, tc)))

    for j in range(CHUNKS_PER_STEP):
        rs = rows[j]
        kq_h = [_dot(jnp.concatenate([kb[j, h], qb[j, h]], axis=0), hh[h].astype(BF16))
                for h in heads]
        eg = [jnp.exp(gcol[j, h]) for h in heads]
        rhs = [beta[j, h] * (gdn_ref[rs, 2 * GDN_QK + h * GDN_D:2 * GDN_QK + (h + 1) * GDN_D]
                             - eg[h] * kq_h[h][:tc]) for h in heads]
        ub = [u.astype(BF16) for u in _dot3_many([tinv[j, h] for h in heads], rhs)]
        ou = []
        for h in heads:
            qk = (qk_kk[j, h][:tc] * decay[j, h]).astype(BF16)
            g_last = gcol[j, h][tc - 1:tc, :]
            kd = (kf[j, h] * jnp.exp(g_last - gcol[j, h])).T.astype(BF16)
            ou.append(_dot(jnp.concatenate([qk, kd], axis=0), ub[h]))
        for h in heads:
            o = eg[h] * kq_h[h][tc:] + ou[h][:tc]
            hh[h] = jnp.exp(gcol[j, h][tc - 1:tc, :]) * hh[h] + ou[h][tc:]
            z = gdn_ref[rs, GDN_CONV_C + h * GDN_D:GDN_CONV_C + (h + 1) * GDN_D]
            o_ref[rs, sls[h]] = (_rms(o, gn) * _silu(z)).astype(o_ref.dtype)

    for h in heads:
        h_scr[sls[h], :] = hh[h]
        st_ref[0, h] = hh[h]


def _gdn_prompt(gdn2d, gn, bsz, t):
    step = CHUNKS_PER_STEP * CHUNK
    assert t % step == 0
    nc = t // step
    return pl.pallas_call(
        _gdn_chunk_kernel,
        grid=(bsz, nc),
        in_specs=[
            pl.BlockSpec((step, GDN_COLS), lambda b, c: (b * nc + c, 0)),
            pl.BlockSpec((1, GDN_D), lambda b, c: (0, 0)),
        ],
        out_specs=[
            pl.BlockSpec((step, GDN_V), lambda b, c: (b * nc + c, 0)),
            pl.BlockSpec((1, H, GDN_D, GDN_D), lambda b, c: (b, 0, 0, 0)),
        ],
        out_shape=[
            jax.ShapeDtypeStruct((bsz * t, GDN_V), BF16),
            jax.ShapeDtypeStruct((bsz, H, GDN_D, GDN_D), F32),
        ],
        scratch_shapes=[pltpu.VMEM((H * GDN_D, GDN_D), F32)],
        compiler_params=pltpu.CompilerParams(
            dimension_semantics=("parallel", "arbitrary"), vmem_limit_bytes=VMEM_LIMIT),
        name="gdn_prompt",
    )(gdn2d, gn)


SEQ_PER_STEP = 16


def _mix_sample_kernel(gla_ref, gdn_ref, h1_ref, h2_ref, h3_ref, sa_ref, sb_ref, cw_ref, gna_ref, gnb_ref,
                       oa_ref, ob_ref, sta_ref, stb_ref, cv1_ref, cv2_ref, cv3_ref, a_scr, b_scr, *, t):
    n = SEQ_PER_STEP
    rows = n * t
    tc = LANE
    spg = SUBLANE // t

    @pl.when(pl.program_id(0) == 0)
    def _():
        a_scr[...] = jnp.zeros_like(a_scr)
        b_scr[...] = jnp.zeros_like(b_scr)

    nh = GDN_CONV_W - 1
    st_refs = (h1_ref, h2_ref, h3_ref)
    cv_refs = (cv1_ref, cv2_ref, cv3_ref)
    x = gdn_ref[:, 0:GDN_CONV_C]
    pos = lax.broadcasted_iota(jnp.int32, x.shape, 0) % t
    r_ts = lax.broadcasted_iota(jnp.int32, (rows, n), 0)
    s_ts = lax.broadcasted_iota(jnp.int32, (rows, n), 1)
    spread = [(r_ts == s_ts * t + p).astype(BF16) for p in range(nh)]
    r_st = lax.broadcasted_iota(jnp.int32, (n, rows), 1)
    s_st = lax.broadcasted_iota(jnp.int32, (n, rows), 0)
    y = x * cw_ref[GDN_CONV_W - 1:GDN_CONV_W, :]
    for kk in range(1, GDN_CONV_W):
        hist = _exact_rows(jnp.concatenate(spread[:kk], axis=1),
                           jnp.concatenate([st_refs[nh + p - kk][...] for p in range(kk)], axis=0))
        m = jnp.where(pos >= kk, pltpu.roll(x, kk, 0), hist)
        y = y + m * cw_ref[GDN_CONV_W - 1 - kk:GDN_CONV_W - kk, :]
    for j in range(nh):
        cv_refs[j][...] = _exact_rows((r_st == s_st * t + t - nh + j).astype(BF16), x)
    b_scr[0:rows, 0:GDN_CONV_C] = _silu(y)
    b_scr[0:rows, GDN_CONV_C:GDN_CONV_C + LANE] = gdn_ref[:, W_GDN:GDN_COLS]
    a_scr[0:rows, :] = gla_ref[...]

    row = lax.broadcasted_iota(jnp.int32, (tc, tc), 0)
    col = lax.broadcasted_iota(jnp.int32, (tc, tc), 1)
    same = (row // t) == (col // t)
    causal = same & (row >= col)
    strict = same & (row > col)
    eye = (row == col).astype(F32)
    tri = causal.astype(BF16)
    last = (col == (row // t) * t + (t - 1)).astype(BF16)
    lane_seq = lax.broadcasted_iota(jnp.int32, (1, tc), 1) // t
    sub = lax.broadcasted_iota(jnp.int32, (SUBLANE, LANE), 0)

    def group_of(x, s):
        g = s // spg
        return x[g * SUBLANE:(g + 1) * SUBLANE]

    def pick_rows(res, off):
        groups = []
        for g in range(rows // SUBLANE):
            piece = res[g * spg + spg - 1][off:off + SUBLANE]
            for j in reversed(range(spg - 1)):
                piece = jnp.where(sub < (j + 1) * t, res[g * spg + j][off:off + SUBLANE], piece)
            groups.append(piece)
        return jnp.concatenate(groups, axis=0)

    q = a_scr[:, 0:GLA_QK] * (GLA_DK ** -0.5)
    k = a_scr[:, GLA_QK:2 * GLA_QK]
    vb = a_scr[:, 2 * GLA_QK:2 * GLA_QK + GLA_V].astype(BF16)
    b = _exact_rows(tri, a_scr[:, W_GLA:GLA_COLS])
    b_end = _exact_rows(last, b)
    qe_f = q * jnp.exp(b)
    qe = qe_f.astype(BF16)
    ke = (k * jnp.exp(-b)).astype(BF16)
    klt = (k * jnp.exp(b_end - b)).T.astype(BF16)
    bt = b.T
    lane_head = lax.broadcasted_iota(jnp.int32, (1, GLA_QK), 1) // GLA_DK
    qmask = [jnp.where(lane_head == h, qe, jnp.zeros_like(qe)) for h in range(H)]
    scores = [_dot_nt(qmask[h], ke) for h in range(H)]
    scores = [jnp.where(causal, s, 0.0).astype(BF16) for s in scores]
    o_intra = [_dot(scores[h], vb[:, h * GLA_DV:(h + 1) * GLA_DV]) for h in range(H)]
    hs = [jnp.concatenate([sa_ref[s, h] for h in range(H)], axis=0) for s in range(n)]
    lhs_q = [jnp.concatenate([jnp.where(lane_head == h, group_of(qe_f, s), 0.0) for h in range(H)],
                             axis=0).astype(BF16) for s in range(0, n, spg)]
    res = [_dot(lhs_q[s // spg], hs[s].astype(BF16)) for s in range(n)]
    upd = [_dot(jnp.where(lane_seq == s, klt, jnp.zeros_like(klt)), vb) for s in range(n)]
    gna = gna_ref[...]
    for h in range(H):
        sl = slice(h * GLA_DV, (h + 1) * GLA_DV)
        oh = o_intra[h][0:rows] + pick_rows(res, h * SUBLANE)
        gate = gla_ref[:, 2 * GLA_QK + GLA_V + h * GLA_DV:2 * GLA_QK + GLA_V + (h + 1) * GLA_DV]
        oa_ref[:, sl] = (_rms(oh, gna) * _silu(gate)).astype(oa_ref.dtype)
    for s in range(n):
        e_col = jnp.exp(bt[:, s * t + t - 1:s * t + t])
        diag = jnp.concatenate(
            [upd[s][h * GLA_DK:(h + 1) * GLA_DK, h * GLA_DV:(h + 1) * GLA_DV] for h in range(H)], axis=0)
        new = hs[s] * e_col + diag
        for h in range(H):
            sta_ref[s, h] = new[h * GLA_DK:(h + 1) * GLA_DK]

    bg = b_scr[:, GDN_CONV_C:GDN_CONV_C + LANE]
    gc = _exact_rows(tri, bg)
    gl = _exact_rows(last, gc)
    gt = gc.T
    heads = range(H)
    sls = [slice(h * GDN_D, (h + 1) * GDN_D) for h in heads]
    kf, qf, kb, qb, vf, beta, gcol, decay = [], [], [], [], [], [], [], []
    for h in heads:
        qh = b_scr[:, sls[h]]
        kh = b_scr[:, GDN_QK + h * GDN_D:GDN_QK + (h + 1) * GDN_D]
        vf.append(b_scr[:, 2 * GDN_QK + h * GDN_D:2 * GDN_QK + (h + 1) * GDN_D])
        qh = qh * lax.rsqrt(jnp.sum(qh * qh, axis=-1, keepdims=True) + EPS) * (GDN_D ** -0.5)
        kh = kh * lax.rsqrt(jnp.sum(kh * kh, axis=-1, keepdims=True) + EPS)
        kf.append(kh)
        qf.append(qh)
        kb.append(kh.astype(BF16))
        qb.append(qh.astype(BF16))
        beta.append(bg[:, h:h + 1])
        gcol.append(gc[:, H + h:H + h + 1])
        grow = gt[H + h:H + h + 1, :]
        decay.append(jnp.where(causal, jnp.exp(jnp.where(causal, gcol[h] - grow, 0.0)), 0.0))
    qk_kk = [_dot_nt(jnp.concatenate([qb[h], kb[h]], axis=0), kb[h]) for h in heads]
    hsb = [[sb_ref[s, h] for h in heads] for s in range(n)]
    kq_lhs = [[jnp.concatenate([group_of(kf[h], s), group_of(qf[h], s)], axis=0).astype(BF16)
               for s in range(0, n, spg)] for h in heads]
    kq = [[_dot(kq_lhs[h][s // spg], hsb[s][h].astype(BF16)) for s in range(n)]
          for h in heads]
    tinv = _tri_inverse([jnp.where(strict, beta[h] * qk_kk[h][tc:] * decay[h], 0.0) for h in heads], eye, t)
    pad_rows = jnp.zeros((tc - rows, GDN_D), F32)
    eg = [jnp.exp(gcol[h]) for h in heads]
    k_h = [jnp.concatenate([pick_rows(kq[h], 0), pad_rows], axis=0) for h in heads]
    rhs = [beta[h] * (vf[h] - eg[h] * k_h[h]) for h in heads]
    ub = [u.astype(BF16) for u in _dot3_many(tinv, rhs)]
    qku = [_dot((qk_kk[h][:tc] * decay[h]).astype(BF16), ub[h]) for h in heads]
    kdt = [(kf[h] * jnp.exp(gl[:, H + h:H + h + 1] - gcol[h])).T.astype(BF16) for h in heads]
    updb = [[_dot(jnp.where(lane_seq == s, kdt[h], jnp.zeros_like(kdt[h])), ub[h]) for s in range(n)]
            for h in heads]
    gnb = gnb_ref[...]
    for h in heads:
        o = eg[h][0:rows] * pick_rows(kq[h], SUBLANE) + qku[h][0:rows]
        z = gdn_ref[:, GDN_CONV_C + h * GDN_D:GDN_CONV_C + (h + 1) * GDN_D]
        ob_ref[:, sls[h]] = (_rms(o, gnb) * _silu(z)).astype(ob_ref.dtype)
        for s in range(n):
            e_last = jnp.exp(gl[s * t:s * t + 1, H + h:H + h + 1])
            stb_ref[s, h] = e_last * hsb[s][h] + updb[h][s]


def _mix_sample(gla2d, gdn2d, hists, sa, sb, cw, gna, gnb, t):
    n_rows = gla2d.shape[0]
    rows = SEQ_PER_STEP * t
    assert SUBLANE % t == 0 and rows <= LANE and n_rows % rows == 0
    bsz = n_rows // t
    r2 = lambda i: (i, 0)
    i4 = lambda i: (i, 0, 0, 0)
    c2 = lambda i: (0, 0)
    return pl.pallas_call(
        functools.partial(_mix_sample_kernel, t=t),
        grid=(n_rows // rows,),
        in_specs=[
            pl.BlockSpec((rows, GLA_COLS), r2),
            pl.BlockSpec((rows, GDN_COLS), r2),
            pl.BlockSpec((SEQ_PER_STEP, GDN_CONV_C), r2),
            pl.BlockSpec((SEQ_PER_STEP, GDN_CONV_C), r2),
            pl.BlockSpec((SEQ_PER_STEP, GDN_CONV_C), r2),
            pl.BlockSpec((SEQ_PER_STEP, H, GLA_DK, GLA_DV), i4),
            pl.BlockSpec((SEQ_PER_STEP, H, GDN_D, GDN_D), i4),
            pl.BlockSpec((GDN_CONV_W, GDN_CONV_C), c2),
            pl.BlockSpec((1, GLA_DV), c2),
            pl.BlockSpec((1, GDN_D), c2),
        ],
        out_specs=[
            pl.BlockSpec((rows, GLA_V), r2),
            pl.BlockSpec((rows, GDN_V), r2),
            pl.BlockSpec((SEQ_PER_STEP, H, GLA_DK, GLA_DV), i4),
            pl.BlockSpec((SEQ_PER_STEP, H, GDN_D, GDN_D), i4),
        ] + [pl.BlockSpec((SEQ_PER_STEP, GDN_CONV_C), r2)] * (GDN_CONV_W - 1),
        out_shape=[
            jax.ShapeDtypeStruct((n_rows, GLA_V), BF16),
            jax.ShapeDtypeStruct((n_rows, GDN_V), BF16),
            jax.ShapeDtypeStruct((bsz, H, GLA_DK, GLA_DV), F32),
            jax.ShapeDtypeStruct((bsz, H, GDN_D, GDN_D), F32),
        ] + [jax.ShapeDtypeStruct((bsz, GDN_CONV_C), F32)] * (GDN_CONV_W - 1),
        scratch_shapes=[pltpu.VMEM((LANE, GLA_COLS), F32),
                        pltpu.VMEM((LANE, GDN_CONV_C + LANE), F32)],
        compiler_params=pltpu.CompilerParams(
            dimension_semantics=("arbitrary",), vmem_limit_bytes=VMEM_LIMIT),
        name="mix_sample",
    )(gla2d, gdn2d, *hists, sa, sb, cw, gna, gnb)


def _ffn_kernel(*refs, seq_tiles, has_hist, seq_len):
    nh = FFN_CONV_W - 1
    if has_hist:
        (x_ref, oa_ref, ob_ref, wo_ref, g2_ref, wu_ref, cw_ref, wd_ref, gf_ref) = refs[:9]
        st_refs = refs[9:9 + nh]
        y_ref = refs[9 + nh]
        last_refs = refs[10 + nh:10 + 2 * nh]
        u_scr = refs[10 + 2 * nh]
    else:
        (x_ref, oa_ref, ob_ref, wo_ref, g2_ref, wu_ref, cw_ref, wd_ref, gf_ref,
         y_ref, tail_ref, u_scr) = refs
    tm = x_ref.shape[0]

    if not has_hist:
        @pl.when(pl.program_id(0) % seq_tiles == 0)
        def _():
            u_scr[0:SUBLANE, :] = jnp.zeros((SUBLANE, 2 * D_FF), F32)

    x1 = (x_ref[...] + _dot(oa_ref[...].astype(BF16), wo_ref[0:GLA_V, :])
          + _dot(ob_ref[...].astype(BF16), wo_ref[GLA_V:GLA_V + GDN_V, :]))
    hn = _rms(x1, g2_ref[...]).astype(BF16)

    if has_hist:
        nseq = tm // seq_len
        pos = lax.broadcasted_iota(jnp.int32, (tm, FF_CK), 0) % seq_len
        r_ts = lax.broadcasted_iota(jnp.int32, (tm, nseq), 0)
        s_ts = lax.broadcasted_iota(jnp.int32, (tm, nseq), 1)
        spread = [(r_ts == s_ts * seq_len + p).astype(BF16) for p in range(nh)]
        r_st = lax.broadcasted_iota(jnp.int32, (nseq, tm), 1)
        s_st = lax.broadcasted_iota(jnp.int32, (nseq, tm), 0)
        gather = [(r_st == s_st * seq_len + seq_len - nh + j).astype(BF16) for j in range(nh)]
        hist = [_exact_rows(jnp.concatenate(spread[:k], axis=1),
                            jnp.concatenate([st_refs[nh + p - k][...] for p in range(k)], axis=0))
                for k in range(1, nh + 1)]

    def conv(u, off):
        cols = slice(off, off + FF_CK)
        if has_hist:
            m1, m2 = [jnp.where(pos >= k, pltpu.roll(u, k, 0), hist[k - 1][:, cols])
                      for k in range(1, nh + 1)]
            u_scr[:, cols] = u
        else:
            u = u_scr[SUBLANE:SUBLANE + tm, cols]
            m1 = u_scr[SUBLANE - 1:SUBLANE - 1 + tm, cols]
            m2 = u_scr[SUBLANE - 2:SUBLANE - 2 + tm, cols]
            tail = u_scr[tm:tm + SUBLANE, cols]
            tail_ref[0, :, cols] = tail
            u_scr[0:SUBLANE, cols] = tail
        return (m2 * cw_ref[0:1, cols] + m1 * cw_ref[1:2, cols] + u * cw_ref[2:3, cols])

    def up(c):
        ua = _dot(hn, wu_ref[:, c:c + FF_CK])
        ub = _dot(hn, wu_ref[:, D_FF + c:D_FF + c + FF_CK])
        if has_hist:
            return ua, ub
        u_scr[SUBLANE:SUBLANE + tm, c:c + FF_CK] = ua
        u_scr[SUBLANE:SUBLANE + tm, D_FF + c:D_FF + c + FF_CK] = ub
        return None, None

    acc = jnp.zeros((tm, D_MODEL), F32)
    ahead = FF_LOOKAHEAD
    steps = list(range(0, D_FF, FF_CK))
    pending = [up(c) for c in steps[:ahead]]
    for i, c in enumerate(steps):
        if i + ahead < len(steps):
            pending.append(up(steps[i + ahead]))
        ua, ub = pending.pop(0)
        a = conv(ua, c)
        b = conv(ub, D_FF + c)
        act = (_silu_of_double(a) * b).astype(BF16)
        acc = acc + _dot(act, wd_ref[c:c + FF_CK, :])
    y_ref[...] = _rms(x1 + acc, gf_ref[...])
    if has_hist:
        for j in range(nh):
            last_refs[j][...] = _exact_rows(gather[j], u_scr[...])


def _ffn(x2d, oa, ob, wo, g2, wu, cw, wd, gf, tm, seq_len, hist=None):
    n = x2d.shape[0]
    has_hist = hist is not None
    row = lambda i: (i, 0)
    const = lambda i: (0, 0)
    in_specs = [
        pl.BlockSpec((tm, D_MODEL), row),
        pl.BlockSpec((tm, GLA_V), row),
        pl.BlockSpec((tm, GDN_V), row),
        pl.BlockSpec((GLA_V + GDN_V, D_MODEL), const),
        pl.BlockSpec((1, D_MODEL), const),
        pl.BlockSpec((D_MODEL, 2 * D_FF), const),
        pl.BlockSpec((FFN_CONV_W, 2 * D_FF), const),
        pl.BlockSpec((D_FF, D_MODEL), const),
        pl.BlockSpec((1, D_MODEL), const),
    ]
    args = [x2d, oa, ob, wo, g2, wu, cw, wd, gf]
    out_specs = [pl.BlockSpec((tm, D_MODEL), row)]
    out_shape = [jax.ShapeDtypeStruct((n, D_MODEL), F32)]
    scratch = []
    if has_hist:
        assert tm % seq_len == 0
        seq_tiles = 1
        nseq = tm // seq_len
        assert len(hist) == FFN_CONV_W - 1 and seq_len >= FFN_CONV_W - 1
        in_specs += [pl.BlockSpec((nseq, 2 * D_FF), row)] * len(hist)
        args += list(hist)
        out_specs += [pl.BlockSpec((nseq, 2 * D_FF), row)] * len(hist)
        out_shape += [jax.ShapeDtypeStruct((n // seq_len, 2 * D_FF), F32)] * len(hist)
        scratch.append(pltpu.VMEM((tm, 2 * D_FF), F32))
    else:
        assert seq_len % tm == 0
        seq_tiles = seq_len // tm
        out_specs.append(pl.BlockSpec((1, SUBLANE, 2 * D_FF), lambda i: (i, 0, 0)))
        out_shape.append(jax.ShapeDtypeStruct((n // tm, SUBLANE, 2 * D_FF), F32))
        scratch.append(pltpu.VMEM((SUBLANE + tm, 2 * D_FF), F32))
    return pl.pallas_call(
        functools.partial(_ffn_kernel, seq_tiles=seq_tiles, has_hist=has_hist, seq_len=seq_len),
        grid=(n // tm,),
        in_specs=in_specs,
        out_specs=out_specs,
        out_shape=out_shape,
        scratch_shapes=scratch,
        compiler_params=pltpu.CompilerParams(
            dimension_semantics=("arbitrary",), vmem_limit_bytes=VMEM_LIMIT),
        name="ffn_sample" if has_hist else "ffn_prompt",
    )(*args)


def _pad_cols(a, width):
    return jnp.pad(a, ((0, 0), (0, width - a.shape[1])))


def kernel(x_prompt, x_sample, state_gla, state_gdn, state_gdn_conv, state_ffn_conv, norm1_g, w_in, gla_w_a2, gla_b_a, gla_norm_g, gdn_conv_w, gdn_a_log, gdn_dt_bias, gdn_norm_g, w_out, norm2_g, w_up, ffn_conv_w, w_down, norm_f_g):
    assert w_in.shape[0] == 1, "single layer"
    bp, tp, _ = x_prompt.shape
    bs, ts, _ = x_sample.shape

    offs = [0]
    for s in IN_SIZES:
        offs.append(offs[-1] + s)
    wi = w_in[0]
    w1 = (wi[:, offs[0]:offs[4]].astype(BF16),
          wi[:, offs[5]:offs[7]].astype(BF16),
          jnp.concatenate([_pad_cols(wi[:, offs[4]:offs[5]], LANE),
                           _pad_cols(wi[:, offs[7]:offs[9]], LANE)], axis=1).astype(BF16))
    wa2 = jnp.pad(gla_w_a2[0], ((0, LANE - GLA_RANK), (0, 0))).astype(BF16)
    ba = gla_b_a[0][None, :]
    alog = jnp.pad(gdn_a_log[0], (H, LANE - 2 * H))[None, :]
    dtb = jnp.pad(gdn_dt_bias[0], (H, LANE - 2 * H))[None, :]
    g1 = norm1_g[0][None, :]
    g2 = norm2_g[0][None, :]
    gf = norm_f_g[None, :]
    gna = gla_norm_g[0][None, :]
    gnb = gdn_norm_g[0][None, :]
    cwb = gdn_conv_w[0]
    cwb_half = 0.5 * cwb
    cwf = jnp.concatenate([0.5 * ffn_conv_w[0][:, :D_FF], ffn_conv_w[0][:, D_FF:]], axis=1)
    wo = w_out[0].astype(BF16)
    wu = w_up[0].astype(BF16)
    wd = w_down[0].astype(BF16)

    xp = x_prompt.reshape(bp * tp, D_MODEL)
    tm_p = 512
    gla_p, gdn_p, qkv_tail = _inproj(xp, g1, w1, wa2, ba, alog, dtb, tm=tm_p, conv_w=cwb_half, seq_len=tp)
    oa_p, p_gla = _gla_prompt(gla_p, gna, bp, tp)
    ob_p, p_gdn = _gdn_prompt(gdn_p, gnb, bp, tp)
    y_p, ffn_tail = _ffn(xp, oa_p, ob_p, wo, g2, wu, cwf, wd, gf, tm=tm_p, seq_len=tp)
    y_prompt = y_p.reshape(bp, tp, D_MODEL)
    p_conv = qkv_tail.reshape(bp, tp // tm_p, SUBLANE, GDN_CONV_C)[:, -1, SUBLANE - (GDN_CONV_W - 1):, :]
    p_ffn = ffn_tail.reshape(bp, tp // tm_p, SUBLANE, 2 * D_FF)[:, -1, SUBLANE - (FFN_CONV_W - 1):, :]

    xs = x_sample.reshape(bs * ts, D_MODEL)
    gla_s, gdn_s = _inproj(xs, g1, w1, wa2, ba, alog, dtb, tm=256)
    oa_s, ob_s, s_gla, s_gdn, *conv_planes = _mix_sample(
        gla_s, gdn_s, [state_gdn_conv[0][:, j] for j in range(GDN_CONV_W - 1)],
        state_gla[0], state_gdn[0], cwb, gna, gnb, ts)
    y_s, *ffn_planes = _ffn(xs, oa_s, ob_s, wo, g2, wu, cwf, wd, gf, tm=128, seq_len=ts,
                            hist=[state_ffn_conv[0][:, j] for j in range(FFN_CONV_W - 1)])
    y_sample = y_s.reshape(bs, ts, D_MODEL)
    s_conv = jnp.stack(conv_planes, axis=1)
    s_ffn = jnp.stack(ffn_planes, axis=1)

    return (y_prompt, y_sample, p_gla[None], p_gdn[None], p_conv[None], p_ffn[None],
            s_gla[None], s_gdn[None], s_conv[None], s_ffn[None])
```
